```python
import jax, jax.numpy as jnp
from jax import lax
import numpy as np

D_MODEL = 1024
BATCH = 8
SEQ = 2048
DEPTH = 1
DEC_BATCH = 4
DEC_SEQ = 8192
PAST_LEN = 128

N_ATTN_HEADS = 8
N_KV_HEADS = 2
ATTN_HEAD_DIM = 64
WINDOW = 128
ATTN_BLOCK = 128
ROPE_THETA = 10000.0
N_GLA_HEADS = 4
GLA_KEY_DIM = 64
GLA_VALUE_DIM = 128
GLA_GATE_RANK = 16
GLA_GATE_NORMALIZER = 16.0
GLA_CHUNK = 64
D_FF = 2816
EPS = 1e-6

ATTN_Q = N_ATTN_HEADS * ATTN_HEAD_DIM
ATTN_KV = N_KV_HEADS * ATTN_HEAD_DIM
GLA_QK = N_GLA_HEADS * GLA_KEY_DIM
GLA_V = N_GLA_HEADS * GLA_VALUE_DIM
MIX_WIDTH = ATTN_Q + GLA_V
IN_SPLIT_OFFSETS = [ATTN_Q, ATTN_Q + ATTN_KV, ATTN_Q + 2 * ATTN_KV, ATTN_Q + 2 * ATTN_KV + GLA_QK, ATTN_Q + 2 * ATTN_KV + 2 * GLA_QK, ATTN_Q + 2 * ATTN_KV + 2 * GLA_QK + GLA_V, ATTN_Q + 2 * ATTN_KV + 2 * GLA_QK + 2 * GLA_V, ATTN_Q + 2 * ATTN_KV + 2 * GLA_QK + 2 * GLA_V + GLA_GATE_RANK]
IN_PROJ_WIDTH = ATTN_Q + 2 * ATTN_KV + 2 * GLA_QK + 2 * GLA_V + 2 * GLA_GATE_RANK

kernel_name = 'hymba_swa_gla_macaron_encoder'


def rmsnorm(x, gain):
    xf = x.astype(jnp.float32)
    y = xf * lax.rsqrt(jnp.mean(xf * xf, axis=-1, keepdims=True) + EPS)
    return (y * gain.astype(jnp.float32)).astype(x.dtype)


def swiglu(h, w_gate, w_up, w_down):
    return (jax.nn.silu(h @ w_gate) * (h @ w_up)) @ w_down


def rotary(x):
    T, d = x.shape[1], x.shape[-1]
    half = d // 2
    inv_freq = ROPE_THETA ** (-jnp.arange(half, dtype=jnp.float32) / half)
    ang = jnp.arange(T, dtype=jnp.float32)[:, None] * inv_freq[None, :]
    cos = jnp.cos(ang)[None, :, None, :]
    sin = jnp.sin(ang)[None, :, None, :]
    xf = x.astype(jnp.float32)
    x1, x2 = xf[..., :half], xf[..., half:]
    return jnp.concatenate([x1 * cos - x2 * sin, x2 * cos + x1 * sin], axis=-1).astype(x.dtype)


def windowed_gqa_attention(q, k, v, sink):
    B, T, Hq, d = q.shape
    nb = T // ATTN_BLOCK
    G = Hq // N_KV_HEADS
    pad = ((0, 0), (ATTN_BLOCK, ATTN_BLOCK), (0, 0), (0, 0))
    kp = jnp.pad(k, pad).reshape(B, nb + 2, ATTN_BLOCK, N_KV_HEADS, d)
    vp = jnp.pad(v, pad).reshape(B, nb + 2, ATTN_BLOCK, N_KV_HEADS, d)
    kw = jnp.concatenate([kp[:, :-2], kp[:, 1:-1], kp[:, 2:]], axis=2)
    vw = jnp.concatenate([vp[:, :-2], vp[:, 1:-1], vp[:, 2:]], axis=2)
    qb = q.reshape(B, nb, ATTN_BLOCK, N_KV_HEADS, G, d)
    s = jnp.einsum('bnqhgd,bnkhd->bnhgqk', qb, kw).astype(jnp.float32) * (d ** -0.5)
    i = jnp.arange(ATTN_BLOCK)[:, None]
    j = jnp.arange(3 * ATTN_BLOCK)[None, :]
    rel = j - ATTN_BLOCK - i
    kpos = (jnp.arange(nb)[:, None, None] - 1) * ATTN_BLOCK + j[None]
    mask = (jnp.abs(rel)[None] <= WINDOW) & (kpos >= 0) & (kpos < T)
    mask = mask[None, :, None, None, :, :]
    s = jnp.where(mask, s, -1e30)
    sink_b = sink.astype(jnp.float32).reshape(N_KV_HEADS, G)[None, None, :, :, None, None]
    m = jnp.maximum(jnp.max(s, axis=-1, keepdims=True), sink_b)
    p = jnp.where(mask, jnp.exp(s - m), 0.0)
    denom = jnp.sum(p, axis=-1, keepdims=True) + jnp.exp(sink_b - m)
    probs = (p / denom).astype(v.dtype)
    o = jnp.einsum('bnhgqk,bnkhd->bnqhgd', probs, vw)
    return o.reshape(B, T, Hq * d)


def gla_chunked(q, k, v, log_a, strict):
    B, H, T, dk = q.shape
    dv = v.shape[-1]
    C = GLA_CHUNK
    n = T // C
    q = q.reshape(B, H, n, C, dk)
    k = k.reshape(B, H, n, C, dk)
    v = v.reshape(B, H, n, C, dv)
    b = jnp.cumsum(log_a.reshape(B, H, n, C, dk), axis=3)
    b_last = b[:, :, :, -1:, :]
    q_e = q * jnp.exp(b)
    k_e = k * jnp.exp(-b)
    tri = jnp.tril(jnp.ones((C, C), jnp.float32), k=-1 if strict else 0)
    A = jnp.einsum('bhnid,bhnjd->bhnij', q_e, k_e) * tri
    o_intra = jnp.einsum('bhnij,bhnjv->bhniv', A, v)
    k_s = k * jnp.exp(b_last - b)
    U = jnp.einsum('bhnjd,bhnjv->bhndv', k_s, v)
    decay = jnp.exp(b_last[:, :, :, 0, :])

    def step(S, inp):
        dec, u = inp
        return dec[..., None] * S + u, S

    S0 = jnp.zeros((B, H, dk, dv), jnp.float32)
    _, S_prev = lax.scan(step, S0, (jnp.moveaxis(decay, 2, 0), jnp.moveaxis(U, 2, 0)))
    S_prev = jnp.moveaxis(S_prev, 0, 2)
    o_inter = jnp.einsum('bhnid,bhndv->bhniv', q_e, S_prev)
    return (o_intra + o_inter).reshape(B, H, T, dv)


def hybrid_mixer(h, w_in, attn_sink, w_gla_decay_fwd, b_gla_decay_fwd, w_gla_decay_bwd, b_gla_decay_bwd, gla_out_norm, w_out):
    B, T, _ = h.shape
    proj = h @ w_in
    aq, ak, av, gq, gk, gv, gg, r_f, r_b = jnp.split(proj, IN_SPLIT_OFFSETS, axis=-1)
    aq = rotary(aq.reshape(B, T, N_ATTN_HEADS, ATTN_HEAD_DIM))
    ak = rotary(ak.reshape(B, T, N_KV_HEADS, ATTN_HEAD_DIM))
    av = av.reshape(B, T, N_KV_HEADS, ATTN_HEAD_DIM)
    o_attn = windowed_gqa_attention(aq, ak, av, attn_sink)

    def to_heads(t, dh):
        return t.reshape(B, T, -1, dh).transpose(0, 2, 1, 3).astype(jnp.float32)

    q_g = to_heads(gq, GLA_KEY_DIM) * (GLA_KEY_DIM ** -0.5)
    k_g = to_heads(gk, GLA_KEY_DIM)
    v_g = to_heads(gv, GLA_VALUE_DIM)
    la_f = to_heads(jax.nn.log_sigmoid((r_f @ w_gla_decay_fwd + b_gla_decay_fwd).astype(jnp.float32)) / GLA_GATE_NORMALIZER, GLA_KEY_DIM)
    la_b = to_heads(jax.nn.log_sigmoid((r_b @ w_gla_decay_bwd + b_gla_decay_bwd).astype(jnp.float32)) / GLA_GATE_NORMALIZER, GLA_KEY_DIM)
    o_f = gla_chunked(q_g, k_g, v_g, la_f, False)
    o_b = jnp.flip(gla_chunked(jnp.flip(q_g, 2), jnp.flip(k_g, 2), jnp.flip(v_g, 2), jnp.flip(la_b, 2), True), 2)
    o_g = (o_f + o_b).transpose(0, 2, 1, 3)
    o_g = rmsnorm(o_g, gla_out_norm) * jax.nn.silu(gg.astype(jnp.float32).reshape(B, T, N_GLA_HEADS, GLA_VALUE_DIM))
    o_gla = o_g.reshape(B, T, GLA_V).astype(h.dtype)

    return jnp.concatenate([o_attn, o_gla], axis=-1) @ w_out


def encoder_trunk(x, norm_ffn1, w_ffn1_gate, w_ffn1_up, w_ffn1_down, norm_mix, w_in, attn_sink, w_gla_decay_fwd, b_gla_decay_fwd, w_gla_decay_bwd, b_gla_decay_bwd, gla_out_norm, w_out, norm_ffn2, w_ffn2_gate, w_ffn2_up, w_ffn2_down, norm_final):
    for l in range(DEPTH):
        x = x + 0.5 * swiglu(rmsnorm(x, norm_ffn1[l]), w_ffn1_gate[l], w_ffn1_up[l], w_ffn1_down[l])
        x = x + hybrid_mixer(rmsnorm(x, norm_mix[l]), w_in[l], attn_sink[l], w_gla_decay_fwd[l], b_gla_decay_fwd[l], w_gla_decay_bwd[l], b_gla_decay_bwd[l], gla_out_norm[l], w_out[l])
        x = x + 0.5 * swiglu(rmsnorm(x, norm_ffn2[l]), w_ffn2_gate[l], w_ffn2_up[l], w_ffn2_down[l])
    return rmsnorm(x, norm_final)


def setup_inputs(seed: int = 0) -> dict:
    key = jax.random.key(seed)
    ks = jax.random.split(key, 24)
    f32 = jnp.float32

    def w(k, shape, fan_in):
        return jax.random.normal(k, shape, f32) * (fan_in ** -0.5)

    def gain(k, shape):
        return 1.0 + 0.01 * jax.random.normal(k, shape, f32)

    return {
        'x_prompt': jax.random.normal(ks[0], (BATCH, SEQ, D_MODEL), f32),
        'x_sample': jax.random.normal(ks[1], (DEC_BATCH, DEC_SEQ, D_MODEL), f32),
        'norm_ffn1': gain(ks[2], (DEPTH, D_MODEL)),
        'w_ffn1_gate': w(ks[3], (DEPTH, D_MODEL, D_FF), D_MODEL),
        'w_ffn1_up': w(ks[4], (DEPTH, D_MODEL, D_FF), D_MODEL),
        'w_ffn1_down': w(ks[5], (DEPTH, D_FF, D_MODEL), D_FF),
        'norm_mix': gain(ks[6], (DEPTH, D_MODEL)),
        'w_in': w(ks[7], (DEPTH, D_MODEL, IN_PROJ_WIDTH), D_MODEL),
        'attn_sink': 0.5 * jax.random.normal(ks[8], (DEPTH, N_ATTN_HEADS), f32),
        'w_gla_decay_fwd': w(ks[9], (DEPTH, GLA_GATE_RANK, GLA_QK), GLA_GATE_RANK),
        'b_gla_decay_fwd': 0.1 * jax.random.normal(ks[10], (DEPTH, GLA_QK), f32),
        'w_gla_decay_bwd': w(ks[11], (DEPTH, GLA_GATE_RANK, GLA_QK), GLA_GATE_RANK),
        'b_gla_decay_bwd': 0.1 * jax.random.normal(ks[12], (DEPTH, GLA_QK), f32),
        'gla_out_norm': gain(ks[13], (DEPTH, GLA_VALUE_DIM)),
        'w_out': w(ks[14], (DEPTH, MIX_WIDTH, D_MODEL), MIX_WIDTH),
        'norm_ffn2': gain(ks[15], (DEPTH, D_MODEL)),
        'w_ffn2_gate': w(ks[16], (DEPTH, D_MODEL, D_FF), D_MODEL),
        'w_ffn2_up': w(ks[17], (DEPTH, D_MODEL, D_FF), D_MODEL),
        'w_ffn2_down': w(ks[18], (DEPTH, D_FF, D_MODEL), D_FF),
        'norm_final': gain(ks[19], (D_MODEL,)),
    }


def reference(x_prompt, x_sample, norm_ffn1, w_ffn1_gate, w_ffn1_up, w_ffn1_down, norm_mix, w_in, attn_sink, w_gla_decay_fwd, b_gla_decay_fwd, w_gla_decay_bwd, b_gla_decay_bwd, gla_out_norm, w_out, norm_ffn2, w_ffn2_gate, w_ffn2_up, w_ffn2_down, norm_final):
    y_prompt = encoder_trunk(x_prompt, norm_ffn1, w_ffn1_gate, w_ffn1_up, w_ffn1_down, norm_mix, w_in, attn_sink, w_gla_decay_fwd, b_gla_decay_fwd, w_gla_decay_bwd, b_gla_decay_bwd, gla_out_norm, w_out, norm_ffn2, w_ffn2_gate, w_ffn2_up, w_ffn2_down, norm_final)
    y_sample = encoder_trunk(x_sample, norm_ffn1, w_ffn1_gate, w_ffn1_up, w_ffn1_down, norm_mix, w_in, attn_sink, w_gla_decay_fwd, b_gla_decay_fwd, w_gla_decay_bwd, b_gla_decay_bwd, gla_out_norm, w_out, norm_ffn2, w_ffn2_gate, w_ffn2_up, w_ffn2_down, norm_final)
    return (y_prompt, y_sample)
```

```python
import functools

import jax
import jax.numpy as jnp
import numpy as np
from jax import lax
from jax.experimental import pallas as pl
from jax.experimental.pallas import tpu as pltpu

F32 = jnp.float32
BF16 = jnp.bfloat16

D_MODEL = 1024
D_FF = 2816
EPS = 1e-6
N_ATTN_HEADS = 8
N_KV_HEADS = 2
ATTN_GROUP = N_ATTN_HEADS // N_KV_HEADS
HEAD_DIM = 64
ATTN_BLOCK = 128
ROPE_THETA = 10000.0
N_GLA_HEADS = 4
GLA_DK = 64
GLA_DV = 128
GLA_RANK = 16
GLA_GATE_NORMALIZER = 16.0
GLA_CHUNK = 64
ATTN_Q = N_ATTN_HEADS * HEAD_DIM
ATTN_KV = N_KV_HEADS * HEAD_DIM
GLA_QK = N_GLA_HEADS * GLA_DK
GLA_V = N_GLA_HEADS * GLA_DV
IN_PROJ_WIDTH = ATTN_Q + 2 * ATTN_KV + 2 * GLA_QK + 2 * GLA_V + 2 * GLA_RANK
LANE = 128
IN_PROJ_PAD = ((IN_PROJ_WIDTH + LANE - 1) // LANE) * LANE
OFF_AQ = 0
OFF_AK = OFF_AQ + ATTN_Q
OFF_AV = OFF_AK + ATTN_KV
OFF_GQ = OFF_AV + ATTN_KV
OFF_GK = OFF_GQ + GLA_QK
OFF_GV = OFF_GK + GLA_QK
OFF_GG = OFF_GV + GLA_V
OFF_R = OFF_GG + GLA_V

ROW_TILE = 512
FF_CHUNK = 256
SWEEP_BLOCK = 256
VMEM_LIMIT = 56 * 1024 * 1024


def _rms(x, gain):
    return x * lax.rsqrt(jnp.mean(x * x, axis=-1, keepdims=True) + EPS) * gain


def _swiglu_residual(x, gain_ref, wg_ref, wu_ref, wd_ref, act_ref):
    h = _rms(x, gain_ref[...]).astype(BF16)
    for c in range(D_FF // FF_CHUNK):
        sl = slice(c * FF_CHUNK, (c + 1) * FF_CHUNK)
        g = jnp.dot(h, wg_ref[:, sl], preferred_element_type=F32)
        u = jnp.dot(h, wu_ref[:, sl], preferred_element_type=F32)
        act_ref[:, sl] = (g * (1.0 / (1.0 + jnp.exp(-g))) * u).astype(BF16)
    y = jnp.dot(act_ref[...], wd_ref[...], preferred_element_type=F32)
    return x + 0.5 * y


def _rope_pair(x, cos, sin_signed, first_half):
    swapped = jnp.where(first_half, pltpu.roll(x, LANE - HEAD_DIM // 2, 1), pltpu.roll(x, HEAD_DIM // 2, 1))
    return x * cos + swapped * sin_signed


def _ffn_inproj_kernel(x_ref, n1_ref, wg_ref, wu_ref, wd_ref, nm_ref, win_ref, cos_ref, sin_ref,
                       x1_ref, qa_ref, ka_ref, va_ref, gq_ref, gk_ref, gv_ref, gg_ref, r_ref, act_ref):
    x1 = _swiglu_residual(x_ref[...], n1_ref, wg_ref, wu_ref, wd_ref, act_ref)
    x1_ref[...] = x1
    h = _rms(x1, nm_ref[...]).astype(BF16)
    cos = cos_ref[...]
    sin = sin_ref[...]
    lane = lax.broadcasted_iota(jnp.int32, (1, LANE), 1)
    first_half = (lane % HEAD_DIM) < (HEAD_DIM // 2)

    def proj(off, width):
        return jnp.dot(h, win_ref[:, off:off + width], preferred_element_type=F32)

    scale = HEAD_DIM ** -0.5
    for j in range(ATTN_Q // LANE):
        q = _rope_pair(proj(OFF_AQ + j * LANE, LANE), cos, sin, first_half)
        qa_ref[:, j * LANE:(j + 1) * LANE] = (q * scale).astype(BF16)
    ka_ref[...] = _rope_pair(proj(OFF_AK, ATTN_KV), cos, sin, first_half).astype(BF16)
    va_ref[...] = proj(OFF_AV, ATTN_KV).astype(BF16)
    gq_ref[...] = proj(OFF_GQ, GLA_QK)
    gk_ref[...] = proj(OFF_GK, GLA_QK)
    gv_ref[...] = proj(OFF_GV, GLA_V).astype(BF16)
    gg_ref[...] = proj(OFF_GG, GLA_V)
    r_ref[...] = proj(OFF_R, LANE)[:, :2 * GLA_RANK].astype(BF16)


def _const_spec(shape):
    return pl.BlockSpec(shape, lambda *_: (0,) * len(shape), pipeline_mode=pl.Buffered(1))


def _ffn_inproj(x2d, seq_len, n1, wg, wu, wd, nm, win, cos_tab, sin_tab):
    n_rows = x2d.shape[0]
    tm = ROW_TILE
    assert n_rows % tm == 0 and seq_len % tm == 0
    tiles_per_seq = seq_len // tm
    row = lambda w: pl.BlockSpec((tm, w), lambda i: (i, 0))
    out_shapes = (
        jax.ShapeDtypeStruct((n_rows, D_MODEL), F32),
        jax.ShapeDtypeStruct((n_rows, ATTN_Q), BF16),
        jax.ShapeDtypeStruct((n_rows, ATTN_KV), BF16),
        jax.ShapeDtypeStruct((n_rows, ATTN_KV), BF16),
        jax.ShapeDtypeStruct((n_rows, GLA_QK), F32),
        jax.ShapeDtypeStruct((n_rows, GLA_QK), F32),
        jax.ShapeDtypeStruct((n_rows, GLA_V), BF16),
        jax.ShapeDtypeStruct((n_rows, GLA_V), F32),
        jax.ShapeDtypeStruct((n_rows, 2 * GLA_RANK), BF16),
    )
    return pl.pallas_call(
        _ffn_inproj_kernel,
        grid=(n_rows // tm,),
        in_specs=[
            row(D_MODEL),
            _const_spec((1, D_MODEL)),
            _const_spec((D_MODEL, D_FF)), _const_spec((D_MODEL, D_FF)), _const_spec((D_FF, D_MODEL)),
            _const_spec((1, D_MODEL)),
            _const_spec((D_MODEL, IN_PROJ_PAD)),
            pl.BlockSpec((tm, LANE), lambda i: (i % tiles_per_seq, 0)),
            pl.BlockSpec((tm, LANE), lambda i: (i % tiles_per_seq, 0)),
        ],
        out_specs=[row(D_MODEL), row(ATTN_Q), row(ATTN_KV), row(ATTN_KV), row(GLA_QK), row(GLA_QK),
                   row(GLA_V), row(GLA_V), row(2 * GLA_RANK)],
        out_shape=out_shapes,
        scratch_shapes=[pltpu.VMEM((tm, D_FF), BF16)],
        compiler_params=pltpu.CompilerParams(dimension_semantics=("parallel",), vmem_limit_bytes=VMEM_LIMIT),
        name="ffn1_inproj",
    )(x2d, n1, wg, wu, wd, nm, win, cos_tab, sin_tab)


def _gla_block(q, k, v, r, wdec_ref, bdec_ref, tri_ref, s_ref, reverse):
    bt = q.shape[0]
    nch = bt // GLA_CHUNK
    z = jnp.dot(r, wdec_ref[...], preferred_element_type=F32) + bdec_ref[...]
    log_a = (jnp.minimum(z, 0.0) - jnp.log1p(jnp.exp(-jnp.abs(z)))) * (1.0 / GLA_GATE_NORMALIZER)
    hi = log_a.astype(BF16)
    lo = (log_a - hi.astype(F32)).astype(BF16)
    tri = tri_ref[...]
    cum = jnp.dot(tri, hi, preferred_element_type=F32) + jnp.dot(tri, lo, preferred_element_type=F32)
    cum3 = cum.reshape(nch, GLA_CHUNK, GLA_QK)
    edge = GLA_CHUNK - 1 if not reverse else 0
    tot3 = cum3[:, edge:edge + 1, :]
    rest = (tot3 - cum3).reshape(bt, GLA_QK)
    qe = (q * (GLA_DK ** -0.5) * jnp.exp(cum)).astype(BF16)
    ke = (k * jnp.exp(-cum)).astype(BF16)
    ks_t = (k * jnp.exp(rest)).T.astype(BF16)
    tot = tot3.reshape(nch, GLA_QK)
    tot_t = jnp.concatenate([tot, jnp.zeros((LANE - nch, GLA_QK), F32)], axis=0).T
    decay_t = jnp.exp(tot_t)

    ii = lax.broadcasted_iota(jnp.int32, (GLA_CHUNK, GLA_CHUNK), 0)
    jj = lax.broadcasted_iota(jnp.int32, (GLA_CHUNK, GLA_CHUNK), 1)
    keep = (jj > ii) if reverse else (jj <= ii)

    out = [[None] * N_GLA_HEADS for _ in range(nch)]
    for n in (range(nch - 1, -1, -1) if reverse else range(nch)):
        rows = slice(n * GLA_CHUNK, (n + 1) * GLA_CHUNK)
        for h in range(N_GLA_HEADS):
            kl = slice(h * GLA_DK, (h + 1) * GLA_DK)
            vl = slice(h * GLA_DV, (h + 1) * GLA_DV)
            qe_nh = qe[rows, kl]
            v_nh = v[rows, vl]
            a = lax.dot_general(qe_nh, ke[rows, kl], (((1,), (1,)), ((), ())), preferred_element_type=F32)
            a = jnp.where(keep, a, 0.0).astype(BF16)
            s_h = s_ref[kl, :]
            o = jnp.dot(a, v_nh, preferred_element_type=F32)
            o = o + jnp.dot(qe_nh, s_h.astype(BF16), preferred_element_type=F32)
            u = jnp.dot(ks_t[kl, rows], v_nh, preferred_element_type=F32)
            s_ref[kl, :] = decay_t[kl, n:n + 1] * s_h + u
            out[n][h] = o
    return out


def _gla_bwd_kernel(gq_ref, gk_ref, gv_ref, r_ref, wdec_ref, bdec_ref, tri_ref, ob_ref, s_ref):
    @pl.when(pl.program_id(1) == 0)
    def _():
        s_ref[...] = jnp.zeros_like(s_ref)

    out = _gla_block(gq_ref[...], gk_ref[...], gv_ref[...], r_ref[...], wdec_ref, bdec_ref, tri_ref, s_ref, True)
    for n, per_head in enumerate(out):
        for h, o in enumerate(per_head):
            ob_ref[n * GLA_CHUNK:(n + 1) * GLA_CHUNK, h * GLA_DV:(h + 1) * GLA_DV] = o


def _gla_bwd(gq, gk, gv, r, wdec, bdec, tri_up):
    batch, seq_len, _ = gq.shape
    bt = SWEEP_BLOCK
    nt = seq_len // bt
    rev = lambda w: pl.BlockSpec((None, bt, w), lambda b, t: (b, nt - 1 - t, 0))
    return pl.pallas_call(
        _gla_bwd_kernel,
        grid=(batch, nt),
        in_specs=[rev(GLA_QK), rev(GLA_QK), rev(GLA_V), rev(2 * GLA_RANK),
                  _const_spec((2 * GLA_RANK, GLA_QK)), _const_spec((1, GLA_QK)), _const_spec((bt, bt))],
        out_specs=rev(GLA_V),
        out_shape=jax.ShapeDtypeStruct((batch, seq_len, GLA_V), F32),
        scratch_shapes=[pltpu.VMEM((GLA_QK, GLA_DV), F32)],
        compiler_params=pltpu.CompilerParams(dimension_semantics=("parallel", "arbitrary"),
                                             vmem_limit_bytes=VMEM_LIMIT),
        name="gla_bwd",
    )(gq, gk, gv, r, wdec, bdec, tri_up)


def _mixer_fwd_kernel(sink_ref, qa_ref, kp_ref, kc_ref, kn_ref, vp_ref, vc_ref, vn_ref,
                      gq_ref, gk_ref, gv_ref, gg_ref, r_ref, ob_ref,
                      wdec_ref, bdec_ref, tri_ref, gnorm_ref, mix_ref, s_ref):
    t = pl.program_id(1)
    bt = qa_ref.shape[0]
    sub = bt // ATTN_BLOCK
    n_qblocks = pl.num_programs(1) * sub

    @pl.when(t == 0)
    def _():
        s_ref[...] = jnp.zeros_like(s_ref)

    kbuf = jnp.concatenate([kp_ref[...], kc_ref[...], kn_ref[...]], axis=0)
    vbuf = jnp.concatenate([vp_ref[...], vc_ref[...], vn_ref[...]], axis=0)
    qi = lax.broadcasted_iota(jnp.int32, (ATTN_BLOCK, 3 * ATTN_BLOCK), 0)
    kj = lax.broadcasted_iota(jnp.int32, (ATTN_BLOCK, 3 * ATTN_BLOCK), 1)
    in_window = (kj >= qi) & (kj <= qi + 2 * ATTN_BLOCK)
    for jb in range(sub):
        qblk = t * sub + jb
        mask = in_window & ((kj >= ATTN_BLOCK) | (qblk > 0)) & ((kj < 2 * ATTN_BLOCK) | (qblk < n_qblocks - 1))
        qrows = slice(jb * ATTN_BLOCK, (jb + 1) * ATTN_BLOCK)
        krows = slice(jb * ATTN_BLOCK, (jb + 3) * ATTN_BLOCK)
        for g in range(N_KV_HEADS):
            kvl = slice(g * HEAD_DIM, (g + 1) * HEAD_DIM)
            heads = range(g * ATTN_GROUP, (g + 1) * ATTN_GROUP)
            qs = jnp.concatenate([qa_ref[qrows, h * HEAD_DIM:(h + 1) * HEAD_DIM] for h in heads], axis=0)
            s = lax.dot_general(qs, kbuf[krows, kvl], (((1,), (1,)), ((), ())), preferred_element_type=F32)
            probs, denoms = [], []
            for hl, h in enumerate(heads):
                s_h = jnp.where(mask, s[hl * ATTN_BLOCK:(hl + 1) * ATTN_BLOCK], -1e30)
                sink = sink_ref[h]
                m = jnp.maximum(jnp.max(s_h, axis=-1, keepdims=True), sink)
                p = jnp.exp(s_h - m)
                denoms.append(jnp.sum(p, axis=-1, keepdims=True) + jnp.exp(sink - m))
                probs.append(p.astype(BF16))
            o = jnp.dot(jnp.concatenate(probs, axis=0), vbuf[krows, kvl], preferred_element_type=F32)
            for hl, h in enumerate(heads):
                o_h = o[hl * ATTN_BLOCK:(hl + 1) * ATTN_BLOCK] / denoms[hl]
                mix_ref[qrows, h * HEAD_DIM:(h + 1) * HEAD_DIM] = o_h.astype(BF16)

    out = _gla_block(gq_ref[...], gk_ref[...], gv_ref[...], r_ref[...], wdec_ref, bdec_ref, tri_ref, s_ref, False)
    gain = gnorm_ref[...]
    for n, per_head in enumerate(out):
        rows = slice(n * GLA_CHUNK, (n + 1) * GLA_CHUNK)
        for h, o in enumerate(per_head):
            cols = slice(h * GLA_DV, (h + 1) * GLA_DV)
            o = o + ob_ref[rows, cols]
            gate = gg_ref[rows, cols]
            o = _rms(o, gain) * (gate * (1.0 / (1.0 + jnp.exp(-gate))))
            mix_ref[rows, ATTN_Q + h * GLA_DV:ATTN_Q + (h + 1) * GLA_DV] = o.astype(BF16)


def _mixer_fwd(sink, qa, ka, va, gq, gk, gv, gg, r, ob, wdec, bdec, tri_lo, gnorm):
    batch, seq_len, _ = qa.shape
    bt = SWEEP_BLOCK
    nt = seq_len // bt
    sub = bt // ATTN_BLOCK
    n_halo = seq_len // ATTN_BLOCK
    cur = lambda w: pl.BlockSpec((None, bt, w), lambda b, t: (b, t, 0))
    prev = pl.BlockSpec((None, ATTN_BLOCK, ATTN_KV), lambda b, t: (b, jnp.maximum(t * sub - 1, 0), 0))
    nxt = pl.BlockSpec((None, ATTN_BLOCK, ATTN_KV), lambda b, t: (b, jnp.minimum((t + 1) * sub, n_halo - 1), 0))
    return pl.pallas_call(
        _mixer_fwd_kernel,
        grid=(batch, nt),
        in_specs=[pl.BlockSpec(memory_space=pltpu.SMEM),
                  cur(ATTN_Q), prev, cur(ATTN_KV), nxt, prev, cur(ATTN_KV), nxt,
                  cur(GLA_QK), cur(GLA_QK), cur(GLA_V), cur(GLA_V), cur(2 * GLA_RANK), cur(GLA_V),
                  _const_spec((2 * GLA_RANK, GLA_QK)), _const_spec((1, GLA_QK)), _const_spec((bt, bt)),
                  _const_spec((1, GLA_DV))],
        out_specs=cur(D_MODEL),
        out_shape=jax.ShapeDtypeStruct((batch, seq_len, D_MODEL), BF16),
        scratch_shapes=[pltpu.VMEM((GLA_QK, GLA_DV), F32)],
        compiler_params=pltpu.CompilerParams(dimension_semantics=("parallel", "arbitrary"),
                                             vmem_limit_bytes=VMEM_LIMIT),
        name="mixer_fwd",
    )(sink, qa, ka, ka, ka, va, va, va, gq, gk, gv, gg, r, ob, wdec, bdec, tri_lo, gnorm)


def _outproj_ffn_kernel(x1_ref, mix_ref, wout_ref, n2_ref, wg_ref, wu_ref, wd_ref, nf_ref, y_ref, act_ref):
    x2 = x1_ref[...] + jnp.dot(mix_ref[...], wout_ref[...], preferred_element_type=F32)
    x3 = _swiglu_residual(x2, n2_ref, wg_ref, wu_ref, wd_ref, act_ref)
    y_ref[...] = _rms(x3, nf_ref[...])


def _outproj_ffn(x1, mix, wout, n2, wg, wu, wd, nf):
    n_rows = x1.shape[0]
    tm = ROW_TILE
    row = lambda w: pl.BlockSpec((tm, w), lambda i: (i, 0))
    return pl.pallas_call(
        _outproj_ffn_kernel,
        grid=(n_rows // tm,),
        in_specs=[row(D_MODEL), row(D_MODEL), _const_spec((D_MODEL, D_MODEL)), _const_spec((1, D_MODEL)),
                  _const_spec((D_MODEL, D_FF)), _const_spec((D_MODEL, D_FF)), _const_spec((D_FF, D_MODEL)),
                  _const_spec((1, D_MODEL))],
        out_specs=row(D_MODEL),
        out_shape=jax.ShapeDtypeStruct((n_rows, D_MODEL), F32),
        scratch_shapes=[pltpu.VMEM((tm, D_FF), BF16)],
        compiler_params=pltpu.CompilerParams(dimension_semantics=("parallel",), vmem_limit_bytes=VMEM_LIMIT),
        name="outproj_ffn2",
    )(x1, mix, wout, n2, wg, wu, wd, nf)


def _rope_tables(seq_len):
    half = HEAD_DIM // 2
    inv_freq = ROPE_THETA ** (-jnp.arange(half, dtype=F32) / half)
    ang = jnp.arange(seq_len, dtype=F32)[:, None] * inv_freq[None, :]
    cos, sin = jnp.cos(ang), jnp.sin(ang)
    cos_tab = jnp.tile(cos, (1, LANE // half))
    sin_tab = jnp.tile(jnp.concatenate([-sin, sin], axis=1), (1, LANE // HEAD_DIM))
    return cos_tab, sin_tab


def _chunk_tri(bt, upper):
    i = np.arange(bt)[:, None]
    j = np.arange(bt)[None, :]
    same_chunk = (i // GLA_CHUNK) == (j // GLA_CHUNK)
    keep = (j >= i) if upper else (j <= i)
    return jnp.asarray(same_chunk & keep, dtype=BF16)


def _trunk(x, p):
    batch, seq_len, _ = x.shape
    cos_tab, sin_tab = _rope_tables(seq_len)
    x2d = x.reshape(batch * seq_len, D_MODEL)
    x1, qa, ka, va, gq, gk, gv, gg, r = _ffn_inproj(
        x2d, seq_len, p["n1"], p["wg1"], p["wu1"], p["wd1"], p["nm"], p["win"], cos_tab, sin_tab)
    seq = lambda a: a.reshape(batch, seq_len, a.shape[-1])
    qa, ka, va, gq, gk, gv, gg, r = map(seq, (qa, ka, va, gq, gk, gv, gg, r))
    ob = _gla_bwd(gq, gk, gv, r, p["wdec_b"], p["bdec_b"], _chunk_tri(SWEEP_BLOCK, True))
    mix = _mixer_fwd(p["sink"], qa, ka, va, gq, gk, gv, gg, r, ob,
                     p["wdec_f"], p["bdec_f"], _chunk_tri(SWEEP_BLOCK, False), p["gnorm"])
    y = _outproj_ffn(x1, mix.reshape(batch * seq_len, D_MODEL), p["wout"], p["n2"],
                     p["wg2"], p["wu2"], p["wd2"], p["nf"])
    return y.reshape(batch, seq_len, D_MODEL)


def kernel(x_prompt, x_sample, norm_ffn1, w_ffn1_gate, w_ffn1_up, w_ffn1_down, norm_mix, w_in, attn_sink, w_gla_decay_fwd, b_gla_decay_fwd, w_gla_decay_bwd, b_gla_decay_bwd, gla_out_norm, w_out, norm_ffn2, w_ffn2_gate, w_ffn2_up, w_ffn2_down, norm_final):
    assert norm_ffn1.shape[0] == 1, "single-layer trunk"
    zeros_rank = jnp.zeros((GLA_RANK, GLA_QK), F32)
    p = dict(
        n1=norm_ffn1[0][None, :], wg1=w_ffn1_gate[0].astype(BF16), wu1=w_ffn1_up[0].astype(BF16),
        wd1=w_ffn1_down[0].astype(BF16),
        nm=norm_mix[0][None, :],
        win=jnp.pad(w_in[0], ((0, 0), (0, IN_PROJ_PAD - IN_PROJ_WIDTH))).astype(BF16),
        sink=attn_sink[0],
        wdec_f=jnp.concatenate([w_gla_decay_fwd[0], zeros_rank], axis=0).astype(BF16),
        bdec_f=b_gla_decay_fwd[0][None, :],
        wdec_b=jnp.concatenate([zeros_rank, w_gla_decay_bwd[0]], axis=0).astype(BF16),
        bdec_b=b_gla_decay_bwd[0][None, :],
        gnorm=gla_out_norm[0][None, :],
        wout=w_out[0].astype(BF16),
        n2=norm_ffn2[0][None, :], wg2=w_ffn2_gate[0].astype(BF16), wu2=w_ffn2_up[0].astype(BF16),
        wd2=w_ffn2_down[0].astype(BF16),
        nf=norm_final[None, :],
    )
    return _trunk(x_prompt, p), _trunk(x_sample, p)
```

```python
import functools

import jax
import jax.numpy as jnp
import numpy as np
from jax import lax
from jax.experimental import pallas as pl
from jax.experimental.pallas import tpu as pltpu

F32 = jnp.float32
BF16 = jnp.bfloat16

D_MODEL = 1024
D_FF = 2816
EPS = 1e-6
N_ATTN_HEADS = 8
N_KV_HEADS = 2
ATTN_GROUP = N_ATTN_HEADS // N_KV_HEADS
HEAD_DIM = 64
ATTN_BLOCK = 128
ROPE_THETA = 10000.0
N_GLA_HEADS = 4
GLA_DK = 64
GLA_DV = 128
GLA_RANK = 16
GLA_GATE_NORMALIZER = 16.0
GLA_CHUNK = 64
ATTN_Q = N_ATTN_HEADS * HEAD_DIM
ATTN_KV = N_KV_HEADS * HEAD_DIM
GLA_QK = N_GLA_HEADS * GLA_DK
GLA_V = N_GLA_HEADS * GLA_DV
IN_PROJ_WIDTH = ATTN_Q + 2 * ATTN_KV + 2 * GLA_QK + 2 * GLA_V + 2 * GLA_RANK
LANE = 128
MXU_COLS = 256
IN_PROJ_PAD = ((IN_PROJ_WIDTH + LANE - 1) // LANE) * LANE
OFF_AQ = 0
OFF_AK = OFF_AQ + ATTN_Q
OFF_AV = OFF_AK + ATTN_KV
OFF_GQ = OFF_AV + ATTN_KV
OFF_GK = OFF_GQ + GLA_QK
OFF_GV = OFF_GK + GLA_QK
OFF_GG = OFF_GV + GLA_V
OFF_R = OFF_GG + GLA_V

ROW_TILE = 512
FF_CHUNK = 256
GLA_BLOCK = 256
VMEM_LIMIT = 56 * 1024 * 1024


def _rms(x, gain):
    return x * lax.rsqrt(jnp.mean(x * x, axis=-1, keepdims=True) + EPS) * gain


def _silu(x):
    return x * (1.0 / (1.0 + jnp.exp(-x)))


def _advance(side, n):
    for _ in range(n):
        next(side, None)


def _swiglu_residual(x, gain_ref, wg_ref, wu_ref, wd_ref, act_ref, side, side_per_chunk):
    h = _rms(x, gain_ref[...]).astype(BF16)
    for c in range(D_FF // FF_CHUNK):
        sl = slice(c * FF_CHUNK, (c + 1) * FF_CHUNK)
        g = jnp.dot(h, wg_ref[:, sl], preferred_element_type=F32)
        u = jnp.dot(h, wu_ref[:, sl], preferred_element_type=F32)
        act_ref[:, sl] = (_silu(g) * u).astype(BF16)
        _advance(side, side_per_chunk)
    y = jnp.dot(act_ref[...], wd_ref[...], preferred_element_type=F32)
    return x + 0.5 * y


def _rope_pair(x, cos, sin_signed, first_half):
    swapped = jnp.where(first_half, pltpu.roll(x, LANE - HEAD_DIM // 2, 1), pltpu.roll(x, HEAD_DIM // 2, 1))
    return x * cos + swapped * sin_signed


GLA_CHUNKS_PER_BLOCK = GLA_BLOCK // GLA_CHUNK
GLA_STAGES_PER_TILE = (ROW_TILE // GLA_BLOCK) * (3 + 2 * GLA_CHUNKS_PER_BLOCK)


def _gla_block_stages(q, k, v, r, wdec_ref, bdec_ref, tri, s_ref, reverse, emit):
    bt = q.shape[0]
    nch = bt // GLA_CHUNK
    z = jnp.dot(r, wdec_ref[...], preferred_element_type=F32) + bdec_ref[...]
    log_a = (jnp.minimum(z, 0.0) - jnp.log1p(jnp.exp(-jnp.abs(z)))) * (1.0 / GLA_GATE_NORMALIZER)
    hi = log_a.astype(BF16)
    lo = (log_a - hi.astype(F32)).astype(BF16)
    cum = jnp.dot(tri, hi, preferred_element_type=F32) + jnp.dot(tri, lo, preferred_element_type=F32)
    yield
    cum3 = cum.reshape(nch, GLA_CHUNK, GLA_QK)
    edge = GLA_CHUNK - 1 if not reverse else 0
    tot3 = cum3[:, edge:edge + 1, :]
    rest = (tot3 - cum3).reshape(bt, GLA_QK)
    qe = (q * (GLA_DK ** -0.5) * jnp.exp(cum)).astype(BF16)
    ke = (k * jnp.exp(-cum)).astype(BF16)
    ks_t = (k * jnp.exp(rest)).T.astype(BF16)
    tot = tot3.reshape(nch, GLA_QK)
    tot_t = jnp.concatenate([tot, jnp.zeros((LANE - nch, GLA_QK), F32)], axis=0).T
    decay_t = jnp.exp(tot_t)

    ii = lax.broadcasted_iota(jnp.int32, (GLA_CHUNK, GLA_CHUNK), 0)
    jj = lax.broadcasted_iota(jnp.int32, (GLA_CHUNK, GLA_CHUNK), 1)
    keep = (jj > ii) if reverse else (jj <= ii)
    rows = [slice(n * GLA_CHUNK, (n + 1) * GLA_CHUNK) for n in range(nch)]
    klanes = [slice(h * GLA_DK, (h + 1) * GLA_DK) for h in range(N_GLA_HEADS)]
    vlanes = [slice(h * GLA_DV, (h + 1) * GLA_DV) for h in range(N_GLA_HEADS)]
    yield

    a = [[None] * N_GLA_HEADS for _ in range(nch)]
    u = [None] * nch
    for n in range(nch):
        for h in range(N_GLA_HEADS):
            s_nh = lax.dot_general(qe[rows[n], klanes[h]], ke[rows[n], klanes[h]], (((1,), (1,)), ((), ())),
                                   preferred_element_type=F32)
            a[n][h] = jnp.where(keep, s_nh, 0.0).astype(BF16)
        u[n] = jnp.concatenate([jnp.dot(ks_t[klanes[h], rows[n]], v[rows[n], vlanes[h]], preferred_element_type=F32)
                                for h in range(N_GLA_HEADS)], axis=0)
        yield
    s = s_ref[...]
    s_in = [None] * nch
    for n in (range(nch - 1, -1, -1) if reverse else range(nch)):
        s_in[n] = s.astype(BF16)
        s = decay_t[:, n:n + 1] * s + u[n]
    s_ref[...] = s
    yield
    for n in range(nch):
        for h in range(N_GLA_HEADS):
            emit(n, h, jnp.dot(a[n][h], v[rows[n], vlanes[h]], preferred_element_type=F32)
                 + jnp.dot(qe[rows[n], klanes[h]], s_in[n][klanes[h], :], preferred_element_type=F32))
        yield


def _gla_tile_stages(q_ref, k_ref, v_ref, r_ref, wdec_ref, bdec_ref, tri_ref, s_ref, reverse, emit):
    nblk = ROW_TILE // GLA_BLOCK
    for blk in (range(nblk - 1, -1, -1) if reverse else range(nblk)):
        brows = slice(blk * GLA_BLOCK, (blk + 1) * GLA_BLOCK)
        emit_block = lambda n, h, o, base=blk * GLA_BLOCK: emit(base + n * GLA_CHUNK, h, o)
        yield from _gla_block_stages(q_ref[brows, :], k_ref[brows, :], v_ref[brows, :], r_ref[brows, :],
                                     wdec_ref, bdec_ref, tri_ref[...], s_ref, reverse, emit_block)


def _reverse_sweep_kernel(x_ref, n1_ref, wg_ref, wu_ref, wd_ref, nm_ref, win_ref, cos_ref, sin_ref,
                          wdec_ref, bdec_ref, tri_ref,
                          x1_ref, qa_ref, ka_ref, va_ref, gq_ref, gk_ref, gv_ref, gg_ref, r_ref, ob_ref,
                          act_ref, pq_ref, pk_ref, pv_ref, pr_ref, s_ref, *, tiles_per_seq):
    g = pl.program_id(0)
    cur_slot = g % 2
    prev_slot = 1 - cur_slot

    @pl.when(g == 0)
    def _():
        pq_ref[...] = jnp.zeros_like(pq_ref)
        pk_ref[...] = jnp.zeros_like(pk_ref)
        pv_ref[...] = jnp.zeros_like(pv_ref)
        pr_ref[...] = jnp.zeros_like(pr_ref)

    @pl.when(jnp.maximum(g - 1, 0) % tiles_per_seq == 0)
    def _():
        s_ref[...] = jnp.zeros_like(s_ref)

    def emit_ob(row0, h, o):
        ob_ref[row0:row0 + GLA_CHUNK, h * GLA_DV:(h + 1) * GLA_DV] = o

    side = _gla_tile_stages(pq_ref.at[prev_slot], pk_ref.at[prev_slot], pv_ref.at[prev_slot], pr_ref.at[prev_slot],
                            wdec_ref, bdec_ref, tri_ref, s_ref, True, emit_ob)

    x1 = _swiglu_residual(x_ref[...], n1_ref, wg_ref, wu_ref, wd_ref, act_ref, side, 1)
    x1_ref[...] = x1
    h = _rms(x1, nm_ref[...]).astype(BF16)
    cos = cos_ref[...]
    sin = sin_ref[...]
    lane = lax.broadcasted_iota(jnp.int32, (1, LANE), 1)
    first_half = (lane % HEAD_DIM) < (HEAD_DIM // 2)

    def proj(off, width):
        _advance(side, 1)
        return jnp.dot(h, win_ref[:, off:off + width], preferred_element_type=F32)

    scale = HEAD_DIM ** -0.5
    for j in range(ATTN_Q // MXU_COLS):
        q2 = proj(OFF_AQ + j * MXU_COLS, MXU_COLS)
        for i in range(MXU_COLS // LANE):
            q = _rope_pair(q2[:, i * LANE:(i + 1) * LANE], cos, sin, first_half)
            qa_ref[:, j * MXU_COLS + i * LANE:j * MXU_COLS + (i + 1) * LANE] = (q * scale).astype(BF16)
    kv = proj(OFF_AK, 2 * ATTN_KV)
    ka_ref[...] = _rope_pair(kv[:, :ATTN_KV], cos, sin, first_half).astype(BF16)
    va_ref[...] = kv[:, ATTN_KV:].astype(BF16)
    gq = proj(OFF_GQ, GLA_QK)
    gq_ref[...] = gq
    pq_ref[cur_slot] = gq
    gk = proj(OFF_GK, GLA_QK)
    gk_ref[...] = gk
    pk_ref[cur_slot] = gk
    for j in range(GLA_V // MXU_COLS):
        cols = slice(j * MXU_COLS, (j + 1) * MXU_COLS)
        gv = proj(OFF_GV + j * MXU_COLS, MXU_COLS).astype(BF16)
        gv_ref[:, cols] = gv
        pv_ref[cur_slot, :, cols] = gv
        gg_ref[:, cols] = proj(OFF_GG + j * MXU_COLS, MXU_COLS)
    rr = proj(OFF_R, LANE)[:, :2 * GLA_RANK].astype(BF16)
    r_ref[...] = rr
    pr_ref[cur_slot] = rr
    for _ in side:
        pass


def _const_spec(shape):
    return pl.BlockSpec(shape, lambda *_: (0,) * len(shape), pipeline_mode=pl.Buffered(1))


def _reverse_sweep(x2d, seq_len, n1, wg, wu, wd, nm, win, cos_tab, sin_tab, wdec, bdec, tri_up):
    n_rows = x2d.shape[0]
    tm = ROW_TILE
    assert n_rows % tm == 0 and seq_len % tm == 0
    nt = seq_len // tm
    n_tiles = n_rows // tm

    def tile_of(step):
        return (step // nt) * nt + (nt - 1 - step % nt)

    cur = lambda g: tile_of(jnp.minimum(g, n_tiles - 1))
    lag = lambda g: tile_of(jnp.maximum(g - 1, 0))
    row = lambda w: pl.BlockSpec((tm, w), lambda g: (cur(g), 0))
    rope = pl.BlockSpec((tm, LANE), lambda g: (nt - 1 - jnp.minimum(g, n_tiles - 1) % nt, 0))
    out_shapes = (
        jax.ShapeDtypeStruct((n_rows, D_MODEL), F32),
        jax.ShapeDtypeStruct((n_rows, ATTN_Q), BF16),
        jax.ShapeDtypeStruct((n_rows, ATTN_KV), BF16),
        jax.ShapeDtypeStruct((n_rows, ATTN_KV), BF16),
        jax.ShapeDtypeStruct((n_rows, GLA_QK), F32),
        jax.ShapeDtypeStruct((n_rows, GLA_QK), F32),
        jax.ShapeDtypeStruct((n_rows, GLA_V), BF16),
        jax.ShapeDtypeStruct((n_rows, GLA_V), F32),
        jax.ShapeDtypeStruct((n_rows, 2 * GLA_RANK), BF16),
        jax.ShapeDtypeStruct((n_rows, GLA_V), F32),
    )
    return pl.pallas_call(
        functools.partial(_reverse_sweep_kernel, tiles_per_seq=nt),
        grid=(n_tiles + 1,),
        in_specs=[
            row(D_MODEL),
            _const_spec((1, D_MODEL)),
            _const_spec((D_MODEL, D_FF)), _const_spec((D_MODEL, D_FF)), _const_spec((D_FF, D_MODEL)),
            _const_spec((1, D_MODEL)),
            _const_spec((D_MODEL, IN_PROJ_PAD)),
            rope, rope,
            _const_spec((2 * GLA_RANK, GLA_QK)), _const_spec((1, GLA_QK)), _const_spec((GLA_BLOCK, GLA_BLOCK)),
        ],
        out_specs=[row(D_MODEL), row(ATTN_Q), row(ATTN_KV), row(ATTN_KV), row(GLA_QK), row(GLA_QK),
                   row(GLA_V), row(GLA_V), row(2 * GLA_RANK),
                   pl.BlockSpec((tm, GLA_V), lambda g: (lag(g), 0))],
        out_shape=out_shapes,
        scratch_shapes=[pltpu.VMEM((tm, D_FF), BF16),
                        pltpu.VMEM((2, tm, GLA_QK), F32), pltpu.VMEM((2, tm, GLA_QK), F32),
                        pltpu.VMEM((2, tm, GLA_V), BF16), pltpu.VMEM((2, tm, 2 * GLA_RANK), BF16),
                        pltpu.VMEM((GLA_QK, GLA_DV), F32)],
        compiler_params=pltpu.CompilerParams(dimension_semantics=("arbitrary",), vmem_limit_bytes=VMEM_LIMIT),
        name="reverse_sweep",
    )(x2d, n1, wg, wu, wd, nm, win, cos_tab, sin_tab, wdec, bdec, tri_up)


ATTN_STAGES_PER_TILE = (ROW_TILE // ATTN_BLOCK) * N_KV_HEADS


def _round_robin(*stage_generators):
    live = list(stage_generators)
    while live:
        for gen in list(live):
            try:
                next(gen)
                yield
            except StopIteration:
                live.remove(gen)


def _attention_tile_stages(sink_ref, qa_ref, kp_ref, kc_ref, kn_ref, vp_ref, vc_ref, vn_ref, mix_ref, tpos,
                           tiles_per_seq):
    sub = ROW_TILE // ATTN_BLOCK
    n_qblocks = tiles_per_seq * sub
    kbuf = jnp.concatenate([kp_ref[...], kc_ref[...], kn_ref[...]], axis=0)
    vbuf = jnp.concatenate([vp_ref[...], vc_ref[...], vn_ref[...]], axis=0)
    qi = lax.broadcasted_iota(jnp.int32, (ATTN_BLOCK, 3 * ATTN_BLOCK), 0)
    kj = lax.broadcasted_iota(jnp.int32, (ATTN_BLOCK, 3 * ATTN_BLOCK), 1)
    in_window = (kj >= qi) & (kj <= qi + 2 * ATTN_BLOCK)
    for jb in range(sub):
        qblk = tpos * sub + jb
        mask = in_window & ((kj >= ATTN_BLOCK) | (qblk > 0)) & ((kj < 2 * ATTN_BLOCK) | (qblk < n_qblocks - 1))
        qrows = slice(jb * ATTN_BLOCK, (jb + 1) * ATTN_BLOCK)
        krows = slice(jb * ATTN_BLOCK, (jb + 3) * ATTN_BLOCK)
        for kv in range(N_KV_HEADS):
            kvl = slice(kv * HEAD_DIM, (kv + 1) * HEAD_DIM)
            heads = range(kv * ATTN_GROUP, (kv + 1) * ATTN_GROUP)
            qs = jnp.concatenate([qa_ref[qrows, h * HEAD_DIM:(h + 1) * HEAD_DIM] for h in heads], axis=0)
            s = lax.dot_general(qs, kbuf[krows, kvl], (((1,), (1,)), ((), ())), preferred_element_type=F32)
            probs, denoms = [], []
            for hl, h in enumerate(heads):
                s_h = jnp.where(mask, s[hl * ATTN_BLOCK:(hl + 1) * ATTN_BLOCK], -1e30)
                sink = sink_ref[h]
                m = jnp.maximum(jnp.max(s_h, axis=-1, keepdims=True), sink)
                p = jnp.exp(s_h - m)
                denoms.append(jnp.sum(p, axis=-1, keepdims=True) + jnp.exp(sink - m))
                probs.append(p.astype(BF16))
            o = jnp.dot(jnp.concatenate(probs, axis=0), vbuf[krows, kvl], preferred_element_type=F32)
            for hl, h in enumerate(heads):
                o_h = o[hl * ATTN_BLOCK:(hl + 1) * ATTN_BLOCK] / denoms[hl]
                mix_ref[qrows, h * HEAD_DIM:(h + 1) * HEAD_DIM] = o_h.astype(BF16)
            yield


def _forward_sweep_kernel(sink_ref, qa_ref, kp_ref, kc_ref, kn_ref, vp_ref, vc_ref, vn_ref,
                          gq_ref, gk_ref, gv_ref, gg_ref, r_ref, ob_ref, wdec_ref, bdec_ref, tri_ref, gnorm_ref,
                          x1_ref, wout_ref, n2_ref, wg_ref, wu_ref, wd_ref, nf_ref,
                          y_ref, act_ref, mix_ref, s_ref, *, tiles_per_seq):
    g = pl.program_id(0)
    mix_cur = mix_ref.at[g % 2]
    mix_prev = mix_ref.at[1 - g % 2]

    @pl.when(g == 0)
    def _():
        mix_ref[...] = jnp.zeros_like(mix_ref)

    @pl.when(g % tiles_per_seq == 0)
    def _():
        s_ref[...] = jnp.zeros_like(s_ref)

    tpos = jnp.minimum(g, pl.num_programs(0) - 2) % tiles_per_seq
    gain = gnorm_ref[...]

    def emit_mix(row0, h, o):
        rows = slice(row0, row0 + GLA_CHUNK)
        cols = slice(h * GLA_DV, (h + 1) * GLA_DV)
        o = _rms(o + ob_ref[rows, cols], gain) * _silu(gg_ref[rows, cols])
        mix_cur[rows, ATTN_Q + h * GLA_DV:ATTN_Q + (h + 1) * GLA_DV] = o.astype(BF16)

    side = _round_robin(
        _attention_tile_stages(sink_ref, qa_ref, kp_ref, kc_ref, kn_ref, vp_ref, vc_ref, vn_ref, mix_cur, tpos,
                               tiles_per_seq),
        _gla_tile_stages(gq_ref, gk_ref, gv_ref, r_ref, wdec_ref, bdec_ref, tri_ref, s_ref, False, emit_mix))
    n_chunks = D_FF // FF_CHUNK
    side_per_chunk = -(-(ATTN_STAGES_PER_TILE + GLA_STAGES_PER_TILE) // n_chunks)

    x2 = x1_ref[...] + jnp.dot(mix_prev[...], wout_ref[...], preferred_element_type=F32)
    x3 = _swiglu_residual(x2, n2_ref, wg_ref, wu_ref, wd_ref, act_ref, side, side_per_chunk)
    y_ref[...] = _rms(x3, nf_ref[...])
    for _ in side:
        pass


def _forward_sweep(seq_len, sink, qa, ka, va, gq, gk, gv, gg, r, ob, wdec, bdec, tri_lo, gnorm,
                   x1, wout, n2, wg, wu, wd, nf):
    n_rows = x1.shape[0]
    tm = ROW_TILE
    nt = seq_len // tm
    n_tiles = n_rows // tm
    sub = tm // ATTN_BLOCK
    n_halo = n_rows // ATTN_BLOCK
    halo_per_seq = seq_len // ATTN_BLOCK
    cur = lambda g: jnp.minimum(g, n_tiles - 1)
    lag = lambda g: jnp.maximum(g - 1, 0)
    row = lambda w: pl.BlockSpec((tm, w), lambda g: (cur(g), 0))
    lag_row = lambda w: pl.BlockSpec((tm, w), lambda g: (lag(g), 0))

    def prev_idx(g):
        t = cur(g)
        return jnp.maximum(t * sub - 1, (t // nt) * halo_per_seq)

    def next_idx(g):
        t = cur(g)
        return jnp.minimum((t + 1) * sub, (t // nt + 1) * halo_per_seq - 1)

    prev = pl.BlockSpec((ATTN_BLOCK, ATTN_KV), lambda g: (prev_idx(g), 0))
    nxt = pl.BlockSpec((ATTN_BLOCK, ATTN_KV), lambda g: (next_idx(g), 0))
    return pl.pallas_call(
        functools.partial(_forward_sweep_kernel, tiles_per_seq=nt),
        grid=(n_tiles + 1,),
        in_specs=[pl.BlockSpec(memory_space=pltpu.SMEM),
                  row(ATTN_Q), prev, row(ATTN_KV), nxt, prev, row(ATTN_KV), nxt,
                  row(GLA_QK), row(GLA_QK), row(GLA_V), row(GLA_V), row(2 * GLA_RANK), row(GLA_V),
                  _const_spec((2 * GLA_RANK, GLA_QK)), _const_spec((1, GLA_QK)), _const_spec((GLA_BLOCK, GLA_BLOCK)),
                  _const_spec((1, GLA_DV)),
                  lag_row(D_MODEL), _const_spec((D_MODEL, D_MODEL)), _const_spec((1, D_MODEL)),
                  _const_spec((D_MODEL, D_FF)), _const_spec((D_MODEL, D_FF)), _const_spec((D_FF, D_MODEL)),
                  _const_spec((1, D_MODEL))],
        out_specs=lag_row(D_MODEL),
        out_shape=jax.ShapeDtypeStruct((n_rows, D_MODEL), F32),
        scratch_shapes=[pltpu.VMEM((tm, D_FF), BF16), pltpu.VMEM((2, tm, D_MODEL), BF16),
                        pltpu.VMEM((GLA_QK, GLA_DV), F32)],
        compiler_params=pltpu.CompilerParams(dimension_semantics=("arbitrary",), vmem_limit_bytes=VMEM_LIMIT),
        name="forward_sweep",
    )(sink, qa, ka, ka, ka, va, va, va, gq, gk, gv, gg, r, ob, wdec, bdec, tri_lo, gnorm,
      x1, wout, n2, wg, wu, wd, nf)


def _rope_tables(seq_len):
    half = HEAD_DIM // 2
    inv_freq = ROPE_THETA ** (-jnp.arange(half, dtype=F32) / half)
    ang = jnp.arange(seq_len, dtype=F32)[:, None] * inv_freq[None, :]
    cos, sin = jnp.cos(ang), jnp.sin(ang)
    cos_tab = jnp.tile(cos, (1, LANE // half))
    sin_tab = jnp.tile(jnp.concatenate([-sin, sin], axis=1), (1, LANE // HEAD_DIM))
    return cos_tab, sin_tab


def _chunk_tri(bt, upper):
    i = np.arange(bt)[:, None]
    j = np.arange(bt)[None, :]
    same_chunk = (i // GLA_CHUNK) == (j // GLA_CHUNK)
    keep = (j >= i) if upper else (j <= i)
    return jnp.asarray(same_chunk & keep, dtype=BF16)


def _trunk(x, p):
    batch, seq_len, _ = x.shape
    cos_tab, sin_tab = _rope_tables(seq_len)
    x2d = x.reshape(batch * seq_len, D_MODEL)
    x1, qa, ka, va, gq, gk, gv, gg, r, ob = _reverse_sweep(
        x2d, seq_len, p["n1"], p["wg1"], p["wu1"], p["wd1"], p["nm"], p["win"], cos_tab, sin_tab,
        p["wdec_b"], p["bdec_b"], _chunk_tri(GLA_BLOCK, True))
    y = _forward_sweep(seq_len, p["sink"], qa, ka, va, gq, gk, gv, gg, r, ob,
                       p["wdec_f"], p["bdec_f"], _chunk_tri(GLA_BLOCK, False), p["gnorm"],
                       x1, p["wout"], p["n2"], p["wg2"], p["wu2"], p["wd2"], p["nf"])
    return y.reshape(batch, seq_len, D_MODEL)


def kernel(x_prompt, x_sample, norm_ffn1, w_ffn1_gate, w_ffn1_up, w_ffn1_down, norm_mix, w_in, attn_sink, w_gla_decay_fwd, b_gla_decay_fwd, w_gla_decay_bwd, b_gla_decay_bwd, gla_out_norm, w_out, norm_ffn2, w_ffn2_gate, w_ffn2_up, w_ffn2_down, norm_final):
    assert norm_ffn1.shape[0] == 1, "single-layer trunk"
    zeros_rank = jnp.zeros((GLA_RANK, GLA_QK), F32)
    p = dict(
        n1=norm_ffn1[0][None, :], wg1=w_ffn1_gate[0].astype(BF16), wu1=w_ffn1_up[0].astype(BF16),
        wd1=w_ffn1_down[0].astype(BF16),
        nm=norm_mix[0][None, :],
        win=jnp.pad(w_in[0], ((0, 0), (0, IN_PROJ_PAD - IN_PROJ_WIDTH))).astype(BF16),
        sink=attn_sink[0],
        wdec_f=jnp.concatenate([w_gla_decay_fwd[0], zeros_rank], axis=0).astype(BF16),
        bdec_f=b_gla_decay_fwd[0][None, :],
        wdec_b=jnp.concatenate([zeros_rank, w_gla_decay_bwd[0]], axis=0).astype(BF16),
        bdec_b=b_gla_decay_bwd[0][None, :],
        gnorm=gla_out_norm[0][None, :],
        wout=w_out[0].astype(BF16),
        n2=norm_ffn2[0][None, :], wg2=w_ffn2_gate[0].astype(BF16), wu2=w_ffn2_up[0].astype(BF16),
        wd2=w_ffn2_down[0].astype(BF16),
        nf=norm_final[None, :],
    )
    return _trunk(x_prompt, p), _trunk(x_sample, p)
```

```python
import functools

import jax
import jax.numpy as jnp
import numpy as np
from jax import lax
from jax.experimental import pallas as pl
from jax.experimental.pallas import tpu as pltpu

F32 = jnp.float32
BF16 = jnp.bfloat16

D_MODEL = 1024
D_FF = 2816
EPS = 1e-6
N_ATTN_HEADS = 8
N_KV_HEADS = 2
ATTN_GROUP = N_ATTN_HEADS // N_KV_HEADS
HEAD_DIM = 64
ATTN_BLOCK = 128
ROPE_THETA = 10000.0
N_GLA_HEADS = 4
GLA_DK = 64
GLA_DV = 128
GLA_RANK = 16
GLA_GATE_NORMALIZER = 16.0
GLA_CHUNK = 64
ATTN_Q = N_ATTN_HEADS * HEAD_DIM
ATTN_KV = N_KV_HEADS * HEAD_DIM
GLA_QK = N_GLA_HEADS * GLA_DK
GLA_V = N_GLA_HEADS * GLA_DV
IN_PROJ_WIDTH = ATTN_Q + 2 * ATTN_KV + 2 * GLA_QK + 2 * GLA_V + 2 * GLA_RANK
LANE = 128
MXU_COLS = 256
IN_PROJ_PAD = ((IN_PROJ_WIDTH + LANE - 1) // LANE) * LANE
OFF_AQ = 0
OFF_AK = OFF_AQ + ATTN_Q
OFF_AV = OFF_AK + ATTN_KV
OFF_GQ = OFF_AV + ATTN_KV
OFF_GK = OFF_GQ + GLA_QK
OFF_GV = OFF_GK + GLA_QK
OFF_GG = OFF_GV + GLA_V
OFF_R = OFF_GG + GLA_V

ROW_TILE = 512
FF_CHUNK = 256
GLA_BLOCK = 256
VMEM_LIMIT = 56 * 1024 * 1024


def _rms(x, gain):
    return x * lax.rsqrt(jnp.mean(x * x, axis=-1, keepdims=True) + EPS) * gain


def _silu(x):
    return x * (1.0 / (1.0 + jnp.exp(-x)))


def _advance(side, n):
    for _ in range(n):
        next(side, None)


FFN_TICKS = 1 + D_FF // FF_CHUNK + D_MODEL // MXU_COLS


def _swiglu_residual(x, gain_ref, wg_ref, wu_ref, wd_ref, act_ref, side, side_per_tick):
    _advance(side, side_per_tick)
    h = _rms(x, gain_ref[...]).astype(BF16)
    for c in range(D_FF // FF_CHUNK):
        sl = slice(c * FF_CHUNK, (c + 1) * FF_CHUNK)
        g = jnp.dot(h, wg_ref[:, sl], preferred_element_type=F32)
        u = jnp.dot(h, wu_ref[:, sl], preferred_element_type=F32)
        act_ref[:, sl] = (_silu(g) * u).astype(BF16)
        _advance(side, side_per_tick)
    out = []
    for j in range(D_MODEL // MXU_COLS):
        cols = slice(j * MXU_COLS, (j + 1) * MXU_COLS)
        y = jnp.dot(act_ref[...], wd_ref[:, cols], preferred_element_type=F32)
        out.append(x[:, cols] + 0.5 * y)
        _advance(side, side_per_tick)
    return jnp.concatenate(out, axis=1)


def _rope_pair(x, cos, sin_signed, first_half):
    swapped = jnp.where(first_half, pltpu.roll(x, LANE - HEAD_DIM // 2, 1), pltpu.roll(x, HEAD_DIM // 2, 1))
    return x * cos + swapped * sin_signed


GLA_CHUNKS_PER_BLOCK = GLA_BLOCK // GLA_CHUNK
GLA_BLOCKS_PER_TILE = ROW_TILE // GLA_BLOCK
GLA_STAGES_PER_TILE = GLA_BLOCKS_PER_TILE * (4 + 2 * GLA_CHUNKS_PER_BLOCK)


def _round_robin(*stage_generators):
    live = list(stage_generators)
    while live:
        for gen in list(live):
            try:
                next(gen)
                yield
            except StopIteration:
                live.remove(gen)


def _gla_block_stages(q_ref, k_ref, v_ref, r_ref, brows, wdec_ref, bdec_ref, tri_ref, s_ref, reverse, emit):
    bt = brows.stop - brows.start
    nch = bt // GLA_CHUNK
    z = jnp.dot(r_ref[brows, :], wdec_ref[...], preferred_element_type=F32) + bdec_ref[...]
    yield
    log_a = (jnp.minimum(z, 0.0) - jnp.log1p(jnp.exp(-jnp.abs(z)))) * (1.0 / GLA_GATE_NORMALIZER)
    hi = log_a.astype(BF16)
    lo = (log_a - hi.astype(F32)).astype(BF16)
    tri = tri_ref[...]
    cum = jnp.dot(tri, hi, preferred_element_type=F32) + jnp.dot(tri, lo, preferred_element_type=F32)
    yield
    q = q_ref[brows, :]
    k = k_ref[brows, :]
    cum3 = cum.reshape(nch, GLA_CHUNK, GLA_QK)
    edge = GLA_CHUNK - 1 if not reverse else 0
    tot3 = cum3[:, edge:edge + 1, :]
    rest = (tot3 - cum3).reshape(bt, GLA_QK)
    qe = (q * (GLA_DK ** -0.5) * jnp.exp(cum)).astype(BF16)
    ke = (k * jnp.exp(-cum)).astype(BF16)
    ks_t = (k * jnp.exp(rest)).T.astype(BF16)
    tot = tot3.reshape(nch, GLA_QK)
    tot_t = jnp.concatenate([tot, jnp.zeros((LANE - nch, GLA_QK), F32)], axis=0).T
    decay_t = jnp.exp(tot_t)

    ii = lax.broadcasted_iota(jnp.int32, (GLA_CHUNK, GLA_CHUNK), 0)
    jj = lax.broadcasted_iota(jnp.int32, (GLA_CHUNK, GLA_CHUNK), 1)
    keep = (jj > ii) if reverse else (jj <= ii)
    rows = [slice(n * GLA_CHUNK, (n + 1) * GLA_CHUNK) for n in range(nch)]
    klanes = [slice(h * GLA_DK, (h + 1) * GLA_DK) for h in range(N_GLA_HEADS)]

    def v_of(n, h):
        return v_ref[brows.start + n * GLA_CHUNK:brows.start + (n + 1) * GLA_CHUNK, h * GLA_DV:(h + 1) * GLA_DV]

    yield

    a = [[None] * N_GLA_HEADS for _ in range(nch)]
    u = [None] * nch
    for n in range(nch):
        for h in range(N_GLA_HEADS):
            s_nh = lax.dot_general(qe[rows[n], klanes[h]], ke[rows[n], klanes[h]], (((1,), (1,)), ((), ())),
                                   preferred_element_type=F32)
            a[n][h] = jnp.where(keep, s_nh, 0.0).astype(BF16)
        u[n] = jnp.concatenate([jnp.dot(ks_t[klanes[h], rows[n]], v_of(n, h), preferred_element_type=F32)
                                for h in range(N_GLA_HEADS)], axis=0)
        yield
    s = s_ref[...]
    s_in = [None] * nch
    for n in (range(nch - 1, -1, -1) if reverse else range(nch)):
        s_in[n] = s.astype(BF16)
        s = decay_t[:, n:n + 1] * s + u[n]
    s_ref[...] = s
    yield
    for n in range(nch):
        for h in range(N_GLA_HEADS):
            emit(n, h, jnp.dot(a[n][h], v_of(n, h), preferred_element_type=F32)
                 + jnp.dot(qe[rows[n], klanes[h]], s_in[n][klanes[h], :], preferred_element_type=F32))
        yield


def _gla_tile_streams(q_ref, k_ref, v_ref, r_ref, wdec_ref, bdec_ref, tri_ref, s_ref, reverse, emit):
    streams = []
    for blk in (range(GLA_BLOCKS_PER_TILE - 1, -1, -1) if reverse else range(GLA_BLOCKS_PER_TILE)):
        brows = slice(blk * GLA_BLOCK, (blk + 1) * GLA_BLOCK)
        emit_block = lambda n, h, o, base=blk * GLA_BLOCK: emit(base + n * GLA_CHUNK, h, o)
        streams.append(_gla_block_stages(q_ref, k_ref, v_ref, r_ref, brows, wdec_ref, bdec_ref, tri_ref, s_ref,
                                         reverse, emit_block))
    return streams


def _reverse_sweep_kernel(x_ref, n1_ref, wg_ref, wu_ref, wd_ref, nm_ref, win_ref, cos_ref, sin_ref,
                          wdec_ref, bdec_ref, tri_ref,
                          x1_ref, qa_ref, ka_ref, va_ref, gq_ref, gk_ref, gv_ref, gg_ref, r_ref, ob_ref,
                          act_ref, pq_ref, pk_ref, pv_ref, pr_ref, s_ref, *, tiles_per_seq):
    g = pl.program_id(0)
    cur_slot = g % 2
    prev_slot = 1 - cur_slot

    @pl.when(g == 0)
    def _():
        pq_ref[...] = jnp.zeros_like(pq_ref)
        pk_ref[...] = jnp.zeros_like(pk_ref)
        pv_ref[...] = jnp.zeros_like(pv_ref)
        pr_ref[...] = jnp.zeros_like(pr_ref)

    @pl.when(jnp.maximum(g - 1, 0) % tiles_per_seq == 0)
    def _():
        s_ref[...] = jnp.zeros_like(s_ref)

    def emit_ob(row0, h, o):
        ob_ref[row0:row0 + GLA_CHUNK, h * GLA_DV:(h + 1) * GLA_DV] = o

    side = _round_robin(*_gla_tile_streams(pq_ref.at[prev_slot], pk_ref.at[prev_slot], pv_ref.at[prev_slot],
                                           pr_ref.at[prev_slot], wdec_ref, bdec_ref, tri_ref, s_ref, True, emit_ob))

    x1 = _swiglu_residual(x_ref[...], n1_ref, wg_ref, wu_ref, wd_ref, act_ref, side, 1)
    x1_ref[...] = x1
    h = _rms(x1, nm_ref[...]).astype(BF16)
    cos = cos_ref[...]
    sin = sin_ref[...]
    lane = lax.broadcasted_iota(jnp.int32, (1, LANE), 1)
    first_half = (lane % HEAD_DIM) < (HEAD_DIM // 2)

    def proj(off, width):
        _advance(side, 1)
        return jnp.dot(h, win_ref[:, off:off + width], preferred_element_type=F32)

    scale = HEAD_DIM ** -0.5
    for j in range(ATTN_Q // MXU_COLS):
        q2 = proj(OFF_AQ + j * MXU_COLS, MXU_COLS)
        for i in range(MXU_COLS // LANE):
            q = _rope_pair(q2[:, i * LANE:(i + 1) * LANE], cos, sin, first_half)
            qa_ref[:, j * MXU_COLS + i * LANE:j * MXU_COLS + (i + 1) * LANE] = (q * scale).astype(BF16)
    kv = proj(OFF_AK, 2 * ATTN_KV)
    ka_ref[...] = _rope_pair(kv[:, :ATTN_KV], cos, sin, first_half).astype(BF16)
    va_ref[...] = kv[:, ATTN_KV:].astype(BF16)
    gq = proj(OFF_GQ, GLA_QK)
    gq_ref[...] = gq
    pq_ref[cur_slot] = gq
    gk = proj(OFF_GK, GLA_QK)
    gk_ref[...] = gk
    pk_ref[cur_slot] = gk
    for j in range(GLA_V // MXU_COLS):
        cols = slice(j * MXU_COLS, (j + 1) * MXU_COLS)
        gv = proj(OFF_GV + j * MXU_COLS, MXU_COLS).astype(BF16)
        gv_ref[:, cols] = gv
        pv_ref[cur_slot, :, cols] = gv
        gg_ref[:, cols] = proj(OFF_GG + j * MXU_COLS, MXU_COLS)
    rr = proj(OFF_R, LANE)[:, :2 * GLA_RANK].astype(BF16)
    r_ref[...] = rr
    pr_ref[cur_slot] = rr
    for _ in side:
        pass


def _const_spec(shape):
    return pl.BlockSpec(shape, lambda *_: (0,) * len(shape), pipeline_mode=pl.Buffered(1))


def _reverse_sweep(x2d, seq_len, n1, wg, wu, wd, nm, win, cos_tab, sin_tab, wdec, bdec, tri_up):
    n_rows = x2d.shape[0]
    tm = ROW_TILE
    assert n_rows % tm == 0 and seq_len % tm == 0
    nt = seq_len // tm
    n_tiles = n_rows // tm

    def tile_of(step):
        return (step // nt) * nt + (nt - 1 - step % nt)

    cur = lambda g: tile_of(jnp.minimum(g, n_tiles - 1))
    lag = lambda g: tile_of(jnp.maximum(g - 1, 0))
    row = lambda w: pl.BlockSpec((tm, w), lambda g: (cur(g), 0))
    rope = pl.BlockSpec((tm, LANE), lambda g: (nt - 1 - jnp.minimum(g, n_tiles - 1) % nt, 0))
    out_shapes = (
        jax.ShapeDtypeStruct((n_rows, D_MODEL), F32),
        jax.ShapeDtypeStruct((n_rows, ATTN_Q), BF16),
        jax.ShapeDtypeStruct((n_rows, ATTN_KV), BF16),
        jax.ShapeDtypeStruct((n_rows, ATTN_KV), BF16),
        jax.ShapeDtypeStruct((n_rows, GLA_QK), F32),
        jax.ShapeDtypeStruct((n_rows, GLA_QK), F32),
        jax.ShapeDtypeStruct((n_rows, GLA_V), BF16),
        jax.ShapeDtypeStruct((n_rows, GLA_V), F32),
        jax.ShapeDtypeStruct((n_rows, 2 * GLA_RANK), BF16),
        jax.ShapeDtypeStruct((n_rows, GLA_V), F32),
    )
    return pl.pallas_call(
        functools.partial(_reverse_sweep_kernel, tiles_per_seq=nt),
        grid=(n_tiles + 1,),
        in_specs=[
            row(D_MODEL),
            _const_spec((1, D_MODEL)),
            _const_spec((D_MODEL, D_FF)), _const_spec((D_MODEL, D_FF)), _const_spec((D_FF, D_MODEL)),
            _const_spec((1, D_MODEL)),
            _const_spec((D_MODEL, IN_PROJ_PAD)),
            rope, rope,
            _const_spec((2 * GLA_RANK, GLA_QK)), _const_spec((1, GLA_QK)), _const_spec((GLA_BLOCK, GLA_BLOCK)),
        ],
        out_specs=[row(D_MODEL), row(ATTN_Q), row(ATTN_KV), row(ATTN_KV), row(GLA_QK), row(GLA_QK),
                   row(GLA_V), row(GLA_V), row(2 * GLA_RANK),
                   pl.BlockSpec((tm, GLA_V), lambda g: (lag(g), 0))],
        out_shape=out_shapes,
        scratch_shapes=[pltpu.VMEM((tm, D_FF), BF16),
                        pltpu.VMEM((2, tm, GLA_QK), F32), pltpu.VMEM((2, tm, GLA_QK), F32),
                        pltpu.VMEM((2, tm, GLA_V), BF16), pltpu.VMEM((2, tm, 2 * GLA_RANK), BF16),
                        pltpu.VMEM((GLA_QK, GLA_DV), F32)],
        compiler_params=pltpu.CompilerParams(dimension_semantics=("arbitrary",), vmem_limit_bytes=VMEM_LIMIT),
        name="reverse_sweep",
    )(x2d, n1, wg, wu, wd, nm, win, cos_tab, sin_tab, wdec, bdec, tri_up)


ATTN_UNITS_PER_TILE = (ROW_TILE // ATTN_BLOCK) * N_KV_HEADS
ATTN_STAGES_PER_TILE = 3 * ATTN_UNITS_PER_TILE


def _attention_tile_stages(sink_ref, qa_ref, kp_ref, kc_ref, kn_ref, vp_ref, vc_ref, vn_ref, mix_ref, tpos,
                           tiles_per_seq):
    sub = ROW_TILE // ATTN_BLOCK
    n_qblocks = tiles_per_seq * sub
    kbuf = jnp.concatenate([kp_ref[...], kc_ref[...], kn_ref[...]], axis=0)
    vbuf = jnp.concatenate([vp_ref[...], vc_ref[...], vn_ref[...]], axis=0)
    qi = lax.broadcasted_iota(jnp.int32, (ATTN_BLOCK, 3 * ATTN_BLOCK), 0)
    kj = lax.broadcasted_iota(jnp.int32, (ATTN_BLOCK, 3 * ATTN_BLOCK), 1)
    in_window = (kj >= qi) & (kj <= qi + 2 * ATTN_BLOCK)

    def unit(jb, kv):
        qblk = tpos * sub + jb
        qrows = slice(jb * ATTN_BLOCK, (jb + 1) * ATTN_BLOCK)
        krows = slice(jb * ATTN_BLOCK, (jb + 3) * ATTN_BLOCK)
        kvl = slice(kv * HEAD_DIM, (kv + 1) * HEAD_DIM)
        heads = range(kv * ATTN_GROUP, (kv + 1) * ATTN_GROUP)
        qs = jnp.concatenate([qa_ref[qrows, h * HEAD_DIM:(h + 1) * HEAD_DIM] for h in heads], axis=0)
        s = lax.dot_general(qs, kbuf[krows, kvl], (((1,), (1,)), ((), ())), preferred_element_type=F32)
        yield
        mask = in_window & ((kj >= ATTN_BLOCK) | (qblk > 0)) & ((kj < 2 * ATTN_BLOCK) | (qblk < n_qblocks - 1))
        probs, denoms = [], []
        for hl, h in enumerate(heads):
            s_h = jnp.where(mask, s[hl * ATTN_BLOCK:(hl + 1) * ATTN_BLOCK], -1e30)
            sink = sink_ref[h]
            m = jnp.maximum(jnp.max(s_h, axis=-1, keepdims=True), sink)
            p = jnp.exp(s_h - m)
            denoms.append(jnp.sum(p, axis=-1, keepdims=True) + jnp.exp(sink - m))
            probs.append(p.astype(BF16))
        p_all = jnp.concatenate(probs, axis=0)
        yield
        o = jnp.dot(p_all, vbuf[krows, kvl], preferred_element_type=F32)
        for hl, h in enumerate(heads):
            o_h = o[hl * ATTN_BLOCK:(hl + 1) * ATTN_BLOCK] / denoms[hl]
            mix_ref[qrows, h * HEAD_DIM:(h + 1) * HEAD_DIM] = o_h.astype(BF16)
        yield

    units = [unit(jb, kv) for jb in range(sub) for kv in range(N_KV_HEADS)]
    for slot in range(len(units) + 4):
        for stage in range(3):
            u = slot - 2 * stage
            if 0 <= u < len(units):
                next(units[u])
                yield


def _forward_sweep_kernel(sink_ref, qa_ref, kp_ref, kc_ref, kn_ref, vp_ref, vc_ref, vn_ref,
                          gq_ref, gk_ref, gv_ref, gg_ref, r_ref, ob_ref, wdec_ref, bdec_ref, tri_ref, gnorm_ref,
                          x1_ref, wout_ref, n2_ref, wg_ref, wu_ref, wd_ref, nf_ref,
                          y_ref, act_ref, mix_ref, s_ref, *, tiles_per_seq):
    g = pl.program_id(0)
    mix_cur = mix_ref.at[g % 2]
    mix_prev = mix_ref.at[1 - g % 2]

    @pl.when(g == 0)
    def _():
        mix_ref[...] = jnp.zeros_like(mix_ref)

    @pl.when(g % tiles_per_seq == 0)
    def _():
        s_ref[...] = jnp.zeros_like(s_ref)

    tpos = jnp.minimum(g, pl.num_programs(0) - 2) % tiles_per_seq
    gain = gnorm_ref[...]

    def emit_mix(row0, h, o):
        rows = slice(row0, row0 + GLA_CHUNK)
        cols = slice(h * GLA_DV, (h + 1) * GLA_DV)
        o = _rms(o + ob_ref[rows, cols], gain) * _silu(gg_ref[rows, cols])
        mix_cur[rows, ATTN_Q + h * GLA_DV:ATTN_Q + (h + 1) * GLA_DV] = o.astype(BF16)

    side = _round_robin(
        _attention_tile_stages(sink_ref, qa_ref, kp_ref, kc_ref, kn_ref, vp_ref, vc_ref, vn_ref, mix_cur, tpos,
                               tiles_per_seq),
        *_gla_tile_streams(gq_ref, gk_ref, gv_ref, r_ref, wdec_ref, bdec_ref, tri_ref, s_ref, False, emit_mix))
    side_per_tick = -(-(ATTN_STAGES_PER_TILE + GLA_STAGES_PER_TILE) // FFN_TICKS)

    x2 = x1_ref[...] + jnp.dot(mix_prev[...], wout_ref[...], preferred_element_type=F32)
    x3 = _swiglu_residual(x2, n2_ref, wg_ref, wu_ref, wd_ref, act_ref, side, side_per_tick)
    y_ref[...] = _rms(x3, nf_ref[...])
    for _ in side:
        pass


def _forward_sweep(seq_len, sink, qa, ka, va, gq, gk, gv, gg, r, ob, wdec, bdec, tri_lo, gnorm,
                   x1, wout, n2, wg, wu, wd, nf):
    n_rows = x1.shape[0]
    tm = ROW_TILE
    nt = seq_len // tm
    n_tiles = n_rows // tm
    sub = tm // ATTN_BLOCK
    n_halo = n_rows // ATTN_BLOCK
    halo_per_seq = seq_len // ATTN_BLOCK
    cur = lambda g: jnp.minimum(g, n_tiles - 1)
    lag = lambda g: jnp.maximum(g - 1, 0)
    row = lambda w: pl.BlockSpec((tm, w), lambda g: (cur(g), 0))
    lag_row = lambda w: pl.BlockSpec((tm, w), lambda g: (lag(g), 0))

    def prev_idx(g):
        t = cur(g)
        return jnp.maximum(t * sub - 1, (t // nt) * halo_per_seq)

    def next_idx(g):
        t = cur(g)
        return jnp.minimum((t + 1) * sub, (t // nt + 1) * halo_per_seq - 1)

    prev = pl.BlockSpec((ATTN_BLOCK, ATTN_KV), lambda g: (prev_idx(g), 0))
    nxt = pl.BlockSpec((ATTN_BLOCK, ATTN_KV), lambda g: (next_idx(g), 0))
    return pl.pallas_call(
        functools.partial(_forward_sweep_kernel, tiles_per_seq=nt),
        grid=(n_tiles + 1,),
        in_specs=[pl.BlockSpec(memory_space=pltpu.SMEM),
                  row(ATTN_Q), prev, row(ATTN_KV), nxt, prev, row(ATTN_KV), nxt,
                  row(GLA_QK), row(GLA_QK), row(GLA_V), row(GLA_V), row(2 * GLA_RANK), row(GLA_V),
                  _const_spec((2 * GLA_RANK, GLA_QK)), _const_spec((1, GLA_QK)), _const_spec((GLA_BLOCK, GLA_BLOCK)),
                  _const_spec((1, GLA_DV)),
                  lag_row(D_MODEL), _const_spec((D_MODEL, D_MODEL)), _const_spec((1, D_MODEL)),
                  _const_spec((D_MODEL, D_FF)), _const_spec((D_MODEL, D_FF)), _const_spec((D_FF, D_MODEL)),
                  _const_spec((1, D_MODEL))],
        out_specs=lag_row(D_MODEL),
        out_shape=jax.ShapeDtypeStruct((n_rows, D_MODEL), F32),
        scratch_shapes=[pltpu.VMEM((tm, D_FF), BF16), pltpu.VMEM((2, tm, D_MODEL), BF16),
                        pltpu.VMEM((GLA_QK, GLA_DV), F32)],
        compiler_params=pltpu.CompilerParams(dimension_semantics=("arbitrary",), vmem_limit_bytes=VMEM_LIMIT),
        name="forward_sweep",
    )(sink, qa, ka, ka, ka, va, va, va, gq, gk, gv, gg, r, ob, wdec, bdec, tri_lo, gnorm,
      x1, wout, n2, wg, wu, wd, nf)


def _rope_tables(seq_len):
    half = HEAD_DIM // 2
    inv_freq = ROPE_THETA ** (-jnp.arange(half, dtype=F32) / half)
    ang = jnp.arange(seq_len, dtype=F32)[:, None] * inv_freq[None, :]
    cos, sin = jnp.cos(ang), jnp.sin(ang)
    cos_tab = jnp.tile(cos, (1, LANE // half))
    sin_tab = jnp.tile(jnp.concatenate([-sin, sin], axis=1), (1, LANE // HEAD_DIM))
    return cos_tab, sin_tab


def _chunk_tri(bt, upper):
    i = np.arange(bt)[:, None]
    j = np.arange(bt)[None, :]
    same_chunk = (i // GLA_CHUNK) == (j // GLA_CHUNK)
    keep = (j >= i) if upper else (j <= i)
    return jnp.asarray(same_chunk & keep, dtype=BF16)


def _trunk(x, p):
    batch, seq_len, _ = x.shape
    cos_tab, sin_tab = _rope_tables(seq_len)
    x2d = x.reshape(batch * seq_len, D_MODEL)
    x1, qa, ka, va, gq, gk, gv, gg, r, ob = _reverse_sweep(
        x2d, seq_len, p["n1"], p["wg1"], p["wu1"], p["wd1"], p["nm"], p["win"], cos_tab, sin_tab,
        p["wdec_b"], p["bdec_b"], _chunk_tri(GLA_BLOCK, True))
    y = _forward_sweep(seq_len, p["sink"], qa, ka, va, gq, gk, gv, gg, r, ob,
                       p["wdec_f"], p["bdec_f"], _chunk_tri(GLA_BLOCK, False), p["gnorm"],
                       x1, p["wout"], p["n2"], p["wg2"], p["wu2"], p["wd2"], p["nf"])
    return y.reshape(batch, seq_len, D_MODEL)


def kernel(x_prompt, x_sample, norm_ffn1, w_ffn1_gate, w_ffn1_up, w_ffn1_down, norm_mix, w_in, attn_sink, w_gla_decay_fwd, b_gla_decay_fwd, w_gla_decay_bwd, b_gla_decay_bwd, gla_out_norm, w_out, norm_ffn2, w_ffn2_gate, w_ffn2_up, w_ffn2_down, norm_final):
    assert norm_ffn1.shape[0] == 1, "single-layer trunk"
    zeros_rank = jnp.zeros((GLA_RANK, GLA_QK), F32)
    p = dict(
        n1=norm_ffn1[0][None, :], wg1=w_ffn1_gate[0].astype(BF16), wu1=w_ffn1_up[0].astype(BF16),
        wd1=w_ffn1_down[0].astype(BF16),
        nm=norm_mix[0][None, :],
        win=jnp.pad(w_in[0], ((0, 0), (0, IN_PROJ_PAD - IN_PROJ_WIDTH))).astype(BF16),
        sink=attn_sink[0],
        wdec_f=jnp.concatenate([w_gla_decay_fwd[0], zeros_rank], axis=0).astype(BF16),
        bdec_f=b_gla_decay_fwd[0][None, :],
        wdec_b=jnp.concatenate([zeros_rank, w_gla_decay_bwd[0]], axis=0).astype(BF16),
        bdec_b=b_gla_decay_bwd[0][None, :],
        gnorm=gla_out_norm[0][None, :],
        wout=w_out[0].astype(BF16),
        n2=norm_ffn2[0][None, :], wg2=w_ffn2_gate[0].astype(BF16), wu2=w_ffn2_up[0].astype(BF16),
        wd2=w_ffn2_down[0].astype(BF16),
        nf=norm_final[None, :],
    )
    return _trunk(x_prompt, p), _trunk(x_sample, p)
```

```python
import functools

import jax
import jax.numpy as jnp
import numpy as np
from jax import lax
from jax.experimental import pallas as pl
from jax.experimental.pallas import tpu as pltpu

F32 = jnp.float32
BF16 = jnp.bfloat16

D_MODEL = 1024
D_FF = 2816
EPS = 1e-6
N_ATTN_HEADS = 8
N_KV_HEADS = 2
ATTN_GROUP = N_ATTN_HEADS // N_KV_HEADS
HEAD_DIM = 64
ATTN_BLOCK = 128
ROPE_THETA = 10000.0
N_GLA_HEADS = 4
GLA_DK = 64
GLA_DV = 128
GLA_RANK = 16
GLA_GATE_NORMALIZER = 16.0
GLA_CHUNK = 64
ATTN_Q = N_ATTN_HEADS * HEAD_DIM
ATTN_KV = N_KV_HEADS * HEAD_DIM
GLA_QK = N_GLA_HEADS * GLA_DK
GLA_V = N_GLA_HEADS * GLA_DV
IN_PROJ_WIDTH = ATTN_Q + 2 * ATTN_KV + 2 * GLA_QK + 2 * GLA_V + 2 * GLA_RANK
LANE = 128
MXU_COLS = 256
IN_PROJ_PAD = ((IN_PROJ_WIDTH + LANE - 1) // LANE) * LANE
OFF_AQ = 0
OFF_AK = OFF_AQ + ATTN_Q
OFF_AV = OFF_AK + ATTN_KV
OFF_GQ = OFF_AV + ATTN_KV
OFF_GK = OFF_GQ + GLA_QK
OFF_GV = OFF_GK + GLA_QK
OFF_GG = OFF_GV + GLA_V
OFF_R = OFF_GG + GLA_V

ROW_TILE = 512
FF_CHUNK = 256
GLA_BLOCK = 256
VMEM_LIMIT = 56 * 1024 * 1024


def _rms(x, gain):
    return x * lax.rsqrt(jnp.mean(x * x, axis=-1, keepdims=True) + EPS) * gain


def _silu(x):
    return x * (1.0 / (1.0 + jnp.exp(-x)))


def _advance(side, n):
    for _ in range(n):
        next(side, None)


FFN_TICKS = 1 + D_FF // FF_CHUNK + D_MODEL // MXU_COLS


def _swiglu_residual(x, gain_ref, wg_ref, wu_ref, wd_ref, act_ref, side, side_per_tick):
    _advance(side, side_per_tick)
    h = _rms(x, gain_ref[...]).astype(BF16)
    for c in range(D_FF // FF_CHUNK):
        sl = slice(c * FF_CHUNK, (c + 1) * FF_CHUNK)
        g = jnp.dot(h, wg_ref[:, sl], preferred_element_type=F32)
        u = jnp.dot(h, wu_ref[:, sl], preferred_element_type=F32)
        act_ref[:, sl] = (_silu(g) * u).astype(BF16)
        _advance(side, side_per_tick)
    out = []
    for j in range(D_MODEL // MXU_COLS):
        cols = slice(j * MXU_COLS, (j + 1) * MXU_COLS)
        y = jnp.dot(act_ref[...], wd_ref[:, cols], preferred_element_type=F32)
        out.append(x[:, cols] + 0.5 * y)
        _advance(side, side_per_tick)
    return jnp.concatenate(out, axis=1)


def _rope_pair(x, cos, sin_signed, first_half):
    swapped = jnp.where(first_half, pltpu.roll(x, LANE - HEAD_DIM // 2, 1), pltpu.roll(x, HEAD_DIM // 2, 1))
    return x * cos + swapped * sin_signed


GLA_CHUNKS_PER_BLOCK = GLA_BLOCK // GLA_CHUNK
GLA_BLOCKS_PER_TILE = ROW_TILE // GLA_BLOCK
GLA_STAGES_PER_TILE = GLA_BLOCKS_PER_TILE * (4 + 2 * GLA_CHUNKS_PER_BLOCK)


def _round_robin(*stage_generators):
    live = list(stage_generators)
    while live:
        for gen in list(live):
            try:
                next(gen)
                yield
            except StopIteration:
                live.remove(gen)


def _gla_block_stages(q_ref, k_ref, v_ref, r_ref, brows, wdec_ref, bdec_ref, tri_ref, s_ref, reverse, emit):
    bt = brows.stop - brows.start
    nch = bt // GLA_CHUNK
    z = jnp.dot(r_ref[brows, :], wdec_ref[...], preferred_element_type=F32) + bdec_ref[...]
    yield
    log_a = (jnp.minimum(z, 0.0) - jnp.log1p(jnp.exp(-jnp.abs(z)))) * (1.0 / GLA_GATE_NORMALIZER)
    hi = log_a.astype(BF16)
    lo = (log_a - hi.astype(F32)).astype(BF16)
    tri = tri_ref[...]
    cum = jnp.dot(tri, hi, preferred_element_type=F32) + jnp.dot(tri, lo, preferred_element_type=F32)
    yield
    q = q_ref[brows, :]
    k = k_ref[brows, :]
    cum3 = cum.reshape(nch, GLA_CHUNK, GLA_QK)
    edge = GLA_CHUNK - 1 if not reverse else 0
    tot3 = cum3[:, edge:edge + 1, :]
    rest = (tot3 - cum3).reshape(bt, GLA_QK)
    qe = (q * (GLA_DK ** -0.5) * jnp.exp(cum)).astype(BF16)
    ke = (k * jnp.exp(-cum)).astype(BF16)
    ks_t = (k * jnp.exp(rest)).T.astype(BF16)
    tot = tot3.reshape(nch, GLA_QK)
    tot_t = jnp.concatenate([tot, jnp.zeros((LANE - nch, GLA_QK), F32)], axis=0).T
    decay_t = jnp.exp(tot_t)

    ii = lax.broadcasted_iota(jnp.int32, (GLA_CHUNK, GLA_CHUNK), 0)
    jj = lax.broadcasted_iota(jnp.int32, (GLA_CHUNK, GLA_CHUNK), 1)
    keep = (jj > ii) if reverse else (jj <= ii)
    rows = [slice(n * GLA_CHUNK, (n + 1) * GLA_CHUNK) for n in range(nch)]
    klanes = [slice(h * GLA_DK, (h + 1) * GLA_DK) for h in range(N_GLA_HEADS)]

    def v_of(n, h):
        return v_ref[brows.start + n * GLA_CHUNK:brows.start + (n + 1) * GLA_CHUNK, h * GLA_DV:(h + 1) * GLA_DV]

    yield

    a = [[None] * N_GLA_HEADS for _ in range(nch)]
    u = [None] * nch
    for n in range(nch):
        for h in range(N_GLA_HEADS):
            s_nh = lax.dot_general(qe[rows[n], klanes[h]], ke[rows[n], klanes[h]], (((1,), (1,)), ((), ())),
                                   preferred_element_type=F32)
            a[n][h] = jnp.where(keep, s_nh, 0.0).astype(BF16)
        u[n] = jnp.concatenate([jnp.dot(ks_t[klanes[h], rows[n]], v_of(n, h), preferred_element_type=F32)
                                for h in range(N_GLA_HEADS)], axis=0)
        yield
    s = s_ref[...]
    s_in = [None] * nch
    for n in (range(nch - 1, -1, -1) if reverse else range(nch)):
        s_in[n] = s.astype(BF16)
        s = decay_t[:, n:n + 1] * s + u[n]
    s_ref[...] = s
    yield
    for n in range(nch):
        for h in range(N_GLA_HEADS):
            emit(n, h, jnp.dot(a[n][h], v_of(n, h), preferred_element_type=F32)
                 + jnp.dot(qe[rows[n], klanes[h]], s_in[n][klanes[h], :], preferred_element_type=F32))
        yield


def _gla_tile_streams(q_ref, k_ref, v_ref, r_ref, wdec_ref, bdec_ref, tri_ref, s_ref, reverse, emit):
    streams = []
    for blk in (range(GLA_BLOCKS_PER_TILE - 1, -1, -1) if reverse else range(GLA_BLOCKS_PER_TILE)):
        brows = slice(blk * GLA_BLOCK, (blk + 1) * GLA_BLOCK)
        emit_block = lambda n, h, o, base=blk * GLA_BLOCK: emit(base + n * GLA_CHUNK, h, o)
        streams.append(_gla_block_stages(q_ref, k_ref, v_ref, r_ref, brows, wdec_ref, bdec_ref, tri_ref, s_ref,
                                         reverse, emit_block))
    return streams


def _reverse_sweep_kernel(x_ref, n1_ref, wg_ref, wu_ref, wd_ref, nm_ref, win_ref, cos_ref, sin_ref,
                          wdec_ref, bdec_ref, tri_ref,
                          x1_ref, qa_ref, ka_ref, va_ref, gq_ref, gk_ref, gv_ref, gg_ref, r_ref, ob_ref,
                          act_ref, pq_ref, pk_ref, pv_ref, pr_ref, s_ref, *, tiles_per_seq):
    g = pl.program_id(0)
    cur_slot = g % 2
    prev_slot = 1 - cur_slot

    @pl.when(g == 0)
    def _():
        pq_ref[...] = jnp.zeros_like(pq_ref)
        pk_ref[...] = jnp.zeros_like(pk_ref)
        pv_ref[...] = jnp.zeros_like(pv_ref)
        pr_ref[...] = jnp.zeros_like(pr_ref)

    @pl.when(jnp.maximum(g - 1, 0) % tiles_per_seq == 0)
    def _():
        s_ref[...] = jnp.zeros_like(s_ref)

    def emit_ob(row0, h, o):
        ob_ref[row0:row0 + GLA_CHUNK, h * GLA_DV:(h + 1) * GLA_DV] = o

    side = _round_robin(*_gla_tile_streams(pq_ref.at[prev_slot], pk_ref.at[prev_slot], pv_ref.at[prev_slot],
                                           pr_ref.at[prev_slot], wdec_ref, bdec_ref, tri_ref, s_ref, True, emit_ob))

    x1 = _swiglu_residual(x_ref[...], n1_ref, wg_ref, wu_ref, wd_ref, act_ref, side, 1)
    x1_ref[...] = x1
    h = _rms(x1, nm_ref[...]).astype(BF16)
    cos = cos_ref[...]
    sin = sin_ref[...]
    lane = lax.broadcasted_iota(jnp.int32, (1, LANE), 1)
    first_half = (lane % HEAD_DIM) < (HEAD_DIM // 2)

    def proj(off, width):
        _advance(side, 1)
        return jnp.dot(h, win_ref[:, off:off + width], preferred_element_type=F32)

    scale = HEAD_DIM ** -0.5
    for j in range(ATTN_Q // MXU_COLS):
        q2 = proj(OFF_AQ + j * MXU_COLS, MXU_COLS)
        for i in range(MXU_COLS // LANE):
            q = _rope_pair(q2[:, i * LANE:(i + 1) * LANE], cos, sin, first_half)
            qa_ref[:, j * MXU_COLS + i * LANE:j * MXU_COLS + (i + 1) * LANE] = (q * scale).astype(BF16)
    kv = proj(OFF_AK, 2 * ATTN_KV)
    ka_ref[...] = _rope_pair(kv[:, :ATTN_KV], cos, sin, first_half).astype(BF16)
    va_ref[...] = kv[:, ATTN_KV:].T.astype(BF16)
    gq = proj(OFF_GQ, GLA_QK)
    gq_ref[...] = gq
    pq_ref[cur_slot] = gq
    gk = proj(OFF_GK, GLA_QK)
    gk_ref[...] = gk
    pk_ref[cur_slot] = gk
    for j in range(GLA_V // MXU_COLS):
        cols = slice(j * MXU_COLS, (j + 1) * MXU_COLS)
        gv = proj(OFF_GV + j * MXU_COLS, MXU_COLS).astype(BF16)
        gv_ref[:, cols] = gv
        pv_ref[cur_slot, :, cols] = gv
        gg_ref[:, cols] = proj(OFF_GG + j * MXU_COLS, MXU_COLS)
    rr = proj(OFF_R, LANE)[:, :2 * GLA_RANK].astype(BF16)
    r_ref[...] = rr
    pr_ref[cur_slot] = rr
    for _ in side:
        pass


def _const_spec(shape):
    return pl.BlockSpec(shape, lambda *_: (0,) * len(shape), pipeline_mode=pl.Buffered(1))


def _reverse_sweep(x2d, seq_len, n1, wg, wu, wd, nm, win, cos_tab, sin_tab, wdec, bdec, tri_up):
    n_rows = x2d.shape[0]
    tm = ROW_TILE
    assert n_rows % tm == 0 and seq_len % tm == 0
    nt = seq_len // tm
    n_tiles = n_rows // tm

    def tile_of(step):
        return (step // nt) * nt + (nt - 1 - step % nt)

    cur = lambda g: tile_of(jnp.minimum(g, n_tiles - 1))
    lag = lambda g: tile_of(jnp.maximum(g - 1, 0))
    row = lambda w: pl.BlockSpec((tm, w), lambda g: (cur(g), 0))
    rope = pl.BlockSpec((tm, LANE), lambda g: (nt - 1 - jnp.minimum(g, n_tiles - 1) % nt, 0))
    out_shapes = (
        jax.ShapeDtypeStruct((n_rows, D_MODEL), F32),
        jax.ShapeDtypeStruct((n_rows, ATTN_Q), BF16),
        jax.ShapeDtypeStruct((n_rows, ATTN_KV), BF16),
        jax.ShapeDtypeStruct((ATTN_KV, n_rows), BF16),
        jax.ShapeDtypeStruct((n_rows, GLA_QK), F32),
        jax.ShapeDtypeStruct((n_rows, GLA_QK), F32),
        jax.ShapeDtypeStruct((n_rows, GLA_V), BF16),
        jax.ShapeDtypeStruct((n_rows, GLA_V), F32),
        jax.ShapeDtypeStruct((n_rows, 2 * GLA_RANK), BF16),
        jax.ShapeDtypeStruct((n_rows, GLA_V), F32),
    )
    return pl.pallas_call(
        functools.partial(_reverse_sweep_kernel, tiles_per_seq=nt),
        grid=(n_tiles + 1,),
        in_specs=[
            row(D_MODEL),
            _const_spec((1, D_MODEL)),
            _const_spec((D_MODEL, D_FF)), _const_spec((D_MODEL, D_FF)), _const_spec((D_FF, D_MODEL)),
            _const_spec((1, D_MODEL)),
            _const_spec((D_MODEL, IN_PROJ_PAD)),
            rope, rope,
            _const_spec((2 * GLA_RANK, GLA_QK)), _const_spec((1, GLA_QK)), _const_spec((GLA_BLOCK, GLA_BLOCK)),
        ],
        out_specs=[row(D_MODEL), row(ATTN_Q), row(ATTN_KV), pl.BlockSpec((ATTN_KV, tm), lambda g: (0, cur(g))),
                   row(GLA_QK), row(GLA_QK),
                   row(GLA_V), row(GLA_V), row(2 * GLA_RANK),
                   pl.BlockSpec((tm, GLA_V), lambda g: (lag(g), 0))],
        out_shape=out_shapes,
        scratch_shapes=[pltpu.VMEM((tm, D_FF), BF16),
                        pltpu.VMEM((2, tm, GLA_QK), F32), pltpu.VMEM((2, tm, GLA_QK), F32),
                        pltpu.VMEM((2, tm, GLA_V), BF16), pltpu.VMEM((2, tm, 2 * GLA_RANK), BF16),
                        pltpu.VMEM((GLA_QK, GLA_DV), F32)],
        compiler_params=pltpu.CompilerParams(dimension_semantics=("arbitrary",), vmem_limit_bytes=VMEM_LIMIT),
        name="reverse_sweep",
    )(x2d, n1, wg, wu, wd, nm, win, cos_tab, sin_tab, wdec, bdec, tri_up)


ATTN_UNITS_PER_TILE = (ROW_TILE // ATTN_BLOCK) * N_KV_HEADS
ATTN_STAGES_PER_TILE = 3 * ATTN_UNITS_PER_TILE


def _attention_tile_stages(sink_ref, qa_ref, kp_ref, kc_ref, kn_ref, vp_ref, vc_ref, vn_ref, mix_ref, tpos,
                           tiles_per_seq):
    sub = ROW_TILE // ATTN_BLOCK
    n_qblocks = tiles_per_seq * sub
    n_keys = 3 * ATTN_BLOCK
    n_cols = ATTN_GROUP * ATTN_BLOCK
    kbuf = jnp.concatenate([kp_ref[...], kc_ref[...], kn_ref[...]], axis=0)
    vbuf_t = jnp.concatenate([vp_ref[...], vc_ref[...], vn_ref[...]], axis=1)
    kj = lax.broadcasted_iota(jnp.int32, (n_keys, n_cols), 0)
    col = lax.broadcasted_iota(jnp.int32, (n_keys, n_cols), 1)
    qi = col % ATTN_BLOCK
    in_window = (kj >= qi) & (kj <= qi + 2 * ATTN_BLOCK)
    head_of_col = lax.broadcasted_iota(jnp.int32, (1, n_cols), 1) // ATTN_BLOCK
    pending = {}

    def unit(jb, kv):
        qblk = tpos * sub + jb
        qrows = slice(jb * ATTN_BLOCK, (jb + 1) * ATTN_BLOCK)
        krows = slice(jb * ATTN_BLOCK, (jb + 3) * ATTN_BLOCK)
        kvl = slice(kv * HEAD_DIM, (kv + 1) * HEAD_DIM)
        heads = range(kv * ATTN_GROUP, (kv + 1) * ATTN_GROUP)
        qs = jnp.concatenate([qa_ref[qrows, h * HEAD_DIM:(h + 1) * HEAD_DIM] for h in heads], axis=0)
        s_t = lax.dot_general(kbuf[krows, kvl], qs, (((1,), (1,)), ((), ())), preferred_element_type=F32)
        yield
        mask = in_window & ((kj >= ATTN_BLOCK) | (qblk > 0)) & ((kj < 2 * ATTN_BLOCK) | (qblk < n_qblocks - 1))
        s_t = jnp.where(mask, s_t, -1e30)
        sink = jnp.full((1, n_cols), sink_ref[heads[-1]], F32)
        for hl in range(ATTN_GROUP - 2, -1, -1):
            sink = jnp.where(head_of_col == hl, sink_ref[heads[hl]], sink)
        m = jnp.maximum(jnp.max(s_t, axis=0, keepdims=True), sink)
        p = jnp.exp(s_t - m)
        denom = jnp.sum(p, axis=0, keepdims=True) + jnp.exp(sink - m)
        p_t = p.astype(BF16)
        yield
        pending[kv] = jnp.dot(vbuf_t[kvl, krows], p_t, preferred_element_type=F32) / denom
        if kv == N_KV_HEADS - 1:
            o_t = jnp.concatenate([pending.pop(i) for i in range(N_KV_HEADS)], axis=0)
            for hl in range(ATTN_GROUP):
                cols = slice(hl * ATTN_BLOCK, (hl + 1) * ATTN_BLOCK)
                mix_ref[qrows, hl * ATTN_KV:(hl + 1) * ATTN_KV] = o_t[:, cols].T.astype(BF16)
        yield

    units = [unit(jb, kv) for jb in range(sub) for kv in range(N_KV_HEADS)]
    for slot in range(len(units) + 4):
        for stage in range(3):
            u = slot - 2 * stage
            if 0 <= u < len(units):
                next(units[u])
                yield


def _forward_sweep_kernel(sink_ref, qa_ref, kp_ref, kc_ref, kn_ref, vp_ref, vc_ref, vn_ref,
                          gq_ref, gk_ref, gv_ref, gg_ref, r_ref, ob_ref, wdec_ref, bdec_ref, tri_ref, gnorm_ref,
                          x1_ref, wout_ref, n2_ref, wg_ref, wu_ref, wd_ref, nf_ref,
                          y_ref, act_ref, mix_ref, s_ref, *, tiles_per_seq):
    g = pl.program_id(0)
    mix_cur = mix_ref.at[g % 2]
    mix_prev = mix_ref.at[1 - g % 2]

    @pl.when(g == 0)
    def _():
        mix_ref[...] = jnp.zeros_like(mix_ref)

    @pl.when(g % tiles_per_seq == 0)
    def _():
        s_ref[...] = jnp.zeros_like(s_ref)

    tpos = jnp.minimum(g, pl.num_programs(0) - 2) % tiles_per_seq
    gain = gnorm_ref[...]

    def emit_mix(row0, h, o):
        rows = slice(row0, row0 + GLA_CHUNK)
        cols = slice(h * GLA_DV, (h + 1) * GLA_DV)
        o = _rms(o + ob_ref[rows, cols], gain) * _silu(gg_ref[rows, cols])
        mix_cur[rows, ATTN_Q + h * GLA_DV:ATTN_Q + (h + 1) * GLA_DV] = o.astype(BF16)

    side = _round_robin(
        _attention_tile_stages(sink_ref, qa_ref, kp_ref, kc_ref, kn_ref, vp_ref, vc_ref, vn_ref, mix_cur, tpos,
                               tiles_per_seq),
        *_gla_tile_streams(gq_ref, gk_ref, gv_ref, r_ref, wdec_ref, bdec_ref, tri_ref, s_ref, False, emit_mix))
    side_per_tick = -(-(ATTN_STAGES_PER_TILE + GLA_STAGES_PER_TILE) // FFN_TICKS)

    x2 = x1_ref[...] + jnp.dot(mix_prev[...], wout_ref[...], preferred_element_type=F32)
    x3 = _swiglu_residual(x2, n2_ref, wg_ref, wu_ref, wd_ref, act_ref, side, side_per_tick)
    y_ref[...] = _rms(x3, nf_ref[...])
    for _ in side:
        pass


def _forward_sweep(seq_len, sink, qa, ka, va, gq, gk, gv, gg, r, ob, wdec, bdec, tri_lo, gnorm,
                   x1, wout, n2, wg, wu, wd, nf):
    n_rows = x1.shape[0]
    tm = ROW_TILE
    nt = seq_len // tm
    n_tiles = n_rows // tm
    sub = tm // ATTN_BLOCK
    n_halo = n_rows // ATTN_BLOCK
    halo_per_seq = seq_len // ATTN_BLOCK
    cur = lambda g: jnp.minimum(g, n_tiles - 1)
    lag = lambda g: jnp.maximum(g - 1, 0)
    row = lambda w: pl.BlockSpec((tm, w), lambda g: (cur(g), 0))
    lag_row = lambda w: pl.BlockSpec((tm, w), lambda g: (lag(g), 0))

    def prev_idx(g):
        t = cur(g)
        return jnp.maximum(t * sub - 1, (t // nt) * halo_per_seq)

    def next_idx(g):
        t = cur(g)
        return jnp.minimum((t + 1) * sub, (t // nt + 1) * halo_per_seq - 1)

    prev = pl.BlockSpec((ATTN_BLOCK, ATTN_KV), lambda g: (prev_idx(g), 0))
    nxt = pl.BlockSpec((ATTN_BLOCK, ATTN_KV), lambda g: (next_idx(g), 0))
    prev_t = pl.BlockSpec((ATTN_KV, ATTN_BLOCK), lambda g: (0, prev_idx(g)))
    cur_t = pl.BlockSpec((ATTN_KV, tm), lambda g: (0, cur(g)))
    nxt_t = pl.BlockSpec((ATTN_KV, ATTN_BLOCK), lambda g: (0, next_idx(g)))
    return pl.pallas_call(
        functools.partial(_forward_sweep_kernel, tiles_per_seq=nt),
        grid=(n_tiles + 1,),
        in_specs=[pl.BlockSpec(memory_space=pltpu.SMEM),
                  row(ATTN_Q), prev, row(ATTN_KV), nxt, prev_t, cur_t, nxt_t,
                  row(GLA_QK), row(GLA_QK), row(GLA_V), row(GLA_V), row(2 * GLA_RANK), row(GLA_V),
                  _const_spec((2 * GLA_RANK, GLA_QK)), _const_spec((1, GLA_QK)), _const_spec((GLA_BLOCK, GLA_BLOCK)),
                  _const_spec((1, GLA_DV)),
                  lag_row(D_MODEL), _const_spec((D_MODEL, D_MODEL)), _const_spec((1, D_MODEL)),
                  _const_spec((D_MODEL, D_FF)), _const_spec((D_MODEL, D_FF)), _const_spec((D_FF, D_MODEL)),
                  _const_spec((1, D_MODEL))],
        out_specs=lag_row(D_MODEL),
        out_shape=jax.ShapeDtypeStruct((n_rows, D_MODEL), F32),
        scratch_shapes=[pltpu.VMEM((tm, D_FF), BF16), pltpu.VMEM((2, tm, D_MODEL), BF16),
                        pltpu.VMEM((GLA_QK, GLA_DV), F32)],
        compiler_params=pltpu.CompilerParams(dimension_semantics=("arbitrary",), vmem_limit_bytes=VMEM_LIMIT),
        name="forward_sweep",
    )(sink, qa, ka, ka, ka, va, va, va, gq, gk, gv, gg, r, ob, wdec, bdec, tri_lo, gnorm,
      x1, wout, n2, wg, wu, wd, nf)


def _rope_tables(seq_len):
    half = HEAD_DIM // 2
    inv_freq = ROPE_THETA ** (-jnp.arange(half, dtype=F32) / half)
    ang = jnp.arange(seq_len, dtype=F32)[:, None] * inv_freq[None, :]
    cos, sin = jnp.cos(ang), jnp.sin(ang)
    cos_tab = jnp.tile(cos, (1, LANE // half))
    sin_tab = jnp.tile(jnp.concatenate([-sin, sin], axis=1), (1, LANE // HEAD_DIM))
    return cos_tab, sin_tab


def _chunk_tri(bt, upper):
    i = np.arange(bt)[:, None]
    j = np.arange(bt)[None, :]
    same_chunk = (i // GLA_CHUNK) == (j // GLA_CHUNK)
    keep = (j >= i) if upper else (j <= i)
    return jnp.asarray(same_chunk & keep, dtype=BF16)


def _trunk(x, p):
    batch, seq_len, _ = x.shape
    cos_tab, sin_tab = _rope_tables(seq_len)
    x2d = x.reshape(batch * seq_len, D_MODEL)
    x1, qa, ka, va, gq, gk, gv, gg, r, ob = _reverse_sweep(
        x2d, seq_len, p["n1"], p["wg1"], p["wu1"], p["wd1"], p["nm"], p["win"], cos_tab, sin_tab,
        p["wdec_b"], p["bdec_b"], _chunk_tri(GLA_BLOCK, True))
    y = _forward_sweep(seq_len, p["sink"], qa, ka, va, gq, gk, gv, gg, r, ob,
                       p["wdec_f"], p["bdec_f"], _chunk_tri(GLA_BLOCK, False), p["gnorm"],
                       x1, p["wout"], p["n2"], p["wg2"], p["wu2"], p["wd2"], p["nf"])
    return y.reshape(batch, seq_len, D_MODEL)


def kernel(x_prompt, x_sample, norm_ffn1, w_ffn1_gate, w_ffn1_up, w_ffn1_down, norm_mix, w_in, attn_sink, w_gla_decay_fwd, b_gla_decay_fwd, w_gla_decay_bwd, b_gla_decay_bwd, gla_out_norm, w_out, norm_ffn2, w_ffn2_gate, w_ffn2_up, w_ffn2_down, norm_final):
    assert norm_ffn1.shape[0] == 1, "single-layer trunk"
    zeros_rank = jnp.zeros((GLA_RANK, GLA_QK), F32)
    p = dict(
        n1=norm_ffn1[0][None, :], wg1=w_ffn1_gate[0].astype(BF16), wu1=w_ffn1_up[0].astype(BF16),
        wd1=w_ffn1_down[0].astype(BF16),
        nm=norm_mix[0][None, :],
        win=jnp.pad(w_in[0], ((0, 0), (0, IN_PROJ_PAD - IN_PROJ_WIDTH))).astype(BF16),
        sink=attn_sink[0],
        wdec_f=jnp.concatenate([w_gla_decay_fwd[0], zeros_rank], axis=0).astype(BF16),
        bdec_f=b_gla_decay_fwd[0][None, :],
        wdec_b=jnp.concatenate([zeros_rank, w_gla_decay_bwd[0]], axis=0).astype(BF16),
        bdec_b=b_gla_decay_bwd[0][None, :],
        gnorm=gla_out_norm[0][None, :],
        wout=jnp.concatenate([
            w_out[0][:ATTN_Q].reshape(N_KV_HEADS, ATTN_GROUP, HEAD_DIM, D_MODEL).transpose(1, 0, 2, 3)
            .reshape(ATTN_Q, D_MODEL), w_out[0][ATTN_Q:]], axis=0).astype(BF16),
        n2=norm_ffn2[0][None, :], wg2=w_ffn2_gate[0].astype(BF16), wu2=w_ffn2_up[0].astype(BF16),
        wd2=w_ffn2_down[0].astype(BF16),
        nf=norm_final[None, :],
    )
    return _trunk(x_prompt, p), _trunk(x_sample, p)
```

```python
import functools

import jax
import jax.numpy as jnp
import numpy as np
from jax import lax
from jax.experimental import pallas as pl
from jax.experimental.pallas import tpu as pltpu

F32 = jnp.float32
BF16 = jnp.bfloat16

D_MODEL = 1024
D_FF = 2816
EPS = 1e-6
N_ATTN_HEADS = 8
N_KV_HEADS = 2
ATTN_GROUP = N_ATTN_HEADS // N_KV_HEADS
HEAD_DIM = 64
ATTN_BLOCK = 128
ROPE_THETA = 10000.0
N_GLA_HEADS = 4
GLA_DK = 64
GLA_DV = 128
GLA_RANK = 16
GLA_GATE_NORMALIZER = 16.0
GLA_CHUNK = 64
ATTN_Q = N_ATTN_HEADS * HEAD_DIM
ATTN_KV = N_KV_HEADS * HEAD_DIM
GLA_QK = N_GLA_HEADS * GLA_DK
GLA_V = N_GLA_HEADS * GLA_DV
IN_PROJ_WIDTH = ATTN_Q + 2 * ATTN_KV + 2 * GLA_QK + 2 * GLA_V + 2 * GLA_RANK
LANE = 128
MXU_COLS = 256
IN_PROJ_PAD = ((IN_PROJ_WIDTH + LANE - 1) // LANE) * LANE
OFF_AQ = 0
OFF_AK = OFF_AQ + ATTN_Q
OFF_AV = OFF_AK + ATTN_KV
OFF_GQ = OFF_AV + ATTN_KV
OFF_GK = OFF_GQ + GLA_QK
OFF_GV = OFF_GK + GLA_QK
OFF_GG = OFF_GV + GLA_V
OFF_R = OFF_GG + GLA_V

ROW_TILE = 512
ROW_SLAB = 256
FF_CHUNK = 256
GLA_BLOCK = 256
VMEM_LIMIT = 56 * 1024 * 1024


def _rms(x, gain):
    return x * lax.rsqrt(jnp.mean(x * x, axis=-1, keepdims=True) + EPS) * gain


def _silu(x):
    return x * (1.0 / (1.0 + jnp.exp(-x)))


def _advance(side, n):
    for _ in range(n):
        next(side, None)


FFN_TICKS = 1 + D_FF // FF_CHUNK + D_MODEL // MXU_COLS


def _swiglu_residual(x_slabs, gain_ref, wg_ref, wu_ref, wd_ref, act_ref, side, side_per_tick):
    _advance(side, side_per_tick)
    gain = gain_ref[...]
    hs = [_rms(x, gain).astype(BF16) for x in x_slabs]

    def up_chunk(h, rows, c):
        sl = slice(c * FF_CHUNK, (c + 1) * FF_CHUNK)
        g = jnp.dot(h, wg_ref[:, sl], preferred_element_type=F32)
        u = jnp.dot(h, wu_ref[:, sl], preferred_element_type=F32)
        act_ref[rows, sl] = (_silu(g) * u).astype(BF16)

    row0 = 0
    for h in hs:
        up_chunk(h, slice(row0, row0 + h.shape[0]), 0)
        row0 += h.shape[0]
    _advance(side, side_per_tick)
    h = jnp.concatenate(hs, axis=0)
    for c in range(1, D_FF // FF_CHUNK):
        up_chunk(h, slice(0, row0), c)
        _advance(side, side_per_tick)
    x = jnp.concatenate(x_slabs, axis=0)
    out = []
    for j in range(D_MODEL // MXU_COLS):
        cols = slice(j * MXU_COLS, (j + 1) * MXU_COLS)
        y = jnp.dot(act_ref[...], wd_ref[:, cols], preferred_element_type=F32)
        out.append(x[:, cols] + 0.5 * y)
        _advance(side, side_per_tick)
    return jnp.concatenate(out, axis=1)


def _rope_pair(x, cos, sin_signed, first_half):
    swapped = jnp.where(first_half, pltpu.roll(x, LANE - HEAD_DIM // 2, 1), pltpu.roll(x, HEAD_DIM // 2, 1))
    return x * cos + swapped * sin_signed


GLA_CHUNKS_PER_BLOCK = GLA_BLOCK // GLA_CHUNK
GLA_BLOCKS_PER_TILE = ROW_TILE // GLA_BLOCK
GLA_STAGES_PER_TILE = GLA_BLOCKS_PER_TILE * (4 + 2 * GLA_CHUNKS_PER_BLOCK)


def _round_robin(*stage_generators):
    live = list(stage_generators)
    while live:
        for gen in list(live):
            try:
                next(gen)
                yield
            except StopIteration:
                live.remove(gen)


def _gla_block_stages(q_ref, k_ref, v_ref, r_ref, brows, wdec_ref, bdec_ref, tri_ref, s_ref, reverse, emit):
    bt = brows.stop - brows.start
    nch = bt // GLA_CHUNK
    z = jnp.dot(r_ref[brows, :], wdec_ref[...], preferred_element_type=F32) + bdec_ref[...]
    yield
    log_a = (jnp.minimum(z, 0.0) - jnp.log1p(jnp.exp(-jnp.abs(z)))) * (1.0 / GLA_GATE_NORMALIZER)
    hi = log_a.astype(BF16)
    lo = (log_a - hi.astype(F32)).astype(BF16)
    tri = tri_ref[...]
    cum = jnp.dot(tri, hi, preferred_element_type=F32) + jnp.dot(tri, lo, preferred_element_type=F32)
    yield
    q = q_ref[brows, :]
    k = k_ref[brows, :]
    cum3 = cum.reshape(nch, GLA_CHUNK, GLA_QK)
    edge = GLA_CHUNK - 1 if not reverse else 0
    tot3 = cum3[:, edge:edge + 1, :]
    rest = (tot3 - cum3).reshape(bt, GLA_QK)
    qe = (q * (GLA_DK ** -0.5) * jnp.exp(cum)).astype(BF16)
    ke = (k * jnp.exp(-cum)).astype(BF16)
    ks_t = (k * jnp.exp(rest)).T.astype(BF16)
    tot = tot3.reshape(nch, GLA_QK)
    tot_t = jnp.concatenate([tot, jnp.zeros((LANE - nch, GLA_QK), F32)], axis=0).T
    decay_t = jnp.exp(tot_t)

    ii = lax.broadcasted_iota(jnp.int32, (GLA_CHUNK, GLA_CHUNK), 0)
    jj = lax.broadcasted_iota(jnp.int32, (GLA_CHUNK, GLA_CHUNK), 1)
    keep = (jj > ii) if reverse else (jj <= ii)
    rows = [slice(n * GLA_CHUNK, (n + 1) * GLA_CHUNK) for n in range(nch)]
    klanes = [slice(h * GLA_DK, (h + 1) * GLA_DK) for h in range(N_GLA_HEADS)]

    def v_of(n, h):
        return v_ref[brows.start + n * GLA_CHUNK:brows.start + (n + 1) * GLA_CHUNK, h * GLA_DV:(h + 1) * GLA_DV]

    yield

    a = [[None] * N_GLA_HEADS for _ in range(nch)]
    u = [None] * nch
    for n in range(nch):
        for h in range(N_GLA_HEADS):
            s_nh = lax.dot_general(qe[rows[n], klanes[h]], ke[rows[n], klanes[h]], (((1,), (1,)), ((), ())),
                                   preferred_element_type=F32)
            a[n][h] = jnp.where(keep, s_nh, 0.0).astype(BF16)
        u[n] = jnp.concatenate([jnp.dot(ks_t[klanes[h], rows[n]], v_of(n, h), preferred_element_type=F32)
                                for h in range(N_GLA_HEADS)], axis=0)
        yield
    s = s_ref[...]
    s_in = [None] * nch
    for n in (range(nch - 1, -1, -1) if reverse else range(nch)):
        s_in[n] = s.astype(BF16)
        s = decay_t[:, n:n + 1] * s + u[n]
    s_ref[...] = s
    yield
    for n in range(nch):
        for h in range(N_GLA_HEADS):
            emit(n, h, jnp.dot(a[n][h], v_of(n, h), preferred_element_type=F32)
                 + jnp.dot(qe[rows[n], klanes[h]], s_in[n][klanes[h], :], preferred_element_type=F32))
        yield


def _gla_tile_streams(q_ref, k_ref, v_ref, r_ref, wdec_ref, bdec_ref, tri_ref, s_ref, reverse, emit):
    streams = []
    for blk in (range(GLA_BLOCKS_PER_TILE - 1, -1, -1) if reverse else range(GLA_BLOCKS_PER_TILE)):
        brows = slice(blk * GLA_BLOCK, (blk + 1) * GLA_BLOCK)
        emit_block = lambda n, h, o, base=blk * GLA_BLOCK: emit(base + n * GLA_CHUNK, h, o)
        streams.append(_gla_block_stages(q_ref, k_ref, v_ref, r_ref, brows, wdec_ref, bdec_ref, tri_ref, s_ref,
                                         reverse, emit_block))
    return streams


def _reverse_sweep_kernel(x_ref, n1_ref, wg_ref, wu_ref, wd_ref, nm_ref, win_ref, cos_ref, sin_ref,
                          wdec_ref, bdec_ref, tri_ref,
                          x1_ref, qa_ref, ka_ref, va_ref, gq_ref, gk_ref, gv_ref, gg_ref, r_ref, ob_ref,
                          act_ref, pq_ref, pk_ref, pv_ref, pr_ref, s_ref, *, tiles_per_seq):
    g = pl.program_id(0)
    cur_slot = g % 2
    prev_slot = 1 - cur_slot

    @pl.when(g == 0)
    def _():
        pq_ref[...] = jnp.zeros_like(pq_ref)
        pk_ref[...] = jnp.zeros_like(pk_ref)
        pv_ref[...] = jnp.zeros_like(pv_ref)
        pr_ref[...] = jnp.zeros_like(pr_ref)

    @pl.when(jnp.maximum(g - 1, 0) % tiles_per_seq == 0)
    def _():
        s_ref[...] = jnp.zeros_like(s_ref)

    def emit_ob(row0, h, o):
        ob_ref[row0:row0 + GLA_CHUNK, h * GLA_DV:(h + 1) * GLA_DV] = o

    side = _round_robin(*_gla_tile_streams(pq_ref.at[prev_slot], pk_ref.at[prev_slot], pv_ref.at[prev_slot],
                                           pr_ref.at[prev_slot], wdec_ref, bdec_ref, tri_ref, s_ref, True, emit_ob))

    x_slabs = [x_ref[r0:r0 + ROW_SLAB, :] for r0 in range(0, ROW_TILE, ROW_SLAB)]
    x1 = _swiglu_residual(x_slabs, n1_ref, wg_ref, wu_ref, wd_ref, act_ref, side, 1)
    x1_ref[...] = x1
    h = jnp.concatenate([_rms(x1[r0:r0 + ROW_SLAB, :], nm_ref[...]).astype(BF16)
                         for r0 in range(0, ROW_TILE, ROW_SLAB)], axis=0)
    cos = cos_ref[...]
    sin = sin_ref[...]
    lane = lax.broadcasted_iota(jnp.int32, (1, LANE), 1)
    first_half = (lane % HEAD_DIM) < (HEAD_DIM // 2)

    def proj(off, width):
        _advance(side, 1)
        return jnp.dot(h, win_ref[:, off:off + width], preferred_element_type=F32)

    scale = HEAD_DIM ** -0.5
    for j in range(ATTN_Q // MXU_COLS):
        q2 = proj(OFF_AQ + j * MXU_COLS, MXU_COLS)
        for i in range(MXU_COLS // LANE):
            q = _rope_pair(q2[:, i * LANE:(i + 1) * LANE], cos, sin, first_half)
            qa_ref[:, j * MXU_COLS + i * LANE:j * MXU_COLS + (i + 1) * LANE] = (q * scale).astype(BF16)
    kv = proj(OFF_AK, 2 * ATTN_KV)
    ka_ref[...] = _rope_pair(kv[:, :ATTN_KV], cos, sin, first_half).astype(BF16)
    va_ref[...] = kv[:, ATTN_KV:].T.astype(BF16)
    gq = proj(OFF_GQ, GLA_QK)
    gq_ref[...] = gq
    pq_ref[cur_slot] = gq
    gk = proj(OFF_GK, GLA_QK)
    gk_ref[...] = gk
    pk_ref[cur_slot] = gk
    for j in range(GLA_V // MXU_COLS):
        cols = slice(j * MXU_COLS, (j + 1) * MXU_COLS)
        gv = proj(OFF_GV + j * MXU_COLS, MXU_COLS).astype(BF16)
        gv_ref[:, cols] = gv
        pv_ref[cur_slot, :, cols] = gv
        gg_ref[:, cols] = proj(OFF_GG + j * MXU_COLS, MXU_COLS)
    rr = proj(OFF_R, LANE)[:, :2 * GLA_RANK].astype(BF16)
    r_ref[...] = rr
    pr_ref[cur_slot] = rr
    for _ in side:
        pass


def _const_spec(shape):
    return pl.BlockSpec(shape, lambda *_: (0,) * len(shape), pipeline_mode=pl.Buffered(1))


def _reverse_sweep(x2d, seq_len, n1, wg, wu, wd, nm, win, cos_tab, sin_tab, wdec, bdec, tri_up):
    n_rows = x2d.shape[0]
    tm = ROW_TILE
    assert n_rows % tm == 0 and seq_len % tm == 0
    nt = seq_len // tm
    n_tiles = n_rows // tm

    def tile_of(step):
        return (step // nt) * nt + (nt - 1 - step % nt)

    cur = lambda g: tile_of(jnp.minimum(g, n_tiles - 1))
    lag = lambda g: tile_of(jnp.maximum(g - 1, 0))
    row = lambda w: pl.BlockSpec((tm, w), lambda g: (cur(g), 0))
    rope = pl.BlockSpec((tm, LANE), lambda g: (nt - 1 - jnp.minimum(g, n_tiles - 1) % nt, 0))
    out_shapes = (
        jax.ShapeDtypeStruct((n_rows, D_MODEL), F32),
        jax.ShapeDtypeStruct((n_rows, ATTN_Q), BF16),
        jax.ShapeDtypeStruct((n_rows, ATTN_KV), BF16),
        jax.ShapeDtypeStruct((ATTN_KV, n_rows), BF16),
        jax.ShapeDtypeStruct((n_rows, GLA_QK), F32),
        jax.ShapeDtypeStruct((n_rows, GLA_QK), F32),
        jax.ShapeDtypeStruct((n_rows, GLA_V), BF16),
        jax.ShapeDtypeStruct((n_rows, GLA_V), F32),
        jax.ShapeDtypeStruct((n_rows, 2 * GLA_RANK), BF16),
        jax.ShapeDtypeStruct((n_rows, GLA_V), F32),
    )
    return pl.pallas_call(
        functools.partial(_reverse_sweep_kernel, tiles_per_seq=nt),
        grid=(n_tiles + 1,),
        in_specs=[
            row(D_MODEL),
            _const_spec((1, D_MODEL)),
            _const_spec((D_MODEL, D_FF)), _const_spec((D_MODEL, D_FF)), _const_spec((D_FF, D_MODEL)),
            _const_spec((1, D_MODEL)),
            _const_spec((D_MODEL, IN_PROJ_PAD)),
            rope, rope,
            _const_spec((2 * GLA_RANK, GLA_QK)), _const_spec((1, GLA_QK)), _const_spec((GLA_BLOCK, GLA_BLOCK)),
        ],
        out_specs=[row(D_MODEL), row(ATTN_Q), row(ATTN_KV), pl.BlockSpec((ATTN_KV, tm), lambda g: (0, cur(g))),
                   row(GLA_QK), row(GLA_QK),
                   row(GLA_V), row(GLA_V), row(2 * GLA_RANK),
                   pl.BlockSpec((tm, GLA_V), lambda g: (lag(g), 0))],
        out_shape=out_shapes,
        scratch_shapes=[pltpu.VMEM((tm, D_FF), BF16),
                        pltpu.VMEM((2, tm, GLA_QK), F32), pltpu.VMEM((2, tm, GLA_QK), F32),
                        pltpu.VMEM((2, tm, GLA_V), BF16), pltpu.VMEM((2, tm, 2 * GLA_RANK), BF16),
                        pltpu.VMEM((GLA_QK, GLA_DV), F32)],
        compiler_params=pltpu.CompilerParams(dimension_semantics=("arbitrary",), vmem_limit_bytes=VMEM_LIMIT),
        name="reverse_sweep",
    )(x2d, n1, wg, wu, wd, nm, win, cos_tab, sin_tab, wdec, bdec, tri_up)


ATTN_UNITS_PER_TILE = (ROW_TILE // ATTN_BLOCK) * N_KV_HEADS
ATTN_STAGES_PER_TILE = 3 * ATTN_UNITS_PER_TILE


def _attention_tile_stages(sink_ref, qa_ref, kp_ref, kc_ref, kn_ref, vp_ref, vc_ref, vn_ref, mix_ref, tpos,
                           tiles_per_seq):
    sub = ROW_TILE // ATTN_BLOCK
    n_qblocks = tiles_per_seq * sub
    n_keys = 3 * ATTN_BLOCK
    n_cols = ATTN_GROUP * ATTN_BLOCK
    kbuf = jnp.concatenate([kp_ref[...], kc_ref[...], kn_ref[...]], axis=0)
    vbuf_t = jnp.concatenate([vp_ref[...], vc_ref[...], vn_ref[...]], axis=1)
    kj = lax.broadcasted_iota(jnp.int32, (n_keys, n_cols), 0)
    col = lax.broadcasted_iota(jnp.int32, (n_keys, n_cols), 1)
    qi = col % ATTN_BLOCK
    in_window = (kj >= qi) & (kj <= qi + 2 * ATTN_BLOCK)
    head_of_col = lax.broadcasted_iota(jnp.int32, (1, n_cols), 1) // ATTN_BLOCK
    pending = {}

    def unit(jb, kv):
        qblk = tpos * sub + jb
        qrows = slice(jb * ATTN_BLOCK, (jb + 1) * ATTN_BLOCK)
        krows = slice(jb * ATTN_BLOCK, (jb + 3) * ATTN_BLOCK)
        kvl = slice(kv * HEAD_DIM, (kv + 1) * HEAD_DIM)
        heads = range(kv * ATTN_GROUP, (kv + 1) * ATTN_GROUP)
        qs = jnp.concatenate([qa_ref[qrows, h * HEAD_DIM:(h + 1) * HEAD_DIM] for h in heads], axis=0)
        s_t = lax.dot_general(kbuf[krows, kvl], qs, (((1,), (1,)), ((), ())), preferred_element_type=F32)
        yield
        mask = in_window & ((kj >= ATTN_BLOCK) | (qblk > 0)) & ((kj < 2 * ATTN_BLOCK) | (qblk < n_qblocks - 1))
        s_t = jnp.where(mask, s_t, -1e30)
        sink = jnp.full((1, n_cols), sink_ref[heads[-1]], F32)
        for hl in range(ATTN_GROUP - 2, -1, -1):
            sink = jnp.where(head_of_col == hl, sink_ref[heads[hl]], sink)
        m = jnp.maximum(jnp.max(s_t, axis=0, keepdims=True), sink)
        p = jnp.exp(s_t - m)
        denom = jnp.sum(p, axis=0, keepdims=True) + jnp.exp(sink - m)
        p_t = p.astype(BF16)
        yield
        pending[kv] = jnp.dot(vbuf_t[kvl, krows], p_t, preferred_element_type=F32) / denom
        if kv == N_KV_HEADS - 1:
            o_t = jnp.concatenate([pending.pop(i) for i in range(N_KV_HEADS)], axis=0)
            for hl in range(ATTN_GROUP):
                cols = slice(hl * ATTN_BLOCK, (hl + 1) * ATTN_BLOCK)
                mix_ref[qrows, hl * ATTN_KV:(hl + 1) * ATTN_KV] = o_t[:, cols].T.astype(BF16)
        yield

    units = [unit(jb, kv) for jb in range(sub) for kv in range(N_KV_HEADS)]
    for slot in range(len(units) + 4):
        for stage in range(3):
            u = slot - 2 * stage
            if 0 <= u < len(units):
                next(units[u])
                yield


def _forward_sweep_kernel(sink_ref, qa_ref, kp_ref, kc_ref, kn_ref, vp_ref, vc_ref, vn_ref,
                          gq_ref, gk_ref, gv_ref, gg_ref, r_ref, ob_ref, wdec_ref, bdec_ref, tri_ref, gnorm_ref,
                          x1_ref, wout_ref, n2_ref, wg_ref, wu_ref, wd_ref, nf_ref,
                          y_ref, act_ref, mix_ref, s_ref, *, tiles_per_seq):
    g = pl.program_id(0)
    mix_cur = mix_ref.at[g % 2]
    mix_prev = mix_ref.at[1 - g % 2]

    @pl.when(g == 0)
    def _():
        mix_ref[...] = jnp.zeros_like(mix_ref)

    @pl.when(g % tiles_per_seq == 0)
    def _():
        s_ref[...] = jnp.zeros_like(s_ref)

    tpos = jnp.minimum(g, pl.num_programs(0) - 2) % tiles_per_seq
    gain = gnorm_ref[...]

    def emit_mix(row0, h, o):
        rows = slice(row0, row0 + GLA_CHUNK)
        cols = slice(h * GLA_DV, (h + 1) * GLA_DV)
        o = _rms(o + ob_ref[rows, cols], gain) * _silu(gg_ref[rows, cols])
        mix_cur[rows, ATTN_Q + h * GLA_DV:ATTN_Q + (h + 1) * GLA_DV] = o.astype(BF16)

    side = _round_robin(
        _attention_tile_stages(sink_ref, qa_ref, kp_ref, kc_ref, kn_ref, vp_ref, vc_ref, vn_ref, mix_cur, tpos,
                               tiles_per_seq),
        *_gla_tile_streams(gq_ref, gk_ref, gv_ref, r_ref, wdec_ref, bdec_ref, tri_ref, s_ref, False, emit_mix))
    side_per_tick = -(-(ATTN_STAGES_PER_TILE + GLA_STAGES_PER_TILE) // FFN_TICKS)

    x2_slabs = [x1_ref[r0:r0 + ROW_SLAB, :]
                + jnp.dot(mix_prev[r0:r0 + ROW_SLAB, :], wout_ref[...], preferred_element_type=F32)
                for r0 in range(0, ROW_TILE, ROW_SLAB)]
    x3 = _swiglu_residual(x2_slabs, n2_ref, wg_ref, wu_ref, wd_ref, act_ref, side, side_per_tick)
    for r0 in range(0, ROW_TILE, ROW_SLAB):
        y_ref[r0:r0 + ROW_SLAB, :] = _rms(x3[r0:r0 + ROW_SLAB, :], nf_ref[...])
    for _ in side:
        pass


def _forward_sweep(seq_len, sink, qa, ka, va, gq, gk, gv, gg, r, ob, wdec, bdec, tri_lo, gnorm,
                   x1, wout, n2, wg, wu, wd, nf):
    n_rows = x1.shape[0]
    tm = ROW_TILE
    nt = seq_len // tm
    n_tiles = n_rows // tm
    sub = tm // ATTN_BLOCK
    halo_per_seq = seq_len // ATTN_BLOCK
    cur = lambda g: jnp.minimum(g, n_tiles - 1)
    lag = lambda g: jnp.maximum(g - 1, 0)
    row = lambda w: pl.BlockSpec((tm, w), lambda g: (cur(g), 0))
    lag_row = lambda w: pl.BlockSpec((tm, w), lambda g: (lag(g), 0))

    def prev_idx(g):
        t = cur(g)
        return jnp.maximum(t * sub - 1, (t // nt) * halo_per_seq)

    def next_idx(g):
        t = cur(g)
        return jnp.minimum((t + 1) * sub, (t // nt + 1) * halo_per_seq - 1)

    prev = pl.BlockSpec((ATTN_BLOCK, ATTN_KV), lambda g: (prev_idx(g), 0))
    nxt = pl.BlockSpec((ATTN_BLOCK, ATTN_KV), lambda g: (next_idx(g), 0))
    prev_t = pl.BlockSpec((ATTN_KV, ATTN_BLOCK), lambda g: (0, prev_idx(g)))
    cur_t = pl.BlockSpec((ATTN_KV, tm), lambda g: (0, cur(g)))
    nxt_t = pl.BlockSpec((ATTN_KV, ATTN_BLOCK), lambda g: (0, next_idx(g)))
    return pl.pallas_call(
        functools.partial(_forward_sweep_kernel, tiles_per_seq=nt),
        grid=(n_tiles + 1,),
        in_specs=[pl.BlockSpec(memory_space=pltpu.SMEM),
                  row(ATTN_Q), prev, row(ATTN_KV), nxt, prev_t, cur_t, nxt_t,
                  row(GLA_QK), row(GLA_QK), row(GLA_V), row(GLA_V), row(2 * GLA_RANK), row(GLA_V),
                  _const_spec((2 * GLA_RANK, GLA_QK)), _const_spec((1, GLA_QK)), _const_spec((GLA_BLOCK, GLA_BLOCK)),
                  _const_spec((1, GLA_DV)),
                  lag_row(D_MODEL), _const_spec((D_MODEL, D_MODEL)), _const_spec((1, D_MODEL)),
                  _const_spec((D_MODEL, D_FF)), _const_spec((D_MODEL, D_FF)), _const_spec((D_FF, D_MODEL)),
                  _const_spec((1, D_MODEL))],
        out_specs=lag_row(D_MODEL),
        out_shape=jax.ShapeDtypeStruct((n_rows, D_MODEL), F32),
        scratch_shapes=[pltpu.VMEM((tm, D_FF), BF16), pltpu.VMEM((2, tm, D_MODEL), BF16),
                        pltpu.VMEM((GLA_QK, GLA_DV), F32)],
        compiler_params=pltpu.CompilerParams(dimension_semantics=("arbitrary",), vmem_limit_bytes=VMEM_LIMIT),
        name="forward_sweep",
    )(sink, qa, ka, ka, ka, va, va, va, gq, gk, gv, gg, r, ob, wdec, bdec, tri_lo, gnorm,
      x1, wout, n2, wg, wu, wd, nf)


def _rope_tables(seq_len):
    half = HEAD_DIM // 2
    inv_freq = ROPE_THETA ** (-jnp.arange(half, dtype=F32) / half)
    ang = jnp.arange(seq_len, dtype=F32)[:, None] * inv_freq[None, :]
    cos, sin = jnp.cos(ang), jnp.sin(ang)
    cos_tab = jnp.tile(cos, (1, LANE // half))
    sin_tab = jnp.tile(jnp.concatenate([-sin, sin], axis=1), (1, LANE // HEAD_DIM))
    return cos_tab, sin_tab


def _chunk_tri(bt, upper):
    i = np.arange(bt)[:, None]
    j = np.arange(bt)[None, :]
    same_chunk = (i // GLA_CHUNK) == (j // GLA_CHUNK)
    keep = (j >= i) if upper else (j <= i)
    return jnp.asarray(same_chunk & keep, dtype=BF16)


def _trunk(x, p):
    batch, seq_len, _ = x.shape
    cos_tab, sin_tab = _rope_tables(seq_len)
    x2d = x.reshape(batch * seq_len, D_MODEL)
    x1, qa, ka, va, gq, gk, gv, gg, r, ob = _reverse_sweep(
        x2d, seq_len, p["n1"], p["wg1"], p["wu1"], p["wd1"], p["nm"], p["win"], cos_tab, sin_tab,
        p["wdec_b"], p["bdec_b"], _chunk_tri(GLA_BLOCK, True))
    y = _forward_sweep(seq_len, p["sink"], qa, ka, va, gq, gk, gv, gg, r, ob,
                       p["wdec_f"], p["bdec_f"], _chunk_tri(GLA_BLOCK, False), p["gnorm"],
                       x1, p["wout"], p["n2"], p["wg2"], p["wu2"], p["wd2"], p["nf"])
    return y.reshape(batch, seq_len, D_MODEL)


def kernel(x_prompt, x_sample, norm_ffn1, w_ffn1_gate, w_ffn1_up, w_ffn1_down, norm_mix, w_in, attn_sink, w_gla_decay_fwd, b_gla_decay_fwd, w_gla_decay_bwd, b_gla_decay_bwd, gla_out_norm, w_out, norm_ffn2, w_ffn2_gate, w_ffn2_up, w_ffn2_down, norm_final):
    assert norm_ffn1.shape[0] == 1, "single-layer trunk"
    zeros_rank = jnp.zeros((GLA_RANK, GLA_QK), F32)
    p = dict(
        n1=norm_ffn1[0][None, :], wg1=w_ffn1_gate[0].astype(BF16), wu1=w_ffn1_up[0].astype(BF16),
        wd1=w_ffn1_down[0].astype(BF16),
        nm=norm_mix[0][None, :],
        win=jnp.pad(w_in[0], ((0, 0), (0, IN_PROJ_PAD - IN_PROJ_WIDTH))).astype(BF16),
        sink=attn_sink[0],
        wdec_f=jnp.concatenate([w_gla_decay_fwd[0], zeros_rank], axis=0).astype(BF16),
        bdec_f=b_gla_decay_fwd[0][None, :],
        wdec_b=jnp.concatenate([zeros_rank, w_gla_decay_bwd[0]], axis=0).astype(BF16),
        bdec_b=b_gla_decay_bwd[0][None, :],
        gnorm=gla_out_norm[0][None, :],
        wout=jnp.concatenate([
            w_out[0][:ATTN_Q].reshape(N_KV_HEADS, ATTN_GROUP, HEAD_DIM, D_MODEL).transpose(1, 0, 2, 3)
            .reshape(ATTN_Q, D_MODEL), w_out[0][ATTN_Q:]], axis=0).astype(BF16),
        n2=norm_ffn2[0][None, :], wg2=w_ffn2_gate[0].astype(BF16), wu2=w_ffn2_up[0].astype(BF16),
        wd2=w_ffn2_down[0].astype(BF16),
        nf=norm_final[None, :],
    )
    return _trunk(x_prompt, p), _trunk(x_sample, p)
```

```python
import functools

import jax
import jax.numpy as jnp
import numpy as np
from jax import lax
from jax.experimental import pallas as pl
from jax.experimental.pallas import tpu as pltpu

F32 = jnp.float32
BF16 = jnp.bfloat16

D_MODEL = 1024
D_FF = 2816
EPS = 1e-6
N_ATTN_HEADS = 8
N_KV_HEADS = 2
ATTN_GROUP = N_ATTN_HEADS // N_KV_HEADS
HEAD_DIM = 64
ATTN_BLOCK = 128
ROPE_THETA = 10000.0
N_GLA_HEADS = 4
GLA_DK = 64
GLA_DV = 128
GLA_RANK = 16
GLA_GATE_NORMALIZER = 16.0
GLA_CHUNK = 64
ATTN_Q = N_ATTN_HEADS * HEAD_DIM
ATTN_KV = N_KV_HEADS * HEAD_DIM
GLA_QK = N_GLA_HEADS * GLA_DK
GLA_V = N_GLA_HEADS * GLA_DV
IN_PROJ_WIDTH = ATTN_Q + 2 * ATTN_KV + 2 * GLA_QK + 2 * GLA_V + 2 * GLA_RANK
LANE = 128
MXU_COLS = 256
IN_PROJ_PAD = ((IN_PROJ_WIDTH + LANE - 1) // LANE) * LANE
OFF_AQ = 0
OFF_AK = OFF_AQ + ATTN_Q
OFF_AV = OFF_AK + ATTN_KV
OFF_GQ = OFF_AV + ATTN_KV
OFF_GK = OFF_GQ + GLA_QK
OFF_GV = OFF_GK + GLA_QK
OFF_GG = OFF_GV + GLA_V
OFF_R = OFF_GG + GLA_V

ROW_TILE = 512
ROW_SLAB = 256
FF_CHUNK_FFN1 = 768
FF_CHUNK_FFN2 = 256
IN_PROJ_COLS = MXU_COLS
IN_PROJ_DOTS = ATTN_Q // IN_PROJ_COLS + 3 + 2 * (GLA_V // IN_PROJ_COLS) + 1
GLA_BLOCK = 256
VMEM_LIMIT = 56 * 1024 * 1024


def _rms(x, gain):
    return x * lax.rsqrt(jnp.mean(x * x, axis=-1, keepdims=True) + EPS) * gain


def _silu(x):
    return x * (1.0 / (1.0 + jnp.exp(-x)))


def _advance(side, n):
    for _ in range(n):
        next(side, None)


def _ffn_ticks(ff_chunk):
    return 1 + -(-D_FF // ff_chunk) + D_MODEL // MXU_COLS


def _swiglu_residual(x_slabs, gain_ref, wg_ref, wu_ref, wd_ref, act_ref, ff_chunk, side, side_per_tick):
    bounds = list(range(0, D_FF, ff_chunk)) + [D_FF]
    _advance(side, side_per_tick)
    gain = gain_ref[...]
    hs = [_rms(x, gain).astype(BF16) for x in x_slabs]

    def up_chunk(h, rows, c):
        sl = slice(bounds[c], bounds[c + 1])
        g = jnp.dot(h, wg_ref[:, sl], preferred_element_type=F32)
        u = jnp.dot(h, wu_ref[:, sl], preferred_element_type=F32)
        act_ref[rows, sl] = (_silu(g) * u).astype(BF16)

    row0 = 0
    for h in hs:
        up_chunk(h, slice(row0, row0 + h.shape[0]), 0)
        row0 += h.shape[0]
    _advance(side, side_per_tick)
    h = jnp.concatenate(hs, axis=0)
    for c in range(1, len(bounds) - 1):
        up_chunk(h, slice(0, row0), c)
        _advance(side, side_per_tick)
    x = jnp.concatenate(x_slabs, axis=0)
    out = []
    for j in range(D_MODEL // MXU_COLS):
        cols = slice(j * MXU_COLS, (j + 1) * MXU_COLS)
        y = jnp.dot(act_ref[...], wd_ref[:, cols], preferred_element_type=F32)
        out.append(x[:, cols] + 0.5 * y)
        _advance(side, side_per_tick)
    return jnp.concatenate(out, axis=1)


def _rope_pair(x, cos, sin_signed, first_half):
    swapped = jnp.where(first_half, pltpu.roll(x, LANE - HEAD_DIM // 2, 1), pltpu.roll(x, HEAD_DIM // 2, 1))
    return x * cos + swapped * sin_signed


GLA_CHUNKS_PER_BLOCK = GLA_BLOCK // GLA_CHUNK
GLA_BLOCKS_PER_TILE = ROW_TILE // GLA_BLOCK
GLA_STAGES_PER_TILE = GLA_BLOCKS_PER_TILE * (4 + 2 * GLA_CHUNKS_PER_BLOCK)


def _round_robin(*stage_generators):
    live = list(stage_generators)
    while live:
        for gen in list(live):
            try:
                next(gen)
                yield
            except StopIteration:
                live.remove(gen)


def _gla_block_stages(q_ref, k_ref, v_ref, r_ref, brows, wdec_ref, bdec_ref, tri_ref, s_ref, reverse, emit):
    bt = brows.stop - brows.start
    nch = bt // GLA_CHUNK
    z = jnp.dot(r_ref[brows, :], wdec_ref[...], preferred_element_type=F32) + bdec_ref[...]
    yield
    log_a = (jnp.minimum(z, 0.0) - jnp.log1p(jnp.exp(-jnp.abs(z)))) * (1.0 / GLA_GATE_NORMALIZER)
    hi = log_a.astype(BF16)
    lo = (log_a - hi.astype(F32)).astype(BF16)
    tri = tri_ref[...]
    cum = jnp.dot(tri, hi, preferred_element_type=F32) + jnp.dot(tri, lo, preferred_element_type=F32)
    yield
    q = q_ref[brows, :]
    k = k_ref[brows, :]
    cum3 = cum.reshape(nch, GLA_CHUNK, GLA_QK)
    edge = GLA_CHUNK - 1 if not reverse else 0
    tot3 = cum3[:, edge:edge + 1, :]
    rest = (tot3 - cum3).reshape(bt, GLA_QK)
    qe = (q * (GLA_DK ** -0.5) * jnp.exp(cum)).astype(BF16)
    ke = (k * jnp.exp(-cum)).astype(BF16)
    ks_t = (k * jnp.exp(rest)).T.astype(BF16)
    tot = tot3.reshape(nch, GLA_QK)
    tot_t = jnp.concatenate([tot, jnp.zeros((LANE - nch, GLA_QK), F32)], axis=0).T
    decay_t = jnp.exp(tot_t)

    ii = lax.broadcasted_iota(jnp.int32, (GLA_CHUNK, GLA_CHUNK), 0)
    jj = lax.broadcasted_iota(jnp.int32, (GLA_CHUNK, GLA_CHUNK), 1)
    keep = (jj > ii) if reverse else (jj <= ii)
    rows = [slice(n * GLA_CHUNK, (n + 1) * GLA_CHUNK) for n in range(nch)]
    klanes = [slice(h * GLA_DK, (h + 1) * GLA_DK) for h in range(N_GLA_HEADS)]

    def v_of(n, h):
        return v_ref[brows.start + n * GLA_CHUNK:brows.start + (n + 1) * GLA_CHUNK, h * GLA_DV:(h + 1) * GLA_DV]

    yield

    a = [[None] * N_GLA_HEADS for _ in range(nch)]
    u = [None] * nch
    for n in range(nch):
        for h in range(N_GLA_HEADS):
            s_nh = lax.dot_general(qe[rows[n], klanes[h]], ke[rows[n], klanes[h]], (((1,), (1,)), ((), ())),
                                   preferred_element_type=F32)
            a[n][h] = jnp.where(keep, s_nh, 0.0).astype(BF16)
        u[n] = jnp.concatenate([jnp.dot(ks_t[klanes[h], rows[n]], v_of(n, h), preferred_element_type=F32)
                                for h in range(N_GLA_HEADS)], axis=0)
        yield
    s = s_ref[...]
    s_in = [None] * nch
    for n in (range(nch - 1, -1, -1) if reverse else range(nch)):
        s_in[n] = s.astype(BF16)
        s = decay_t[:, n:n + 1] * s + u[n]
    s_ref[...] = s
    yield
    for n in range(nch):
        for h in range(N_GLA_HEADS):
            emit(n, h, jnp.dot(a[n][h], v_of(n, h), preferred_element_type=F32)
                 + jnp.dot(qe[rows[n], klanes[h]], s_in[n][klanes[h], :], preferred_element_type=F32))
        yield


def _gla_tile_streams(q_ref, k_ref, v_ref, r_ref, wdec_ref, bdec_ref, tri_ref, s_ref, reverse, emit):
    streams = []
    for blk in (range(GLA_BLOCKS_PER_TILE - 1, -1, -1) if reverse else range(GLA_BLOCKS_PER_TILE)):
        brows = slice(blk * GLA_BLOCK, (blk + 1) * GLA_BLOCK)
        emit_block = lambda n, h, o, base=blk * GLA_BLOCK: emit(base + n * GLA_CHUNK, h, o)
        streams.append(_gla_block_stages(q_ref, k_ref, v_ref, r_ref, brows, wdec_ref, bdec_ref, tri_ref, s_ref,
                                         reverse, emit_block))
    return streams


def _reverse_sweep_kernel(x_ref, n1_ref, wg_ref, wu_ref, wd_ref, nm_ref, win_ref, cos_ref, sin_ref,
                          wdec_ref, bdec_ref, tri_ref,
                          x1_ref, qa_ref, ka_ref, va_ref, gq_ref, gk_ref, gv_ref, gg_ref, r_ref, ob_ref,
                          act_ref, pq_ref, pk_ref, pv_ref, pr_ref, s_ref, *, tiles_per_seq):
    g = pl.program_id(0)
    cur_slot = g % 2
    prev_slot = 1 - cur_slot

    @pl.when(g == 0)
    def _():
        pq_ref[...] = jnp.zeros_like(pq_ref)
        pk_ref[...] = jnp.zeros_like(pk_ref)
        pv_ref[...] = jnp.zeros_like(pv_ref)
        pr_ref[...] = jnp.zeros_like(pr_ref)

    @pl.when(jnp.maximum(g - 1, 0) % tiles_per_seq == 0)
    def _():
        s_ref[...] = jnp.zeros_like(s_ref)

    def emit_ob(row0, h, o):
        ob_ref[row0:row0 + GLA_CHUNK, h * GLA_DV:(h + 1) * GLA_DV] = o

    side = _round_robin(*_gla_tile_streams(pq_ref.at[prev_slot], pk_ref.at[prev_slot], pv_ref.at[prev_slot],
                                           pr_ref.at[prev_slot], wdec_ref, bdec_ref, tri_ref, s_ref, True, emit_ob))

    x_slabs = [x_ref[r0:r0 + ROW_SLAB, :] for r0 in range(0, ROW_TILE, ROW_SLAB)]
    side_per_tick = -(-(GLA_STAGES_PER_TILE - IN_PROJ_DOTS) // _ffn_ticks(FF_CHUNK_FFN1))
    x1 = _swiglu_residual(x_slabs, n1_ref, wg_ref, wu_ref, wd_ref, act_ref, FF_CHUNK_FFN1, side, side_per_tick)
    x1_ref[...] = x1
    h = jnp.concatenate([_rms(x1[r0:r0 + ROW_SLAB, :], nm_ref[...]).astype(BF16)
                         for r0 in range(0, ROW_TILE, ROW_SLAB)], axis=0)
    cos = cos_ref[...]
    sin = sin_ref[...]
    lane = lax.broadcasted_iota(jnp.int32, (1, LANE), 1)
    first_half = (lane % HEAD_DIM) < (HEAD_DIM // 2)

    def proj(off, width):
        _advance(side, 1)
        return jnp.dot(h, win_ref[:, off:off + width], preferred_element_type=F32)

    scale = HEAD_DIM ** -0.5
    for j in range(ATTN_Q // IN_PROJ_COLS):
        q2 = proj(OFF_AQ + j * IN_PROJ_COLS, IN_PROJ_COLS)
        for i in range(IN_PROJ_COLS // LANE):
            q = _rope_pair(q2[:, i * LANE:(i + 1) * LANE], cos, sin, first_half)
            qa_ref[:, j * IN_PROJ_COLS + i * LANE:j * IN_PROJ_COLS + (i + 1) * LANE] = (q * scale).astype(BF16)
    kv = proj(OFF_AK, 2 * ATTN_KV)
    ka_ref[...] = _rope_pair(kv[:, :ATTN_KV], cos, sin, first_half).astype(BF16)
    va_ref[...] = kv[:, ATTN_KV:].T.astype(BF16)
    gq = proj(OFF_GQ, GLA_QK)
    gq_ref[...] = gq
    pq_ref[cur_slot] = gq
    gk = proj(OFF_GK, GLA_QK)
    gk_ref[...] = gk
    pk_ref[cur_slot] = gk
    for j in range(GLA_V // IN_PROJ_COLS):
        cols = slice(j * IN_PROJ_COLS, (j + 1) * IN_PROJ_COLS)
        gv = proj(OFF_GV + j * IN_PROJ_COLS, IN_PROJ_COLS).astype(BF16)
        gv_ref[:, cols] = gv
        pv_ref[cur_slot, :, cols] = gv
        gg_ref[:, cols] = proj(OFF_GG + j * IN_PROJ_COLS, IN_PROJ_COLS)
    rr = proj(OFF_R, LANE)[:, :2 * GLA_RANK].astype(BF16)
    r_ref[...] = rr
    pr_ref[cur_slot] = rr
    for _ in side:
        pass


def _const_spec(shape):
    return pl.BlockSpec(shape, lambda *_: (0,) * len(shape), pipeline_mode=pl.Buffered(1))


def _reverse_sweep(x2d, seq_len, n1, wg, wu, wd, nm, win, cos_tab, sin_tab, wdec, bdec, tri_up):
    n_rows = x2d.shape[0]
    tm = ROW_TILE
    assert n_rows % tm == 0 and seq_len % tm == 0
    nt = seq_len // tm
    n_tiles = n_rows // tm

    def tile_of(step):
        return (step // nt) * nt + (nt - 1 - step % nt)

    cur = lambda g: tile_of(jnp.minimum(g, n_tiles - 1))
    lag = lambda g: tile_of(jnp.maximum(g - 1, 0))
    row = lambda w: pl.BlockSpec((tm, w), lambda g: (cur(g), 0))
    rope = pl.BlockSpec((tm, LANE), lambda g: (nt - 1 - jnp.minimum(g, n_tiles - 1) % nt, 0))
    out_shapes = (
        jax.ShapeDtypeStruct((n_rows, D_MODEL), F32),
        jax.ShapeDtypeStruct((n_rows, ATTN_Q), BF16),
        jax.ShapeDtypeStruct((n_rows, ATTN_KV), BF16),
        jax.ShapeDtypeStruct((ATTN_KV, n_rows), BF16),
        jax.ShapeDtypeStruct((n_rows, GLA_QK), F32),
        jax.ShapeDtypeStruct((n_rows, GLA_QK), F32),
        jax.ShapeDtypeStruct((n_rows, GLA_V), BF16),
        jax.ShapeDtypeStruct((n_rows, GLA_V), F32),
        jax.ShapeDtypeStruct((n_rows, 2 * GLA_RANK), BF16),
        jax.ShapeDtypeStruct((n_rows, GLA_V), F32),
    )
    return pl.pallas_call(
        functools.partial(_reverse_sweep_kernel, tiles_per_seq=nt),
        grid=(n_tiles + 1,),
        in_specs=[
            row(D_MODEL),
            _const_spec((1, D_MODEL)),
            _const_spec((D_MODEL, D_FF)), _const_spec((D_MODEL, D_FF)), _const_spec((D_FF, D_MODEL)),
            _const_spec((1, D_MODEL)),
            _const_spec((D_MODEL, IN_PROJ_PAD)),
            rope, rope,
            _const_spec((2 * GLA_RANK, GLA_QK)), _const_spec((1, GLA_QK)), _const_spec((GLA_BLOCK, GLA_BLOCK)),
        ],
        out_specs=[row(D_MODEL), row(ATTN_Q), row(ATTN_KV), pl.BlockSpec((ATTN_KV, tm), lambda g: (0, cur(g))),
                   row(GLA_QK), row(GLA_QK),
                   row(GLA_V), row(GLA_V), row(2 * GLA_RANK),
                   pl.BlockSpec((tm, GLA_V), lambda g: (lag(g), 0))],
        out_shape=out_shapes,
        scratch_shapes=[pltpu.VMEM((tm, D_FF), BF16),
                        pltpu.VMEM((2, tm, GLA_QK), F32), pltpu.VMEM((2, tm, GLA_QK), F32),
                        pltpu.VMEM((2, tm, GLA_V), BF16), pltpu.VMEM((2, tm, 2 * GLA_RANK), BF16),
                        pltpu.VMEM((GLA_QK, GLA_DV), F32)],
        compiler_params=pltpu.CompilerParams(dimension_semantics=("arbitrary",), vmem_limit_bytes=VMEM_LIMIT),
        name="reverse_sweep",
    )(x2d, n1, wg, wu, wd, nm, win, cos_tab, sin_tab, wdec, bdec, tri_up)


ATTN_UNITS_PER_TILE = (ROW_TILE // ATTN_BLOCK) * N_KV_HEADS
ATTN_STAGES_PER_TILE = 3 * ATTN_UNITS_PER_TILE


def _attention_tile_stages(sink_ref, qa_ref, kp_ref, kc_ref, kn_ref, vp_ref, vc_ref, vn_ref, mix_ref, tpos,
                           tiles_per_seq):
    sub = ROW_TILE // ATTN_BLOCK
    n_qblocks = tiles_per_seq * sub
    n_keys = 3 * ATTN_BLOCK
    n_cols = ATTN_GROUP * ATTN_BLOCK
    kbuf = jnp.concatenate([kp_ref[...], kc_ref[...], kn_ref[...]], axis=0)
    vbuf_t = jnp.concatenate([vp_ref[...], vc_ref[...], vn_ref[...]], axis=1)
    kj = lax.broadcasted_iota(jnp.int32, (n_keys, n_cols), 0)
    col = lax.broadcasted_iota(jnp.int32, (n_keys, n_cols), 1)
    qi = col % ATTN_BLOCK
    in_window = (kj >= qi) & (kj <= qi + 2 * ATTN_BLOCK)
    head_of_col = lax.broadcasted_iota(jnp.int32, (1, n_cols), 1) // ATTN_BLOCK
    pending = {}

    def unit(jb, kv):
        qblk = tpos * sub + jb
        qrows = slice(jb * ATTN_BLOCK, (jb + 1) * ATTN_BLOCK)
        krows = slice(jb * ATTN_BLOCK, (jb + 3) * ATTN_BLOCK)
        kvl = slice(kv * HEAD_DIM, (kv + 1) * HEAD_DIM)
        heads = range(kv * ATTN_GROUP, (kv + 1) * ATTN_GROUP)
        qs = jnp.concatenate([qa_ref[qrows, h * HEAD_DIM:(h + 1) * HEAD_DIM] for h in heads], axis=0)
        s_t = lax.dot_general(kbuf[krows, kvl], qs, (((1,), (1,)), ((), ())), preferred_element_type=F32)
        yield
        mask = in_window & ((kj >= ATTN_BLOCK) | (qblk > 0)) & ((kj < 2 * ATTN_BLOCK) | (qblk < n_qblocks - 1))
        s_t = jnp.where(mask, s_t, -1e30)
        sink = jnp.full((1, n_cols), sink_ref[heads[-1]], F32)
        for hl in range(ATTN_GROUP - 2, -1, -1):
            sink = jnp.where(head_of_col == hl, sink_ref[heads[hl]], sink)
        m = jnp.maximum(jnp.max(s_t, axis=0, keepdims=True), sink)
        p = jnp.exp(s_t - m)
        denom = jnp.sum(p, axis=0, keepdims=True) + jnp.exp(sink - m)
        p_t = p.astype(BF16)
        yield
        pending[kv] = jnp.dot(vbuf_t[kvl, krows], p_t, preferred_element_type=F32) / denom
        if kv == N_KV_HEADS - 1:
            o_t = jnp.concatenate([pending.pop(i) for i in range(N_KV_HEADS)], axis=0)
            for hl in range(ATTN_GROUP):
                cols = slice(hl * ATTN_BLOCK, (hl + 1) * ATTN_BLOCK)
                mix_ref[qrows, hl * ATTN_KV:(hl + 1) * ATTN_KV] = o_t[:, cols].T.astype(BF16)
        yield

    units = [unit(jb, kv) for jb in range(sub) for kv in range(N_KV_HEADS)]
    for slot in range(len(units) + 4):
        for stage in range(3):
            u = slot - 2 * stage
            if 0 <= u < len(units):
                next(units[u])
                yield


def _forward_sweep_kernel(sink_ref, qa_ref, kp_ref, kc_ref, kn_ref, vp_ref, vc_ref, vn_ref,
                          gq_ref, gk_ref, gv_ref, gg_ref, r_ref, ob_ref, wdec_ref, bdec_ref, tri_ref, gnorm_ref,
                          x1_ref, wout_ref, n2_ref, wg_ref, wu_ref, wd_ref, nf_ref,
                          y_ref, act_ref, mix_ref, s_ref, *, tiles_per_seq):
    g = pl.program_id(0)
    mix_cur = mix_ref.at[g % 2]
    mix_prev = mix_ref.at[1 - g % 2]

    @pl.when(g == 0)
    def _():
        mix_ref[...] = jnp.zeros_like(mix_ref)

    @pl.when(g % tiles_per_seq == 0)
    def _():
        s_ref[...] = jnp.zeros_like(s_ref)

    tpos = jnp.minimum(g, pl.num_programs(0) - 2) % tiles_per_seq
    gain = gnorm_ref[...]

    def emit_mix(row0, h, o):
        rows = slice(row0, row0 + GLA_CHUNK)
        cols = slice(h * GLA_DV, (h + 1) * GLA_DV)
        o = _rms(o + ob_ref[rows, cols], gain) * _silu(gg_ref[rows, cols])
        mix_cur[rows, ATTN_Q + h * GLA_DV:ATTN_Q + (h + 1) * GLA_DV] = o.astype(BF16)

    side = _round_robin(
        _attention_tile_stages(sink_ref, qa_ref, kp_ref, kc_ref, kn_ref, vp_ref, vc_ref, vn_ref, mix_cur, tpos,
                               tiles_per_seq),
        *_gla_tile_streams(gq_ref, gk_ref, gv_ref, r_ref, wdec_ref, bdec_ref, tri_ref, s_ref, False, emit_mix))
    side_per_tick = -(-(ATTN_STAGES_PER_TILE + GLA_STAGES_PER_TILE) // _ffn_ticks(FF_CHUNK_FFN2))

    x2_slabs = [x1_ref[r0:r0 + ROW_SLAB, :]
                + jnp.dot(mix_prev[r0:r0 + ROW_SLAB, :], wout_ref[...], preferred_element_type=F32)
                for r0 in range(0, ROW_TILE, ROW_SLAB)]
    x3 = _swiglu_residual(x2_slabs, n2_ref, wg_ref, wu_ref, wd_ref, act_ref, FF_CHUNK_FFN2, side, side_per_tick)
    for r0 in range(0, ROW_TILE, ROW_SLAB):
        y_ref[r0:r0 + ROW_SLAB, :] = _rms(x3[r0:r0 + ROW_SLAB, :], nf_ref[...])
    for _ in side:
        pass


def _forward_sweep(seq_len, sink, qa, ka, va, gq, gk, gv, gg, r, ob, wdec, bdec, tri_lo, gnorm,
                   x1, wout, n2, wg, wu, wd, nf):
    n_rows = x1.shape[0]
    tm = ROW_TILE
    nt = seq_len // tm
    n_tiles = n_rows // tm
    sub = tm // ATTN_BLOCK
    halo_per_seq = seq_len // ATTN_BLOCK
    cur = lambda g: jnp.minimum(g, n_tiles - 1)
    lag = lambda g: jnp.maximum(g - 1, 0)
    row = lambda w: pl.BlockSpec((tm, w), lambda g: (cur(g), 0))
    lag_row = lambda w: pl.BlockSpec((tm, w), lambda g: (lag(g), 0))

    def prev_idx(g):
        t = cur(g)
        return jnp.maximum(t * sub - 1, (t // nt) * halo_per_seq)

    def next_idx(g):
        t = cur(g)
        return jnp.minimum((t + 1) * sub, (t // nt + 1) * halo_per_seq - 1)

    prev = pl.BlockSpec((ATTN_BLOCK, ATTN_KV), lambda g: (prev_idx(g), 0))
    nxt = pl.BlockSpec((ATTN_BLOCK, ATTN_KV), lambda g: (next_idx(g), 0))
    prev_t = pl.BlockSpec((ATTN_KV, ATTN_BLOCK), lambda g: (0, prev_idx(g)))
    cur_t = pl.BlockSpec((ATTN_KV, tm), lambda g: (0, cur(g)))
    nxt_t = pl.BlockSpec((ATTN_KV, ATTN_BLOCK), lambda g: (0, next_idx(g)))
    return pl.pallas_call(
        functools.partial(_forward_sweep_kernel, tiles_per_seq=nt),
        grid=(n_tiles + 1,),
        in_specs=[pl.BlockSpec(memory_space=pltpu.SMEM),
                  row(ATTN_Q), prev, row(ATTN_KV), nxt, prev_t, cur_t, nxt_t,
                  row(GLA_QK), row(GLA_QK), row(GLA_V), row(GLA_V), row(2 * GLA_RANK), row(GLA_V),
                  _const_spec((2 * GLA_RANK, GLA_QK)), _const_spec((1, GLA_QK)), _const_spec((GLA_BLOCK, GLA_BLOCK)),
                  _const_spec((1, GLA_DV)),
                  lag_row(D_MODEL), _const_spec((D_MODEL, D_MODEL)), _const_spec((1, D_MODEL)),
                  _const_spec((D_MODEL, D_FF)), _const_spec((D_MODEL, D_FF)), _const_spec((D_FF, D_MODEL)),
                  _const_spec((1, D_MODEL))],
        out_specs=lag_row(D_MODEL),
        out_shape=jax.ShapeDtypeStruct((n_rows, D_MODEL), F32),
        scratch_shapes=[pltpu.VMEM((tm, D_FF), BF16), pltpu.VMEM((2, tm, D_MODEL), BF16),
                        pltpu.VMEM((GLA_QK, GLA_DV), F32)],
        compiler_params=pltpu.CompilerParams(dimension_semantics=("arbitrary",), vmem_limit_bytes=VMEM_LIMIT),
        name="forward_sweep",
    )(sink, qa, ka, ka, ka, va, va, va, gq, gk, gv, gg, r, ob, wdec, bdec, tri_lo, gnorm,
      x1, wout, n2, wg, wu, wd, nf)


def _rope_tables(seq_len):
    half = HEAD_DIM // 2
    inv_freq = ROPE_THETA ** (-jnp.arange(half, dtype=F32) / half)
    ang = jnp.arange(seq_len, dtype=F32)[:, None] * inv_freq[None, :]
    cos, sin = jnp.cos(ang), jnp.sin(ang)
    cos_tab = jnp.tile(cos, (1, LANE // half))
    sin_tab = jnp.tile(jnp.concatenate([-sin, sin], axis=1), (1, LANE // HEAD_DIM))
    return cos_tab, sin_tab


def _chunk_tri(bt, upper):
    i = np.arange(bt)[:, None]
    j = np.arange(bt)[None, :]
    same_chunk = (i // GLA_CHUNK) == (j // GLA_CHUNK)
    keep = (j >= i) if upper else (j <= i)
    return jnp.asarray(same_chunk & keep, dtype=BF16)


def _trunk(x, p):
    batch, seq_len, _ = x.shape
    cos_tab, sin_tab = _rope_tables(seq_len)
    x2d = x.reshape(batch * seq_len, D_MODEL)
    x1, qa, ka, va, gq, gk, gv, gg, r, ob = _reverse_sweep(
        x2d, seq_len, p["n1"], p["wg1"], p["wu1"], p["wd1"], p["nm"], p["win"], cos_tab, sin_tab,
        p["wdec_b"], p["bdec_b"], _chunk_tri(GLA_BLOCK, True))
    y = _forward_sweep(seq_len, p["sink"], qa, ka, va, gq, gk, gv, gg, r, ob,
                       p["wdec_f"], p["bdec_f"], _chunk_tri(GLA_BLOCK, False), p["gnorm"],
                       x1, p["wout"], p["n2"], p["wg2"], p["wu2"], p["wd2"], p["nf"])
    return y.reshape(batch, seq_len, D_MODEL)


def kernel(x_prompt, x_sample, norm_ffn1, w_ffn1_gate, w_ffn1_up, w_ffn1_down, norm_mix, w_in, attn_sink, w_gla_decay_fwd, b_gla_decay_fwd, w_gla_decay_bwd, b_gla_decay_bwd, gla_out_norm, w_out, norm_ffn2, w_ffn2_gate, w_ffn2_up, w_ffn2_down, norm_final):
    assert norm_ffn1.shape[0] == 1, "single-layer trunk"
    zeros_rank = jnp.zeros((GLA_RANK, GLA_QK), F32)
    p = dict(
        n1=norm_ffn1[0][None, :], wg1=w_ffn1_gate[0].astype(BF16), wu1=w_ffn1_up[0].astype(BF16),
        wd1=w_ffn1_down[0].astype(BF16),
        nm=norm_mix[0][None, :],
        win=jnp.pad(w_in[0], ((0, 0), (0, IN_PROJ_PAD - IN_PROJ_WIDTH))).astype(BF16),
        sink=attn_sink[0],
        wdec_f=jnp.concatenate([w_gla_decay_fwd[0], zeros_rank], axis=0).astype(BF16),
        bdec_f=b_gla_decay_fwd[0][None, :],
        wdec_b=jnp.concatenate([zeros_rank, w_gla_decay_bwd[0]], axis=0).astype(BF16),
        bdec_b=b_gla_decay_bwd[0][None, :],
        gnorm=gla_out_norm[0][None, :],
        wout=jnp.concatenate([
            w_out[0][:ATTN_Q].reshape(N_KV_HEADS, ATTN_GROUP, HEAD_DIM, D_MODEL).transpose(1, 0, 2, 3)
            .reshape(ATTN_Q, D_MODEL), w_out[0][ATTN_Q:]], axis=0).astype(BF16),
        n2=norm_ffn2[0][None, :], wg2=w_ffn2_gate[0].astype(BF16), wu2=w_ffn2_up[0].astype(BF16),
        wd2=w_ffn2_down[0].astype(BF16),
        nf=norm_final[None, :],
    )
    return _trunk(x_prompt, p), _trunk(x_sample, p)
```

```python
import functools

import jax
import jax.numpy as jnp
import numpy as np
from jax import lax
from jax.experimental import pallas as pl
from jax.experimental.pallas import tpu as pltpu

F32 = jnp.float32
BF16 = jnp.bfloat16

D_MODEL = 1024
D_FF = 2816
EPS = 1e-6
N_ATTN_HEADS = 8
N_KV_HEADS = 2
ATTN_GROUP = N_ATTN_HEADS // N_KV_HEADS
HEAD_DIM = 64
ATTN_BLOCK = 128
ROPE_THETA = 10000.0
N_GLA_HEADS = 4
GLA_DK = 64
GLA_DV = 128
GLA_RANK = 16
GLA_GATE_NORMALIZER = 16.0
GLA_CHUNK = 64
ATTN_Q = N_ATTN_HEADS * HEAD_DIM
ATTN_KV = N_KV_HEADS * HEAD_DIM
GLA_QK = N_GLA_HEADS * GLA_DK
GLA_V = N_GLA_HEADS * GLA_DV
IN_PROJ_WIDTH = ATTN_Q + 2 * ATTN_KV + 2 * GLA_QK + 2 * GLA_V + 2 * GLA_RANK
LANE = 128
MXU_COLS = 256
IN_PROJ_PAD = ((IN_PROJ_WIDTH + LANE - 1) // LANE) * LANE
OFF_AQ = 0
OFF_AK = OFF_AQ + ATTN_Q
OFF_AV = OFF_AK + ATTN_KV
OFF_GQ = OFF_AV + ATTN_KV
OFF_GK = OFF_GQ + GLA_QK
OFF_GV = OFF_GK + GLA_QK
OFF_GG = OFF_GV + GLA_V
OFF_R = OFF_GG + GLA_V

ROW_TILE = 512
ROW_SLAB = 256
FF_CHUNK_FFN1 = 768
FF_CHUNK_FFN2 = 256
IN_PROJ_COLS = MXU_COLS
IN_PROJ_DOTS = ATTN_Q // IN_PROJ_COLS + 3 + 2 * (GLA_V // IN_PROJ_COLS) + 1
GLA_BLOCK = 256
VMEM_LIMIT = 56 * 1024 * 1024


def _rms(x, gain):
    return x * lax.rsqrt(jnp.mean(x * x, axis=-1, keepdims=True) + EPS) * gain


def _silu(x):
    return x * (1.0 / (1.0 + jnp.exp(-x)))


def _advance(side, n):
    for _ in range(n):
        next(side, None)


def _ffn_ticks(ff_chunk):
    return 1 + -(-D_FF // ff_chunk) + D_MODEL // MXU_COLS


def _swiglu_residual(x_slabs, gain_ref, wg_ref, wu_ref, wd_ref, act_ref, ff_chunk, side, side_per_tick):
    bounds = list(range(0, D_FF, ff_chunk)) + [D_FF]
    _advance(side, side_per_tick)
    gain = gain_ref[...]
    hs = [_rms(x, gain).astype(BF16) for x in x_slabs]

    def up_chunk(h, rows, c):
        sl = slice(bounds[c], bounds[c + 1])
        g = jnp.dot(h, wg_ref[:, sl], preferred_element_type=F32)
        u = jnp.dot(h, wu_ref[:, sl], preferred_element_type=F32)
        act_ref[rows, sl] = (_silu(g) * u).astype(BF16)

    row0 = 0
    for h in hs:
        up_chunk(h, slice(row0, row0 + h.shape[0]), 0)
        row0 += h.shape[0]
    _advance(side, side_per_tick)
    h = jnp.concatenate(hs, axis=0)
    for c in range(1, len(bounds) - 1):
        up_chunk(h, slice(0, row0), c)
        _advance(side, side_per_tick)
    x = jnp.concatenate(x_slabs, axis=0)
    out = []
    for j in range(D_MODEL // MXU_COLS):
        cols = slice(j * MXU_COLS, (j + 1) * MXU_COLS)
        y = jnp.dot(act_ref[...], wd_ref[:, cols], preferred_element_type=F32)
        out.append(x[:, cols] + 0.5 * y)
        _advance(side, side_per_tick)
    return jnp.concatenate(out, axis=1)


def _rope_pair(x, cos, sin_signed, first_half):
    swapped = jnp.where(first_half, pltpu.roll(x, LANE - HEAD_DIM // 2, 1), pltpu.roll(x, HEAD_DIM // 2, 1))
    return x * cos + swapped * sin_signed


GLA_CHUNKS_PER_BLOCK = GLA_BLOCK // GLA_CHUNK
GLA_BLOCKS_PER_TILE = ROW_TILE // GLA_BLOCK
GLA_STAGES_PER_TILE = GLA_BLOCKS_PER_TILE * (4 + 2 * GLA_CHUNKS_PER_BLOCK)


def _round_robin(*stage_generators):
    live = list(stage_generators)
    while live:
        for gen in list(live):
            try:
                next(gen)
                yield
            except StopIteration:
                live.remove(gen)


def _gla_block_stages(q_ref, k_ref, v_ref, r_ref, brows, wdec_ref, bdec_ref, tri_ref, s_ref, reverse, emit):
    bt = brows.stop - brows.start
    nch = bt // GLA_CHUNK
    z = jnp.dot(r_ref[brows, :], wdec_ref[...], preferred_element_type=F32) + bdec_ref[...]
    yield
    log_a = (jnp.minimum(z, 0.0) - jnp.log1p(jnp.exp(-jnp.abs(z)))) * (1.0 / GLA_GATE_NORMALIZER)
    hi = log_a.astype(BF16)
    lo = (log_a - hi.astype(F32)).astype(BF16)
    tri = tri_ref[...]
    cum = jnp.dot(tri, hi, preferred_element_type=F32) + jnp.dot(tri, lo, preferred_element_type=F32)
    yield
    q = q_ref[brows, :]
    k = k_ref[brows, :]
    cum3 = cum.reshape(nch, GLA_CHUNK, GLA_QK)
    edge = GLA_CHUNK - 1 if not reverse else 0
    tot3 = cum3[:, edge:edge + 1, :]
    rest = (tot3 - cum3).reshape(bt, GLA_QK)
    qe = (q * (GLA_DK ** -0.5) * jnp.exp(cum)).astype(BF16)
    ke = (k * jnp.exp(-cum)).astype(BF16)
    ks_t = (k * jnp.exp(rest)).T.astype(BF16)
    tot = tot3.reshape(nch, GLA_QK)
    tot_t = jnp.concatenate([tot, jnp.zeros((LANE - nch, GLA_QK), F32)], axis=0).T
    decay_t = jnp.exp(tot_t)

    pair_k = 2 * GLA_DK
    pair_v = 2 * GLA_DV
    n_pairs = N_GLA_HEADS // 2
    ii = lax.broadcasted_iota(jnp.int32, (GLA_CHUNK, pair_k), 0)
    jj = lax.broadcasted_iota(jnp.int32, (GLA_CHUNK, pair_k), 1) % GLA_CHUNK
    keep = (jj > ii) if reverse else (jj <= ii)
    first_k = lax.broadcasted_iota(jnp.int32, (1, pair_k), 1) < GLA_DK
    first_v = lax.broadcasted_iota(jnp.int32, (1, pair_v), 1) < GLA_DV
    first_row = lax.broadcasted_iota(jnp.int32, (pair_k, 1), 0) < GLA_DK
    rows = [slice(n * GLA_CHUNK, (n + 1) * GLA_CHUNK) for n in range(nch)]
    klanes = [slice(p * pair_k, (p + 1) * pair_k) for p in range(n_pairs)]

    def diag_rows(x, first):
        zero = jnp.zeros_like(x)
        return jnp.concatenate([jnp.where(first, x, zero), jnp.where(first, zero, x)], axis=0)

    def v_of(n, p):
        return v_ref[brows.start + n * GLA_CHUNK:brows.start + (n + 1) * GLA_CHUNK, p * pair_v:(p + 1) * pair_v]

    yield

    a = [[None] * n_pairs for _ in range(nch)]
    u = [None] * nch
    for n in range(nch):
        u_heads = []
        for p in range(n_pairs):
            s_np = lax.dot_general(qe[rows[n], klanes[p]], diag_rows(ke[rows[n], klanes[p]], first_k),
                                   (((1,), (1,)), ((), ())), preferred_element_type=F32)
            a[n][p] = jnp.where(keep, s_np, 0.0).astype(BF16)
            v_np = v_of(n, p)
            for i in range(2):
                h = 2 * p + i
                u_heads.append(jnp.dot(ks_t[h * GLA_DK:(h + 1) * GLA_DK, rows[n]],
                                       v_np[:, i * GLA_DV:(i + 1) * GLA_DV], preferred_element_type=F32))
        u[n] = jnp.concatenate(u_heads, axis=0)
        yield
    s = s_ref[...]
    s_in = [None] * nch
    for n in (range(nch - 1, -1, -1) if reverse else range(nch)):
        s_in[n] = s.astype(BF16)
        s = decay_t[:, n:n + 1] * s + u[n]
    s_ref[...] = s
    yield
    for n in range(nch):
        for p in range(n_pairs):
            s_pair = s_in[n][klanes[p], :]
            zero = jnp.zeros_like(s_pair)
            s_diag = jnp.concatenate([jnp.where(first_row, s_pair, zero), jnp.where(first_row, zero, s_pair)], axis=1)
            lhs = jnp.concatenate([a[n][p], qe[rows[n], klanes[p]]], axis=1)
            rhs = jnp.concatenate([diag_rows(v_of(n, p), first_v), s_diag], axis=0)
            o = jnp.dot(lhs, rhs, preferred_element_type=F32)
            emit(n, 2 * p, o[:, :GLA_DV])
            emit(n, 2 * p + 1, o[:, GLA_DV:])
        yield


def _gla_tile_streams(q_ref, k_ref, v_ref, r_ref, wdec_ref, bdec_ref, tri_ref, s_ref, reverse, emit):
    streams = []
    for blk in (range(GLA_BLOCKS_PER_TILE - 1, -1, -1) if reverse else range(GLA_BLOCKS_PER_TILE)):
        brows = slice(blk * GLA_BLOCK, (blk + 1) * GLA_BLOCK)
        emit_block = lambda n, h, o, base=blk * GLA_BLOCK: emit(base + n * GLA_CHUNK, h, o)
        streams.append(_gla_block_stages(q_ref, k_ref, v_ref, r_ref, brows, wdec_ref, bdec_ref, tri_ref, s_ref,
                                         reverse, emit_block))
    return streams


def _reverse_sweep_kernel(x_ref, n1_ref, wg_ref, wu_ref, wd_ref, nm_ref, win_ref, cos_ref, sin_ref,
                          wdec_ref, bdec_ref, tri_ref,
                          x1_ref, qa_ref, ka_ref, va_ref, gq_ref, gk_ref, gv_ref, gg_ref, r_ref, ob_ref,
                          act_ref, pq_ref, pk_ref, pv_ref, pr_ref, s_ref, *, tiles_per_seq):
    g = pl.program_id(0)
    cur_slot = g % 2
    prev_slot = 1 - cur_slot

    @pl.when(g == 0)
    def _():
        pq_ref[...] = jnp.zeros_like(pq_ref)
        pk_ref[...] = jnp.zeros_like(pk_ref)
        pv_ref[...] = jnp.zeros_like(pv_ref)
        pr_ref[...] = jnp.zeros_like(pr_ref)

    @pl.when(jnp.maximum(g - 1, 0) % tiles_per_seq == 0)
    def _():
        s_ref[...] = jnp.zeros_like(s_ref)

    def emit_ob(row0, h, o):
        ob_ref[row0:row0 + GLA_CHUNK, h * GLA_DV:(h + 1) * GLA_DV] = o

    side = _round_robin(*_gla_tile_streams(pq_ref.at[prev_slot], pk_ref.at[prev_slot], pv_ref.at[prev_slot],
                                           pr_ref.at[prev_slot], wdec_ref, bdec_ref, tri_ref, s_ref, True, emit_ob))

    x_slabs = [x_ref[r0:r0 + ROW_SLAB, :] for r0 in range(0, ROW_TILE, ROW_SLAB)]
    side_per_tick = -(-(GLA_STAGES_PER_TILE - IN_PROJ_DOTS) // _ffn_ticks(FF_CHUNK_FFN1))
    x1 = _swiglu_residual(x_slabs, n1_ref, wg_ref, wu_ref, wd_ref, act_ref, FF_CHUNK_FFN1, side, side_per_tick)
    x1_ref[...] = x1
    h = jnp.concatenate([_rms(x1[r0:r0 + ROW_SLAB, :], nm_ref[...]).astype(BF16)
                         for r0 in range(0, ROW_TILE, ROW_SLAB)], axis=0)
    cos = cos_ref[...]
    sin = sin_ref[...]
    lane = lax.broadcasted_iota(jnp.int32, (1, LANE), 1)
    first_half = (lane % HEAD_DIM) < (HEAD_DIM // 2)

    def proj(off, width):
        _advance(side, 1)
        return jnp.dot(h, win_ref[:, off:off + width], preferred_element_type=F32)

    scale = HEAD_DIM ** -0.5
    for j in range(ATTN_Q // IN_PROJ_COLS):
        q2 = proj(OFF_AQ + j * IN_PROJ_COLS, IN_PROJ_COLS)
        for i in range(IN_PROJ_COLS // LANE):
            q = _rope_pair(q2[:, i * LANE:(i + 1) * LANE], cos, sin, first_half)
            qa_ref[:, j * IN_PROJ_COLS + i * LANE:j * IN_PROJ_COLS + (i + 1) * LANE] = (q * scale).astype(BF16)
    kv = proj(OFF_AK, 2 * ATTN_KV)
    ka_ref[...] = _rope_pair(kv[:, :ATTN_KV], cos, sin, first_half).astype(BF16)
    va_ref[...] = kv[:, ATTN_KV:].T.astype(BF16)
    gq = proj(OFF_GQ, GLA_QK)
    gq_ref[...] = gq
    pq_ref[cur_slot] = gq
    gk = proj(OFF_GK, GLA_QK)
    gk_ref[...] = gk
    pk_ref[cur_slot] = gk
    for j in range(GLA_V // IN_PROJ_COLS):
        cols = slice(j * IN_PROJ_COLS, (j + 1) * IN_PROJ_COLS)
        gv = proj(OFF_GV + j * IN_PROJ_COLS, IN_PROJ_COLS).astype(BF16)
        gv_ref[:, cols] = gv
        pv_ref[cur_slot, :, cols] = gv
        gg_ref[:, cols] = proj(OFF_GG + j * IN_PROJ_COLS, IN_PROJ_COLS)
    rr = proj(OFF_R, LANE)[:, :2 * GLA_RANK].astype(BF16)
    r_ref[...] = rr
    pr_ref[cur_slot] = rr
    for _ in side:
        pass


def _const_spec(shape):
    return pl.BlockSpec(shape, lambda *_: (0,) * len(shape), pipeline_mode=pl.Buffered(1))


def _reverse_sweep(x2d, seq_len, n1, wg, wu, wd, nm, win, cos_tab, sin_tab, wdec, bdec, tri_up):
    n_rows = x2d.shape[0]
    tm = ROW_TILE
    assert n_rows % tm == 0 and seq_len % tm == 0
    nt = seq_len // tm
    n_tiles = n_rows // tm

    def tile_of(step):
        return (step // nt) * nt + (nt - 1 - step % nt)

    cur = lambda g: tile_of(jnp.minimum(g, n_tiles - 1))
    lag = lambda g: tile_of(jnp.maximum(g - 1, 0))
    row = lambda w: pl.BlockSpec((tm, w), lambda g: (cur(g), 0))
    rope = pl.BlockSpec((tm, LANE), lambda g: (nt - 1 - jnp.minimum(g, n_tiles - 1) % nt, 0))
    out_shapes = (
        jax.ShapeDtypeStruct((n_rows, D_MODEL), F32),
        jax.ShapeDtypeStruct((n_rows, ATTN_Q), BF16),
        jax.ShapeDtypeStruct((n_rows, ATTN_KV), BF16),
        jax.ShapeDtypeStruct((ATTN_KV, n_rows), BF16),
        jax.ShapeDtypeStruct((n_rows, GLA_QK), F32),
        jax.ShapeDtypeStruct((n_rows, GLA_QK), F32),
        jax.ShapeDtypeStruct((n_rows, GLA_V), BF16),
        jax.ShapeDtypeStruct((n_rows, GLA_V), F32),
        jax.ShapeDtypeStruct((n_rows, 2 * GLA_RANK), BF16),
        jax.ShapeDtypeStruct((n_rows, GLA_V), F32),
    )
    return pl.pallas_call(
        functools.partial(_reverse_sweep_kernel, tiles_per_seq=nt),
        grid=(n_tiles + 1,),
        in_specs=[
            row(D_MODEL),
            _const_spec((1, D_MODEL)),
            _const_spec((D_MODEL, D_FF)), _const_spec((D_MODEL, D_FF)), _const_spec((D_FF, D_MODEL)),
            _const_spec((1, D_MODEL)),
            _const_spec((D_MODEL, IN_PROJ_PAD)),
            rope, rope,
            _const_spec((2 * GLA_RANK, GLA_QK)), _const_spec((1, GLA_QK)), _const_spec((GLA_BLOCK, GLA_BLOCK)),
        ],
        out_specs=[row(D_MODEL), row(ATTN_Q), row(ATTN_KV), pl.BlockSpec((ATTN_KV, tm), lambda g: (0, cur(g))),
                   row(GLA_QK), row(GLA_QK),
                   row(GLA_V), row(GLA_V), row(2 * GLA_RANK),
                   pl.BlockSpec((tm, GLA_V), lambda g: (lag(g), 0))],
        out_shape=out_shapes,
        scratch_shapes=[pltpu.VMEM((tm, D_FF), BF16),
                        pltpu.VMEM((2, tm, GLA_QK), F32), pltpu.VMEM((2, tm, GLA_QK), F32),
                        pltpu.VMEM((2, tm, GLA_V), BF16), pltpu.VMEM((2, tm, 2 * GLA_RANK), BF16),
                        pltpu.VMEM((GLA_QK, GLA_DV), F32)],
        compiler_params=pltpu.CompilerParams(dimension_semantics=("arbitrary",), vmem_limit_bytes=VMEM_LIMIT),
        name="reverse_sweep",
    )(x2d, n1, wg, wu, wd, nm, win, cos_tab, sin_tab, wdec, bdec, tri_up)


ATTN_UNITS_PER_TILE = (ROW_TILE // ATTN_BLOCK) * N_KV_HEADS
ATTN_STAGES_PER_TILE = 3 * ATTN_UNITS_PER_TILE


def _attention_tile_stages(sink_ref, qa_ref, kp_ref, kc_ref, kn_ref, vp_ref, vc_ref, vn_ref, mix_ref, tpos,
                           tiles_per_seq):
    sub = ROW_TILE // ATTN_BLOCK
    n_qblocks = tiles_per_seq * sub
    n_keys = 3 * ATTN_BLOCK
    n_cols = ATTN_GROUP * ATTN_BLOCK
    kbuf = jnp.concatenate([kp_ref[...], kc_ref[...], kn_ref[...]], axis=0)
    vbuf_t = jnp.concatenate([vp_ref[...], vc_ref[...], vn_ref[...]], axis=1)
    kj = lax.broadcasted_iota(jnp.int32, (n_keys, n_cols), 0)
    col = lax.broadcasted_iota(jnp.int32, (n_keys, n_cols), 1)
    qi = col % ATTN_BLOCK
    in_window = (kj >= qi) & (kj <= qi + 2 * ATTN_BLOCK)
    head_of_col = lax.broadcasted_iota(jnp.int32, (1, n_cols), 1) // ATTN_BLOCK
    pending = {}

    def unit(jb, kv):
        qblk = tpos * sub + jb
        qrows = slice(jb * ATTN_BLOCK, (jb + 1) * ATTN_BLOCK)
        krows = slice(jb * ATTN_BLOCK, (jb + 3) * ATTN_BLOCK)
        kvl = slice(kv * HEAD_DIM, (kv + 1) * HEAD_DIM)
        heads = range(kv * ATTN_GROUP, (kv + 1) * ATTN_GROUP)
        qs = jnp.concatenate([qa_ref[qrows, h * HEAD_DIM:(h + 1) * HEAD_DIM] for h in heads], axis=0)
        s_t = lax.dot_general(kbuf[krows, kvl], qs, (((1,), (1,)), ((), ())), preferred_element_type=F32)
        yield
        mask = in_window & ((kj >= ATTN_BLOCK) | (qblk > 0)) & ((kj < 2 * ATTN_BLOCK) | (qblk < n_qblocks - 1))
        s_t = jnp.where(mask, s_t, -1e30)
        sink = jnp.full((1, n_cols), sink_ref[heads[-1]], F32)
        for hl in range(ATTN_GROUP - 2, -1, -1):
            sink = jnp.where(head_of_col == hl, sink_ref[heads[hl]], sink)
        m = jnp.maximum(jnp.max(s_t, axis=0, keepdims=True), sink)
        p = jnp.exp(s_t - m)
        denom = jnp.sum(p, axis=0, keepdims=True) + jnp.exp(sink - m)
        p_t = p.astype(BF16)
        yield
        pending[kv] = jnp.dot(vbuf_t[kvl, krows], p_t, preferred_element_type=F32) / denom
        if kv == N_KV_HEADS - 1:
            o_t = jnp.concatenate([pending.pop(i) for i in range(N_KV_HEADS)], axis=0)
            for hl in range(ATTN_GROUP):
                cols = slice(hl * ATTN_BLOCK, (hl + 1) * ATTN_BLOCK)
                mix_ref[qrows, hl * ATTN_KV:(hl + 1) * ATTN_KV] = o_t[:, cols].T.astype(BF16)
        yield

    units = [unit(jb, kv) for jb in range(sub) for kv in range(N_KV_HEADS)]
    for slot in range(len(units) + 4):
        for stage in range(3):
            u = slot - 2 * stage
            if 0 <= u < len(units):
                next(units[u])
                yield


def _forward_sweep_kernel(sink_ref, qa_ref, kp_ref, kc_ref, kn_ref, vp_ref, vc_ref, vn_ref,
                          gq_ref, gk_ref, gv_ref, gg_ref, r_ref, ob_ref, wdec_ref, bdec_ref, tri_ref, gnorm_ref,
                          x1_ref, wout_ref, n2_ref, wg_ref, wu_ref, wd_ref, nf_ref,
                          y_ref, act_ref, mix_ref, s_ref, *, tiles_per_seq):
    g = pl.program_id(0)
    mix_cur = mix_ref.at[g % 2]
    mix_prev = mix_ref.at[1 - g % 2]

    @pl.when(g == 0)
    def _():
        mix_ref[...] = jnp.zeros_like(mix_ref)

    @pl.when(g % tiles_per_seq == 0)
    def _():
        s_ref[...] = jnp.zeros_like(s_ref)

    tpos = jnp.minimum(g, pl.num_programs(0) - 2) % tiles_per_seq
    gain = gnorm_ref[...]

    def emit_mix(row0, h, o):
        rows = slice(row0, row0 + GLA_CHUNK)
        cols = slice(h * GLA_DV, (h + 1) * GLA_DV)
        o = _rms(o + ob_ref[rows, cols], gain) * _silu(gg_ref[rows, cols])
        mix_cur[rows, ATTN_Q + h * GLA_DV:ATTN_Q + (h + 1) * GLA_DV] = o.astype(BF16)

    side = _round_robin(
        _attention_tile_stages(sink_ref, qa_ref, kp_ref, kc_ref, kn_ref, vp_ref, vc_ref, vn_ref, mix_cur, tpos,
                               tiles_per_seq),
        *_gla_tile_streams(gq_ref, gk_ref, gv_ref, r_ref, wdec_ref, bdec_ref, tri_ref, s_ref, False, emit_mix))
    side_per_tick = -(-(ATTN_STAGES_PER_TILE + GLA_STAGES_PER_TILE) // _ffn_ticks(FF_CHUNK_FFN2))

    x2_slabs = [x1_ref[r0:r0 + ROW_SLAB, :]
                + jnp.dot(mix_prev[r0:r0 + ROW_SLAB, :], wout_ref[...], preferred_element_type=F32)
                for r0 in range(0, ROW_TILE, ROW_SLAB)]
    x3 = _swiglu_residual(x2_slabs, n2_ref, wg_ref, wu_ref, wd_ref, act_ref, FF_CHUNK_FFN2, side, side_per_tick)
    for r0 in range(0, ROW_TILE, ROW_SLAB):
        y_ref[r0:r0 + ROW_SLAB, :] = _rms(x3[r0:r0 + ROW_SLAB, :], nf_ref[...])
    for _ in side:
        pass


def _forward_sweep(seq_len, sink, qa, ka, va, gq, gk, gv, gg, r, ob, wdec, bdec, tri_lo, gnorm,
                   x1, wout, n2, wg, wu, wd, nf):
    n_rows = x1.shape[0]
    tm = ROW_TILE
    nt = seq_len // tm
    n_tiles = n_rows // tm
    sub = tm // ATTN_BLOCK
    halo_per_seq = seq_len // ATTN_BLOCK
    cur = lambda g: jnp.minimum(g, n_tiles - 1)
    lag = lambda g: jnp.maximum(g - 1, 0)
    row = lambda w: pl.BlockSpec((tm, w), lambda g: (cur(g), 0))
    lag_row = lambda w: pl.BlockSpec((tm, w), lambda g: (lag(g), 0))

    def prev_idx(g):
        t = cur(g)
        return jnp.maximum(t * sub - 1, (t // nt) * halo_per_seq)

    def next_idx(g):
        t = cur(g)
        return jnp.minimum((t + 1) * sub, (t // nt + 1) * halo_per_seq - 1)

    prev = pl.BlockSpec((ATTN_BLOCK, ATTN_KV), lambda g: (prev_idx(g), 0))
    nxt = pl.BlockSpec((ATTN_BLOCK, ATTN_KV), lambda g: (next_idx(g), 0))
    prev_t = pl.BlockSpec((ATTN_KV, ATTN_BLOCK), lambda g: (0, prev_idx(g)))
    cur_t = pl.BlockSpec((ATTN_KV, tm), lambda g: (0, cur(g)))
    nxt_t = pl.BlockSpec((ATTN_KV, ATTN_BLOCK), lambda g: (0, next_idx(g)))
    return pl.pallas_call(
        functools.partial(_forward_sweep_kernel, tiles_per_seq=nt),
        grid=(n_tiles + 1,),
        in_specs=[pl.BlockSpec(memory_space=pltpu.SMEM),
                  row(ATTN_Q), prev, row(ATTN_KV), nxt, prev_t, cur_t, nxt_t,
                  row(GLA_QK), row(GLA_QK), row(GLA_V), row(GLA_V), row(2 * GLA_RANK), row(GLA_V),
                  _const_spec((2 * GLA_RANK, GLA_QK)), _const_spec((1, GLA_QK)), _const_spec((GLA_BLOCK, GLA_BLOCK)),
                  _const_spec((1, GLA_DV)),
                  lag_row(D_MODEL), _const_spec((D_MODEL, D_MODEL)), _const_spec((1, D_MODEL)),
                  _const_spec((D_MODEL, D_FF)), _const_spec((D_MODEL, D_FF)), _const_spec((D_FF, D_MODEL)),
                  _const_spec((1, D_MODEL))],
        out_specs=lag_row(D_MODEL),
        out_shape=jax.ShapeDtypeStruct((n_rows, D_MODEL), F32),
        scratch_shapes=[pltpu.VMEM((tm, D_FF), BF16), pltpu.VMEM((2, tm, D_MODEL), BF16),
                        pltpu.VMEM((GLA_QK, GLA_DV), F32)],
        compiler_params=pltpu.CompilerParams(dimension_semantics=("arbitrary",), vmem_limit_bytes=VMEM_LIMIT),
        name="forward_sweep",
    )(sink, qa, ka, ka, ka, va, va, va, gq, gk, gv, gg, r, ob, wdec, bdec, tri_lo, gnorm,
      x1, wout, n2, wg, wu, wd, nf)


def _rope_tables(seq_len):
    half = HEAD_DIM // 2
    inv_freq = ROPE_THETA ** (-jnp.arange(half, dtype=F32) / half)
    ang = jnp.arange(seq_len, dtype=F32)[:, None] * inv_freq[None, :]
    cos, sin = jnp.cos(ang), jnp.sin(ang)
    cos_tab = jnp.tile(cos, (1, LANE // half))
    sin_tab = jnp.tile(jnp.concatenate([-sin, sin], axis=1), (1, LANE // HEAD_DIM))
    return cos_tab, sin_tab


def _chunk_tri(bt, upper):
    i = np.arange(bt)[:, None]
    j = np.arange(bt)[None, :]
    same_chunk = (i // GLA_CHUNK) == (j // GLA_CHUNK)
    keep = (j >= i) if upper else (j <= i)
    return jnp.asarray(same_chunk & keep, dtype=BF16)


def _trunk(x, p):
    batch, seq_len, _ = x.shape
    cos_tab, sin_tab = _rope_tables(seq_len)
    x2d = x.reshape(batch * seq_len, D_MODEL)
    x1, qa, ka, va, gq, gk, gv, gg, r, ob = _reverse_sweep(
        x2d, seq_len, p["n1"], p["wg1"], p["wu1"], p["wd1"], p["nm"], p["win"], cos_tab, sin_tab,
        p["wdec_b"], p["bdec_b"], _chunk_tri(GLA_BLOCK, True))
    y = _forward_sweep(seq_len, p["sink"], qa, ka, va, gq, gk, gv, gg, r, ob,
                       p["wdec_f"], p["bdec_f"], _chunk_tri(GLA_BLOCK, False), p["gnorm"],
                       x1, p["wout"], p["n2"], p["wg2"], p["wu2"], p["wd2"], p["nf"])
    return y.reshape(batch, seq_len, D_MODEL)


def kernel(x_prompt, x_sample, norm_ffn1, w_ffn1_gate, w_ffn1_up, w_ffn1_down, norm_mix, w_in, attn_sink, w_gla_decay_fwd, b_gla_decay_fwd, w_gla_decay_bwd, b_gla_decay_bwd, gla_out_norm, w_out, norm_ffn2, w_ffn2_gate, w_ffn2_up, w_ffn2_down, norm_final):
    assert norm_ffn1.shape[0] == 1, "single-layer trunk"
    zeros_rank = jnp.zeros((GLA_RANK, GLA_QK), F32)
    p = dict(
        n1=norm_ffn1[0][None, :], wg1=w_ffn1_gate[0].astype(BF16), wu1=w_ffn1_up[0].astype(BF16),
        wd1=w_ffn1_down[0].astype(BF16),
        nm=norm_mix[0][None, :],
        win=jnp.pad(w_in[0], ((0, 0), (0, IN_PROJ_PAD - IN_PROJ_WIDTH))).astype(BF16),
        sink=attn_sink[0],
        wdec_f=jnp.concatenate([w_gla_decay_fwd[0], zeros_rank], axis=0).astype(BF16),
        bdec_f=b_gla_decay_fwd[0][None, :],
        wdec_b=jnp.concatenate([zeros_rank, w_gla_decay_bwd[0]], axis=0).astype(BF16),
        bdec_b=b_gla_decay_bwd[0][None, :],
        gnorm=gla_out_norm[0][None, :],
        wout=jnp.concatenate([
            w_out[0][:ATTN_Q].reshape(N_KV_HEADS, ATTN_GROUP, HEAD_DIM, D_MODEL).transpose(1, 0, 2, 3)
            .reshape(ATTN_Q, D_MODEL), w_out[0][ATTN_Q:]], axis=0).astype(BF16),
        n2=norm_ffn2[0][None, :], wg2=w_ffn2_gate[0].astype(BF16), wu2=w_ffn2_up[0].astype(BF16),
        wd2=w_ffn2_down[0].astype(BF16),
        nf=norm_final[None, :],
    )
    return _trunk(x_prompt, p), _trunk(x_sample, p)
```

```python
import functools

import jax
import jax.numpy as jnp
import numpy as np
from jax import lax
from jax.experimental import pallas as pl
from jax.experimental.pallas import tpu as pltpu

F32 = jnp.float32
BF16 = jnp.bfloat16

D_MODEL = 1024
D_FF = 2816
EPS = 1e-6
N_ATTN_HEADS = 8
N_KV_HEADS = 2
ATTN_GROUP = N_ATTN_HEADS // N_KV_HEADS
HEAD_DIM = 64
ATTN_BLOCK = 128
ROPE_THETA = 10000.0
N_GLA_HEADS = 4
GLA_DK = 64
GLA_DV = 128
GLA_RANK = 16
GLA_GATE_NORMALIZER = 16.0
GLA_CHUNK = 64
ATTN_Q = N_ATTN_HEADS * HEAD_DIM
ATTN_KV = N_KV_HEADS * HEAD_DIM
GLA_QK = N_GLA_HEADS * GLA_DK
GLA_V = N_GLA_HEADS * GLA_DV
IN_PROJ_WIDTH = ATTN_Q + 2 * ATTN_KV + 2 * GLA_QK + 2 * GLA_V + 2 * GLA_RANK
LANE = 128
MXU_COLS = 256
IN_PROJ_PAD = ((IN_PROJ_WIDTH + LANE - 1) // LANE) * LANE
OFF_AQ = 0
OFF_AK = OFF_AQ + ATTN_Q
OFF_AV = OFF_AK + ATTN_KV
OFF_GQ = OFF_AV + ATTN_KV
OFF_GK = OFF_GQ + GLA_QK
OFF_GV = OFF_GK + GLA_QK
OFF_GG = OFF_GV + GLA_V
OFF_R = OFF_GG + GLA_V

ROW_TILE = 512
ROW_SLAB = 256
FF_CHUNK_FFN1 = 768
FF_CHUNK_FFN2 = 256
IN_PROJ_COLS = MXU_COLS
IN_PROJ_DOTS = ATTN_Q // IN_PROJ_COLS + 3 + 2 * (GLA_V // IN_PROJ_COLS) + 1
GLA_BLOCK = 256
VMEM_LIMIT = 56 * 1024 * 1024


def _rms(x, gain):
    return x * lax.rsqrt(jnp.mean(x * x, axis=-1, keepdims=True) + EPS) * gain


def _silu(x):
    return x * (1.0 / (1.0 + jnp.exp(-x)))


def _advance(side, n):
    for _ in range(n):
        next(side, None)


def _ffn_ticks(ff_chunk):
    return 1 + -(-D_FF // ff_chunk) + D_MODEL // MXU_COLS


def _swiglu_residual(x_slabs, gain_ref, wg_ref, wu_ref, wd_ref, act_ref, ff_chunk, side, side_per_tick):
    bounds = list(range(0, D_FF, ff_chunk)) + [D_FF]
    _advance(side, side_per_tick)
    gain = gain_ref[...]
    hs = [_rms(x, gain).astype(BF16) for x in x_slabs]

    def up_chunk(h, rows, c):
        sl = slice(bounds[c], bounds[c + 1])
        g = jnp.dot(h, wg_ref[:, sl], preferred_element_type=F32)
        u = jnp.dot(h, wu_ref[:, sl], preferred_element_type=F32)
        act_ref[rows, sl] = (_silu(g) * u).astype(BF16)

    row0 = 0
    for h in hs:
        up_chunk(h, slice(row0, row0 + h.shape[0]), 0)
        row0 += h.shape[0]
    _advance(side, side_per_tick)
    h = jnp.concatenate(hs, axis=0)
    for c in range(1, len(bounds) - 1):
        up_chunk(h, slice(0, row0), c)
        _advance(side, side_per_tick)
    x = jnp.concatenate(x_slabs, axis=0)
    out = []
    for j in range(D_MODEL // MXU_COLS):
        cols = slice(j * MXU_COLS, (j + 1) * MXU_COLS)
        y = jnp.dot(act_ref[...], wd_ref[:, cols], preferred_element_type=F32)
        out.append(x[:, cols] + 0.5 * y)
        _advance(side, side_per_tick)
    return jnp.concatenate(out, axis=1)


def _rope_pair(x, cos, sin_signed, first_half):
    swapped = jnp.where(first_half, pltpu.roll(x, LANE - HEAD_DIM // 2, 1), pltpu.roll(x, HEAD_DIM // 2, 1))
    return x * cos + swapped * sin_signed


GLA_CHUNKS_PER_BLOCK = GLA_BLOCK // GLA_CHUNK
GLA_BLOCKS_PER_TILE = ROW_TILE // GLA_BLOCK
GLA_STAGES_PER_TILE = GLA_BLOCKS_PER_TILE * (4 + 2 * GLA_CHUNKS_PER_BLOCK)


def _round_robin(*stage_generators):
    live = list(stage_generators)
    while live:
        for gen in list(live):
            try:
                next(gen)
                yield
            except StopIteration:
                live.remove(gen)


def _gla_block_stages(q_ref, k_ref, v_ref, r_ref, brows, wdec_ref, bdec_ref, tri_ref, s_ref, reverse, emit):
    bt = brows.stop - brows.start
    nch = bt // GLA_CHUNK
    z = jnp.dot(r_ref[brows, :], wdec_ref[...], preferred_element_type=F32) + bdec_ref[...]
    yield
    log_a = (jnp.minimum(z, 0.0) - jnp.log1p(jnp.exp(-jnp.abs(z)))) * (1.0 / GLA_GATE_NORMALIZER)
    hi = log_a.astype(BF16)
    lo = (log_a - hi.astype(F32)).astype(BF16)
    tri = tri_ref[...]
    cum = jnp.dot(tri, hi, preferred_element_type=F32) + jnp.dot(tri, lo, preferred_element_type=F32)
    yield
    q = q_ref[brows, :]
    k = k_ref[brows, :]
    cum3 = cum.reshape(nch, GLA_CHUNK, GLA_QK)
    edge = GLA_CHUNK - 1 if not reverse else 0
    tot3 = cum3[:, edge:edge + 1, :]
    rest = (tot3 - cum3).reshape(bt, GLA_QK)
    qe = (q * (GLA_DK ** -0.5) * jnp.exp(cum)).astype(BF16)
    ke = (k * jnp.exp(-cum)).astype(BF16)
    ks_t = (k * jnp.exp(rest)).T.astype(BF16)
    tot = tot3.reshape(nch, GLA_QK)
    tot_t = jnp.concatenate([tot, jnp.zeros((LANE - nch, GLA_QK), F32)], axis=0).T
    decay_t = jnp.exp(tot_t)

    ii = lax.broadcasted_iota(jnp.int32, (GLA_CHUNK, GLA_CHUNK), 0)
    jj = lax.broadcasted_iota(jnp.int32, (GLA_CHUNK, GLA_CHUNK), 1)
    keep = (jj > ii) if reverse else (jj <= ii)
    rows = [slice(n * GLA_CHUNK, (n + 1) * GLA_CHUNK) for n in range(nch)]
    klanes = [slice(h * GLA_DK, (h + 1) * GLA_DK) for h in range(N_GLA_HEADS)]

    def v_of(n, h):
        return v_ref[brows.start + n * GLA_CHUNK:brows.start + (n + 1) * GLA_CHUNK, h * GLA_DV:(h + 1) * GLA_DV]

    yield

    a = [[None] * N_GLA_HEADS for _ in range(nch)]
    u = [None] * nch
    for n in range(nch):
        for h in range(N_GLA_HEADS):
            s_nh = lax.dot_general(qe[rows[n], klanes[h]], ke[rows[n], klanes[h]], (((1,), (1,)), ((), ())),
                                   preferred_element_type=F32)
            a[n][h] = jnp.where(keep, s_nh, 0.0).astype(BF16)
        u[n] = jnp.concatenate([jnp.dot(ks_t[klanes[h], rows[n]], v_of(n, h), preferred_element_type=F32)
                                for h in range(N_GLA_HEADS)], axis=0)
        yield
    s = s_ref[...]
    s_in = [None] * nch
    for n in (range(nch - 1, -1, -1) if reverse else range(nch)):
        s_in[n] = s.astype(BF16)
        s = decay_t[:, n:n + 1] * s + u[n]
    s_ref[...] = s
    yield
    for n in range(nch):
        for h in range(N_GLA_HEADS):
            emit(n, h, jnp.dot(a[n][h], v_of(n, h), preferred_element_type=F32)
                 + jnp.dot(qe[rows[n], klanes[h]], s_in[n][klanes[h], :], preferred_element_type=F32))
        yield


def _gla_tile_streams(q_ref, k_ref, v_ref, r_ref, wdec_ref, bdec_ref, tri_ref, s_ref, reverse, emit):
    streams = []
    for blk in (range(GLA_BLOCKS_PER_TILE - 1, -1, -1) if reverse else range(GLA_BLOCKS_PER_TILE)):
        brows = slice(blk * GLA_BLOCK, (blk + 1) * GLA_BLOCK)
        emit_block = lambda n, h, o, base=blk * GLA_BLOCK: emit(base + n * GLA_CHUNK, h, o)
        streams.append(_gla_block_stages(q_ref, k_ref, v_ref, r_ref, brows, wdec_ref, bdec_ref, tri_ref, s_ref,
                                         reverse, emit_block))
    return streams


def _reverse_sweep_kernel(x_ref, n1_ref, wg_ref, wu_ref, wd_ref, nm_ref, win_ref, cos_ref, sin_ref,
                          wdec_ref, bdec_ref, tri_ref,
                          x1_ref, qa_ref, ka_ref, va_ref, gq_ref, gk_ref, gv_ref, gg_ref, r_ref, ob_ref,
                          act_ref, pq_ref, pk_ref, pv_ref, pr_ref, s_ref, *, tiles_per_seq):
    g = pl.program_id(0)
    last = pl.num_programs(0) - 1
    cur_slot = g % 2
    prev_slot = 1 - cur_slot

    @pl.when(jnp.maximum(g - 1, 0) % tiles_per_seq == 0)
    def _():
        s_ref[...] = jnp.zeros_like(s_ref)

    def emit_ob(row0, h, o):
        ob_ref[row0:row0 + GLA_CHUNK, h * GLA_DV:(h + 1) * GLA_DV] = o

    def ffn1_inproj(side):
        x_slabs = [x_ref[r0:r0 + ROW_SLAB, :] for r0 in range(0, ROW_TILE, ROW_SLAB)]
        side_per_tick = -(-(GLA_STAGES_PER_TILE - IN_PROJ_DOTS) // _ffn_ticks(FF_CHUNK_FFN1))
        x1 = _swiglu_residual(x_slabs, n1_ref, wg_ref, wu_ref, wd_ref, act_ref, FF_CHUNK_FFN1, side, side_per_tick)
        x1_ref[...] = x1
        h = jnp.concatenate([_rms(x1[r0:r0 + ROW_SLAB, :], nm_ref[...]).astype(BF16)
                             for r0 in range(0, ROW_TILE, ROW_SLAB)], axis=0)
        cos = cos_ref[...]
        sin = sin_ref[...]
        lane = lax.broadcasted_iota(jnp.int32, (1, LANE), 1)
        first_half = (lane % HEAD_DIM) < (HEAD_DIM // 2)

        def proj(off, width):
            _advance(side, 1)
            return jnp.dot(h, win_ref[:, off:off + width], preferred_element_type=F32)

        scale = HEAD_DIM ** -0.5
        for j in range(ATTN_Q // IN_PROJ_COLS):
            q2 = proj(OFF_AQ + j * IN_PROJ_COLS, IN_PROJ_COLS)
            for i in range(IN_PROJ_COLS // LANE):
                q = _rope_pair(q2[:, i * LANE:(i + 1) * LANE], cos, sin, first_half)
                qa_ref[:, j * IN_PROJ_COLS + i * LANE:j * IN_PROJ_COLS + (i + 1) * LANE] = (q * scale).astype(BF16)
        kv = proj(OFF_AK, 2 * ATTN_KV)
        ka_ref[...] = _rope_pair(kv[:, :ATTN_KV], cos, sin, first_half).astype(BF16)
        va_ref[...] = kv[:, ATTN_KV:].T.astype(BF16)
        gq = proj(OFF_GQ, GLA_QK)
        gq_ref[...] = gq
        pq_ref[cur_slot] = gq
        gk = proj(OFF_GK, GLA_QK)
        gk_ref[...] = gk
        pk_ref[cur_slot] = gk
        for j in range(GLA_V // IN_PROJ_COLS):
            cols = slice(j * IN_PROJ_COLS, (j + 1) * IN_PROJ_COLS)
            gv = proj(OFF_GV + j * IN_PROJ_COLS, IN_PROJ_COLS).astype(BF16)
            gv_ref[:, cols] = gv
            pv_ref[cur_slot, :, cols] = gv
            gg_ref[:, cols] = proj(OFF_GG + j * IN_PROJ_COLS, IN_PROJ_COLS)
        rr = proj(OFF_R, LANE)[:, :2 * GLA_RANK].astype(BF16)
        r_ref[...] = rr
        pr_ref[cur_slot] = rr

    def step(with_ffn, with_scan):
        side = iter(())
        if with_scan:
            side = _round_robin(*_gla_tile_streams(pq_ref.at[prev_slot], pk_ref.at[prev_slot], pv_ref.at[prev_slot],
                                                   pr_ref.at[prev_slot], wdec_ref, bdec_ref, tri_ref, s_ref, True,
                                                   emit_ob))
        if with_ffn:
            ffn1_inproj(side)
        for _ in side:
            pass

    pl.when(g == 0)(lambda: step(True, False))
    pl.when((g > 0) & (g < last))(lambda: step(True, True))
    pl.when(g == last)(lambda: step(False, True))


def _const_spec(shape):
    return pl.BlockSpec(shape, lambda *_: (0,) * len(shape), pipeline_mode=pl.Buffered(1))


def _reverse_sweep(x2d, seq_len, n1, wg, wu, wd, nm, win, cos_tab, sin_tab, wdec, bdec, tri_up):
    n_rows = x2d.shape[0]
    tm = ROW_TILE
    assert n_rows % tm == 0 and seq_len % tm == 0
    nt = seq_len // tm
    n_tiles = n_rows // tm

    def tile_of(step):
        return (step // nt) * nt + (nt - 1 - step % nt)

    cur = lambda g: tile_of(jnp.minimum(g, n_tiles - 1))
    lag = lambda g: tile_of(jnp.maximum(g - 1, 0))
    row = lambda w: pl.BlockSpec((tm, w), lambda g: (cur(g), 0))
    rope = pl.BlockSpec((tm, LANE), lambda g: (nt - 1 - jnp.minimum(g, n_tiles - 1) % nt, 0))
    out_shapes = (
        jax.ShapeDtypeStruct((n_rows, D_MODEL), F32),
        jax.ShapeDtypeStruct((n_rows, ATTN_Q), BF16),
        jax.ShapeDtypeStruct((n_rows, ATTN_KV), BF16),
        jax.ShapeDtypeStruct((ATTN_KV, n_rows), BF16),
        jax.ShapeDtypeStruct((n_rows, GLA_QK), F32),
        jax.ShapeDtypeStruct((n_rows, GLA_QK), F32),
        jax.ShapeDtypeStruct((n_rows, GLA_V), BF16),
        jax.ShapeDtypeStruct((n_rows, GLA_V), F32),
        jax.ShapeDtypeStruct((n_rows, 2 * GLA_RANK), BF16),
        jax.ShapeDtypeStruct((n_rows, GLA_V), F32),
    )
    return pl.pallas_call(
        functools.partial(_reverse_sweep_kernel, tiles_per_seq=nt),
        grid=(n_tiles + 1,),
        in_specs=[
            row(D_MODEL),
            _const_spec((1, D_MODEL)),
            _const_spec((D_MODEL, D_FF)), _const_spec((D_MODEL, D_FF)), _const_spec((D_FF, D_MODEL)),
            _const_spec((1, D_MODEL)),
            _const_spec((D_MODEL, IN_PROJ_PAD)),
            rope, rope,
            _const_spec((2 * GLA_RANK, GLA_QK)), _const_spec((1, GLA_QK)), _const_spec((GLA_BLOCK, GLA_BLOCK)),
        ],
        out_specs=[row(D_MODEL), row(ATTN_Q), row(ATTN_KV), pl.BlockSpec((ATTN_KV, tm), lambda g: (0, cur(g))),
                   row(GLA_QK), row(GLA_QK),
                   row(GLA_V), row(GLA_V), row(2 * GLA_RANK),
                   pl.BlockSpec((tm, GLA_V), lambda g: (lag(g), 0))],
        out_shape=out_shapes,
        scratch_shapes=[pltpu.VMEM((tm, D_FF), BF16),
                        pltpu.VMEM((2, tm, GLA_QK), F32), pltpu.VMEM((2, tm, GLA_QK), F32),
                        pltpu.VMEM((2, tm, GLA_V), BF16), pltpu.VMEM((2, tm, 2 * GLA_RANK), BF16),
                        pltpu.VMEM((GLA_QK, GLA_DV), F32)],
        compiler_params=pltpu.CompilerParams(dimension_semantics=("arbitrary",), vmem_limit_bytes=VMEM_LIMIT),
        name="reverse_sweep",
    )(x2d, n1, wg, wu, wd, nm, win, cos_tab, sin_tab, wdec, bdec, tri_up)


ATTN_UNITS_PER_TILE = (ROW_TILE // ATTN_BLOCK) * N_KV_HEADS
ATTN_STAGES_PER_TILE = 3 * ATTN_UNITS_PER_TILE


def _attention_tile_stages(sink_ref, qa_ref, kp_ref, kc_ref, kn_ref, vp_ref, vc_ref, vn_ref, mix_ref, tpos,
                           tiles_per_seq):
    sub = ROW_TILE // ATTN_BLOCK
    n_qblocks = tiles_per_seq * sub
    n_keys = 3 * ATTN_BLOCK
    n_cols = ATTN_GROUP * ATTN_BLOCK
    kbuf = jnp.concatenate([kp_ref[...], kc_ref[...], kn_ref[...]], axis=0)
    vbuf_t = jnp.concatenate([vp_ref[...], vc_ref[...], vn_ref[...]], axis=1)
    kj = lax.broadcasted_iota(jnp.int32, (n_keys, n_cols), 0)
    col = lax.broadcasted_iota(jnp.int32, (n_keys, n_cols), 1)
    qi = col % ATTN_BLOCK
    in_window = (kj >= qi) & (kj <= qi + 2 * ATTN_BLOCK)
    head_of_col = lax.broadcasted_iota(jnp.int32, (1, n_cols), 1) // ATTN_BLOCK
    pending = {}

    def unit(jb, kv):
        qblk = tpos * sub + jb
        qrows = slice(jb * ATTN_BLOCK, (jb + 1) * ATTN_BLOCK)
        krows = slice(jb * ATTN_BLOCK, (jb + 3) * ATTN_BLOCK)
        kvl = slice(kv * HEAD_DIM, (kv + 1) * HEAD_DIM)
        heads = range(kv * ATTN_GROUP, (kv + 1) * ATTN_GROUP)
        qs = jnp.concatenate([qa_ref[qrows, h * HEAD_DIM:(h + 1) * HEAD_DIM] for h in heads], axis=0)
        s_t = lax.dot_general(kbuf[krows, kvl], qs, (((1,), (1,)), ((), ())), preferred_element_type=F32)
        yield
        mask = in_window
        if jb == 0:
            mask = mask & ((kj >= ATTN_BLOCK) | (qblk > 0))
        if jb == sub - 1:
            mask = mask & ((kj < 2 * ATTN_BLOCK) | (qblk < n_qblocks - 1))
        s_t = jnp.where(mask, s_t, -1e30)
        sink = jnp.full((1, n_cols), sink_ref[heads[-1]], F32)
        for hl in range(ATTN_GROUP - 2, -1, -1):
            sink = jnp.where(head_of_col == hl, sink_ref[heads[hl]], sink)
        m = jnp.maximum(jnp.max(s_t, axis=0, keepdims=True), sink)
        p = jnp.exp(s_t - m)
        denom = jnp.sum(p, axis=0, keepdims=True) + jnp.exp(sink - m)
        p_t = p.astype(BF16)
        yield
        pending[kv] = jnp.dot(vbuf_t[kvl, krows], p_t, preferred_element_type=F32) / denom
        if kv == N_KV_HEADS - 1:
            o_t = jnp.concatenate([pending.pop(i) for i in range(N_KV_HEADS)], axis=0)
            for hl in range(ATTN_GROUP):
                cols = slice(hl * ATTN_BLOCK, (hl + 1) * ATTN_BLOCK)
                mix_ref[qrows, hl * ATTN_KV:(hl + 1) * ATTN_KV] = o_t[:, cols].T.astype(BF16)
        yield

    units = [unit(jb, kv) for jb in range(sub) for kv in range(N_KV_HEADS)]
    for slot in range(len(units) + 4):
        for stage in range(3):
            u = slot - 2 * stage
            if 0 <= u < len(units):
                next(units[u])
                yield


def _forward_sweep_kernel(sink_ref, qa_ref, kp_ref, kc_ref, kn_ref, vp_ref, vc_ref, vn_ref,
                          gq_ref, gk_ref, gv_ref, gg_ref, r_ref, ob_ref, wdec_ref, bdec_ref, tri_ref, gnorm_ref,
                          x1_ref, wout_ref, n2_ref, wg_ref, wu_ref, wd_ref, nf_ref,
                          y_ref, act_ref, mix_ref, s_ref, *, tiles_per_seq):
    g = pl.program_id(0)
    last = pl.num_programs(0) - 1
    mix_cur = mix_ref.at[g % 2]
    mix_prev = mix_ref.at[1 - g % 2]

    @pl.when(g % tiles_per_seq == 0)
    def _():
        s_ref[...] = jnp.zeros_like(s_ref)

    tpos = g % tiles_per_seq
    gain = gnorm_ref[...]

    def emit_mix(row0, h, o):
        rows = slice(row0, row0 + GLA_CHUNK)
        cols = slice(h * GLA_DV, (h + 1) * GLA_DV)
        o = _rms(o + ob_ref[rows, cols], gain) * _silu(gg_ref[rows, cols])
        mix_cur[rows, ATTN_Q + h * GLA_DV:ATTN_Q + (h + 1) * GLA_DV] = o.astype(BF16)

    def outproj_ffn2(side):
        side_per_tick = -(-(ATTN_STAGES_PER_TILE + GLA_STAGES_PER_TILE) // _ffn_ticks(FF_CHUNK_FFN2))
        x2_slabs = [x1_ref[r0:r0 + ROW_SLAB, :]
                    + jnp.dot(mix_prev[r0:r0 + ROW_SLAB, :], wout_ref[...], preferred_element_type=F32)
                    for r0 in range(0, ROW_TILE, ROW_SLAB)]
        x3 = _swiglu_residual(x2_slabs, n2_ref, wg_ref, wu_ref, wd_ref, act_ref, FF_CHUNK_FFN2, side, side_per_tick)
        for r0 in range(0, ROW_TILE, ROW_SLAB):
            y_ref[r0:r0 + ROW_SLAB, :] = _rms(x3[r0:r0 + ROW_SLAB, :], nf_ref[...])

    def step(with_mixer, with_ffn):
        side = iter(())
        if with_mixer:
            side = _round_robin(
                _attention_tile_stages(sink_ref, qa_ref, kp_ref, kc_ref, kn_ref, vp_ref, vc_ref, vn_ref, mix_cur,
                                       tpos, tiles_per_seq),
                *_gla_tile_streams(gq_ref, gk_ref, gv_ref, r_ref, wdec_ref, bdec_ref, tri_ref, s_ref, False,
                                   emit_mix))
        if with_ffn:
            outproj_ffn2(side)
        for _ in side:
            pass

    pl.when(g == 0)(lambda: step(True, False))
    pl.when((g > 0) & (g < last))(lambda: step(True, True))
    pl.when(g == last)(lambda: step(False, True))


def _forward_sweep(seq_len, sink, qa, ka, va, gq, gk, gv, gg, r, ob, wdec, bdec, tri_lo, gnorm,
                   x1, wout, n2, wg, wu, wd, nf):
    n_rows = x1.shape[0]
    tm = ROW_TILE
    nt = seq_len // tm
    n_tiles = n_rows // tm
    sub = tm // ATTN_BLOCK
    halo_per_seq = seq_len // ATTN_BLOCK
    cur = lambda g: jnp.minimum(g, n_tiles - 1)
    lag = lambda g: jnp.maximum(g - 1, 0)
    row = lambda w: pl.BlockSpec((tm, w), lambda g: (cur(g), 0))
    lag_row = lambda w: pl.BlockSpec((tm, w), lambda g: (lag(g), 0))

    def prev_idx(g):
        t = cur(g)
        return jnp.maximum(t * sub - 1, (t // nt) * halo_per_seq)

    def next_idx(g):
        t = cur(g)
        return jnp.minimum((t + 1) * sub, (t // nt + 1) * halo_per_seq - 1)

    prev = pl.BlockSpec((ATTN_BLOCK, ATTN_KV), lambda g: (prev_idx(g), 0))
    nxt = pl.BlockSpec((ATTN_BLOCK, ATTN_KV), lambda g: (next_idx(g), 0))
    prev_t = pl.BlockSpec((ATTN_KV, ATTN_BLOCK), lambda g: (0, prev_idx(g)))
    cur_t = pl.BlockSpec((ATTN_KV, tm), lambda g: (0, cur(g)))
    nxt_t = pl.BlockSpec((ATTN_KV, ATTN_BLOCK), lambda g: (0, next_idx(g)))
    return pl.pallas_call(
        functools.partial(_forward_sweep_kernel, tiles_per_seq=nt),
        grid=(n_tiles + 1,),
        in_specs=[pl.BlockSpec(memory_space=pltpu.SMEM),
                  row(ATTN_Q), prev, row(ATTN_KV), nxt, prev_t, cur_t, nxt_t,
                  row(GLA_QK), row(GLA_QK), row(GLA_V), row(GLA_V), row(2 * GLA_RANK), row(GLA_V),
                  _const_spec((2 * GLA_RANK, GLA_QK)), _const_spec((1, GLA_QK)), _const_spec((GLA_BLOCK, GLA_BLOCK)),
                  _const_spec((1, GLA_DV)),
                  lag_row(D_MODEL), _const_spec((D_MODEL, D_MODEL)), _const_spec((1, D_MODEL)),
                  _const_spec((D_MODEL, D_FF)), _const_spec((D_MODEL, D_FF)), _const_spec((D_FF, D_MODEL)),
                  _const_spec((1, D_MODEL))],
        out_specs=lag_row(D_MODEL),
        out_shape=jax.ShapeDtypeStruct((n_rows, D_MODEL), F32),
        scratch_shapes=[pltpu.VMEM((tm, D_FF), BF16), pltpu.VMEM((2, tm, D_MODEL), BF16),
                        pltpu.VMEM((GLA_QK, GLA_DV), F32)],
        compiler_params=pltpu.CompilerParams(dimension_semantics=("arbitrary",), vmem_limit_bytes=VMEM_LIMIT),
        name="forward_sweep",
    )(sink, qa, ka, ka, ka, va, va, va, gq, gk, gv, gg, r, ob, wdec, bdec, tri_lo, gnorm,
      x1, wout, n2, wg, wu, wd, nf)


def _rope_tables(seq_len):
    half = HEAD_DIM // 2
    inv_freq = ROPE_THETA ** (-jnp.arange(half, dtype=F32) / half)
    ang = jnp.arange(seq_len, dtype=F32)[:, None] * inv_freq[None, :]
    cos, sin = jnp.cos(ang), jnp.sin(ang)
    cos_tab = jnp.tile(cos, (1, LANE // half))
    sin_tab = jnp.tile(jnp.concatenate([-sin, sin], axis=1), (1, LANE // HEAD_DIM))
    return cos_tab, sin_tab


def _chunk_tri(bt, upper):
    i = np.arange(bt)[:, None]
    j = np.arange(bt)[None, :]
    same_chunk = (i // GLA_CHUNK) == (j // GLA_CHUNK)
    keep = (j >= i) if upper else (j <= i)
    return jnp.asarray(same_chunk & keep, dtype=BF16)


def _trunk(x, p):
    batch, seq_len, _ = x.shape
    cos_tab, sin_tab = _rope_tables(seq_len)
    x2d = x.reshape(batch * seq_len, D_MODEL)
    x1, qa, ka, va, gq, gk, gv, gg, r, ob = _reverse_sweep(
        x2d, seq_len, p["n1"], p["wg1"], p["wu1"], p["wd1"], p["nm"], p["win"], cos_tab, sin_tab,
        p["wdec_b"], p["bdec_b"], _chunk_tri(GLA_BLOCK, True))
    y = _forward_sweep(seq_len, p["sink"], qa, ka, va, gq, gk, gv, gg, r, ob,
                       p["wdec_f"], p["bdec_f"], _chunk_tri(GLA_BLOCK, False), p["gnorm"],
                       x1, p["wout"], p["n2"], p["wg2"], p["wu2"], p["wd2"], p["nf"])
    return y.reshape(batch, seq_len, D_MODEL)


def kernel(x_prompt, x_sample, norm_ffn1, w_ffn1_gate, w_ffn1_up, w_ffn1_down, norm_mix, w_in, attn_sink, w_gla_decay_fwd, b_gla_decay_fwd, w_gla_decay_bwd, b_gla_decay_bwd, gla_out_norm, w_out, norm_ffn2, w_ffn2_gate, w_ffn2_up, w_ffn2_down, norm_final):
    assert norm_ffn1.shape[0] == 1, "single-layer trunk"
    zeros_rank = jnp.zeros((GLA_RANK, GLA_QK), F32)
    p = dict(
        n1=norm_ffn1[0][None, :], wg1=w_ffn1_gate[0].astype(BF16), wu1=w_ffn1_up[0].astype(BF16),
        wd1=w_ffn1_down[0].astype(BF16),
        nm=norm_mix[0][None, :],
        win=jnp.pad(w_in[0], ((0, 0), (0, IN_PROJ_PAD - IN_PROJ_WIDTH))).astype(BF16),
        sink=attn_sink[0],
        wdec_f=jnp.concatenate([w_gla_decay_fwd[0], zeros_rank], axis=0).astype(BF16),
        bdec_f=b_gla_decay_fwd[0][None, :],
        wdec_b=jnp.concatenate([zeros_rank, w_gla_decay_bwd[0]], axis=0).astype(BF16),
        bdec_b=b_gla_decay_bwd[0][None, :],
        gnorm=gla_out_norm[0][None, :],
        wout=jnp.concatenate([
            w_out[0][:ATTN_Q].reshape(N_KV_HEADS, ATTN_GROUP, HEAD_DIM, D_MODEL).transpose(1, 0, 2, 3)
            .reshape(ATTN_Q, D_MODEL), w_out[0][ATTN_Q:]], axis=0).astype(BF16),
        n2=norm_ffn2[0][None, :], wg2=w_ffn2_gate[0].astype(BF16), wu2=w_ffn2_up[0].astype(BF16),
        wd2=w_ffn2_down[0].astype(BF16),
        nf=norm_final[None, :],
    )
    return _trunk(x_prompt, p), _trunk(x_sample, p)
```

```python
import functools

import jax
import jax.numpy as jnp
import numpy as np
from jax import lax
from jax.experimental import pallas as pl
from jax.experimental.pallas import tpu as pltpu

F32 = jnp.float32
BF16 = jnp.bfloat16

D_MODEL = 1024
D_FF = 2816
EPS = 1e-6
N_ATTN_HEADS = 8
N_KV_HEADS = 2
ATTN_GROUP = N_ATTN_HEADS // N_KV_HEADS
HEAD_DIM = 64
ATTN_BLOCK = 128
ROPE_THETA = 10000.0
N_GLA_HEADS = 4
GLA_DK = 64
GLA_DV = 128
GLA_RANK = 16
GLA_GATE_NORMALIZER = 16.0
GLA_CHUNK = 64
ATTN_Q = N_ATTN_HEADS * HEAD_DIM
ATTN_KV = N_KV_HEADS * HEAD_DIM
GLA_QK = N_GLA_HEADS * GLA_DK
GLA_V = N_GLA_HEADS * GLA_DV
IN_PROJ_WIDTH = ATTN_Q + 2 * ATTN_KV + 2 * GLA_QK + 2 * GLA_V + 2 * GLA_RANK
LANE = 128
MXU_COLS = 256
IN_PROJ_PAD = ((IN_PROJ_WIDTH + LANE - 1) // LANE) * LANE
OFF_AQ = 0
OFF_AK = OFF_AQ + ATTN_Q
OFF_AV = OFF_AK + ATTN_KV
OFF_GQ = OFF_AV + ATTN_KV
OFF_GK = OFF_GQ + GLA_QK
OFF_GV = OFF_GK + GLA_QK
OFF_GG = OFF_GV + GLA_V
OFF_R = OFF_GG + GLA_V

ROW_TILE = 512
ROW_SLAB = 256
FF_CHUNK_FFN1 = 768
FF_CHUNK_FFN2 = 256
IN_PROJ_COLS = MXU_COLS
IN_PROJ_DOTS = ATTN_Q // IN_PROJ_COLS + 3 + 2 * (GLA_V // IN_PROJ_COLS) + 1
GLA_BLOCK = 256
VMEM_LIMIT = 56 * 1024 * 1024


def _rms(x, gain):
    return x * lax.rsqrt(jnp.mean(x * x, axis=-1, keepdims=True) + EPS) * gain


def _silu(x):
    return x * (1.0 / (1.0 + jnp.exp(-x)))


def _advance(side, n):
    for _ in range(n):
        next(side, None)


def _ffn_ticks(ff_chunk):
    return 1 + -(-D_FF // ff_chunk) + D_MODEL // MXU_COLS


def _swiglu_residual(x_slabs, gain_ref, wg_ref, wu_ref, wd_ref, act_ref, ff_chunk, side, side_per_tick):
    bounds = list(range(0, D_FF, ff_chunk)) + [D_FF]
    _advance(side, side_per_tick)
    gain = gain_ref[...]
    hs = [_rms(x, gain).astype(BF16) for x in x_slabs]

    def up_chunk(h, rows, c):
        sl = slice(bounds[c], bounds[c + 1])
        g = jnp.dot(h, wg_ref[:, sl], preferred_element_type=F32)
        u = jnp.dot(h, wu_ref[:, sl], preferred_element_type=F32)
        act_ref[rows, sl] = (_silu(g) * u).astype(BF16)

    row0 = 0
    for h in hs:
        up_chunk(h, slice(row0, row0 + h.shape[0]), 0)
        row0 += h.shape[0]
    _advance(side, side_per_tick)
    h = jnp.concatenate(hs, axis=0)
    for c in range(1, len(bounds) - 1):
        up_chunk(h, slice(0, row0), c)
        _advance(side, side_per_tick)
    x = jnp.concatenate(x_slabs, axis=0)
    out = []
    for j in range(D_MODEL // MXU_COLS):
        cols = slice(j * MXU_COLS, (j + 1) * MXU_COLS)
        y = jnp.dot(act_ref[...], wd_ref[:, cols], preferred_element_type=F32)
        out.append(x[:, cols] + 0.5 * y)
        _advance(side, side_per_tick)
    return jnp.concatenate(out, axis=1)


def _rope_pair(x, cos, sin_signed, first_half):
    swapped = jnp.where(first_half, pltpu.roll(x, LANE - HEAD_DIM // 2, 1), pltpu.roll(x, HEAD_DIM // 2, 1))
    return x * cos + swapped * sin_signed


GLA_CHUNKS_PER_BLOCK = GLA_BLOCK // GLA_CHUNK
GLA_BLOCKS_PER_TILE = ROW_TILE // GLA_BLOCK
GLA_STAGES_PER_TILE = GLA_BLOCKS_PER_TILE * (4 + 2 * GLA_CHUNKS_PER_BLOCK)


def _round_robin(*stage_generators):
    live = list(stage_generators)
    while live:
        for gen in list(live):
            try:
                next(gen)
                yield
            except StopIteration:
                live.remove(gen)


def _gla_block_stages(q_ref, k_ref, v_ref, r_ref, brows, wdec_ref, bdec_ref, tri_ref, s_ref, reverse, emit):
    bt = brows.stop - brows.start
    nch = bt // GLA_CHUNK
    z = jnp.dot(r_ref[brows, :], wdec_ref[...], preferred_element_type=F32) + bdec_ref[...]
    yield
    log_a = (jnp.minimum(z, 0.0) - jnp.log1p(jnp.exp(-jnp.abs(z)))) * (1.0 / GLA_GATE_NORMALIZER)
    hi = log_a.astype(BF16)
    lo = (log_a - hi.astype(F32)).astype(BF16)
    tri = tri_ref[...]
    cum = jnp.dot(tri, hi, preferred_element_type=F32) + jnp.dot(tri, lo, preferred_element_type=F32)
    yield
    q = q_ref[brows, :]
    k = k_ref[brows, :]
    cum3 = cum.reshape(nch, GLA_CHUNK, GLA_QK)
    edge = GLA_CHUNK - 1 if not reverse else 0
    tot3 = cum3[:, edge:edge + 1, :]
    rest = (tot3 - cum3).reshape(bt, GLA_QK)
    qe = (q * (GLA_DK ** -0.5) * jnp.exp(cum)).astype(BF16)
    ke = (k * jnp.exp(-cum)).astype(BF16)
    ks_t = (k * jnp.exp(rest)).T.astype(BF16)
    tot = tot3.reshape(nch, GLA_QK)
    tot_t = jnp.concatenate([tot, jnp.zeros((LANE - nch, GLA_QK), F32)], axis=0).T
    decay_t = jnp.exp(tot_t)

    ii = lax.broadcasted_iota(jnp.int32, (GLA_CHUNK, GLA_CHUNK), 0)
    jj = lax.broadcasted_iota(jnp.int32, (GLA_CHUNK, GLA_CHUNK), 1)
    keep = (jj > ii) if reverse else (jj <= ii)
    rows = [slice(n * GLA_CHUNK, (n + 1) * GLA_CHUNK) for n in range(nch)]
    klanes = [slice(h * GLA_DK, (h + 1) * GLA_DK) for h in range(N_GLA_HEADS)]

    def v_of(n, h):
        return v_ref[brows.start + n * GLA_CHUNK:brows.start + (n + 1) * GLA_CHUNK, h * GLA_DV:(h + 1) * GLA_DV]

    yield

    a = [[None] * N_GLA_HEADS for _ in range(nch)]
    u = [None] * nch
    for n in range(nch):
        for h in range(N_GLA_HEADS):
            s_nh = lax.dot_general(qe[rows[n], klanes[h]], ke[rows[n], klanes[h]], (((1,), (1,)), ((), ())),
                                   preferred_element_type=F32)
            a[n][h] = jnp.where(keep, s_nh, 0.0).astype(BF16)
        u[n] = jnp.concatenate([jnp.dot(ks_t[klanes[h], rows[n]], v_of(n, h), preferred_element_type=F32)
                                for h in range(N_GLA_HEADS)], axis=0)
        yield
    s = s_ref[...]
    s_in = [None] * nch
    for n in (range(nch - 1, -1, -1) if reverse else range(nch)):
        s_in[n] = s.astype(BF16)
        s = decay_t[:, n:n + 1] * s + u[n]
    s_ref[...] = s
    yield
    for n in range(nch):
        for h in range(N_GLA_HEADS):
            emit(n, h, jnp.dot(a[n][h], v_of(n, h), preferred_element_type=F32)
                 + jnp.dot(qe[rows[n], klanes[h]], s_in[n][klanes[h], :], preferred_element_type=F32))
        yield


def _gla_tile_streams(q_ref, k_ref, v_ref, r_ref, wdec_ref, bdec_ref, tri_ref, s_ref, reverse, emit):
    streams = []
    for blk in (range(GLA_BLOCKS_PER_TILE - 1, -1, -1) if reverse else range(GLA_BLOCKS_PER_TILE)):
        brows = slice(blk * GLA_BLOCK, (blk + 1) * GLA_BLOCK)
        emit_block = lambda n, h, o, base=blk * GLA_BLOCK: emit(base + n * GLA_CHUNK, h, o)
        streams.append(_gla_block_stages(q_ref, k_ref, v_ref, r_ref, brows, wdec_ref, bdec_ref, tri_ref, s_ref,
                                         reverse, emit_block))
    return streams


def _reverse_sweep_kernel(x_ref, n1_ref, wg_ref, wu_ref, wd_ref, nm_ref, win_ref, cos_ref, sin_ref,
                          wdec_ref, bdec_ref, tri_ref,
                          x1_ref, qa_ref, ka_ref, va_ref, gq_ref, gk_ref, gv_ref, gg_ref, r_ref, ob_ref,
                          act_ref, pq_ref, pk_ref, pv_ref, pr_ref, s_ref, *, tiles_per_seq):
    g = pl.program_id(0)
    last = pl.num_programs(0) - 1
    cur_slot = g % 2
    prev_slot = 1 - cur_slot

    @pl.when(jnp.maximum(g - 1, 0) % tiles_per_seq == 0)
    def _():
        s_ref[...] = jnp.zeros_like(s_ref)

    def emit_ob(row0, h, o):
        ob_ref[row0:row0 + GLA_CHUNK, h * GLA_DV:(h + 1) * GLA_DV] = o

    def ffn1_inproj(side):
        x_slabs = [x_ref[r0:r0 + ROW_SLAB, :] for r0 in range(0, ROW_TILE, ROW_SLAB)]
        side_per_tick = -(-(GLA_STAGES_PER_TILE - IN_PROJ_DOTS) // _ffn_ticks(FF_CHUNK_FFN1))
        x1 = _swiglu_residual(x_slabs, n1_ref, wg_ref, wu_ref, wd_ref, act_ref, FF_CHUNK_FFN1, side, side_per_tick)
        x1_ref[...] = x1
        h = jnp.concatenate([_rms(x1[r0:r0 + ROW_SLAB, :], nm_ref[...]).astype(BF16)
                             for r0 in range(0, ROW_TILE, ROW_SLAB)], axis=0)
        cos = cos_ref[...]
        sin = sin_ref[...]
        lane = lax.broadcasted_iota(jnp.int32, (1, LANE), 1)
        first_half = (lane % HEAD_DIM) < (HEAD_DIM // 2)

        def proj(off, width):
            _advance(side, 1)
            return jnp.dot(h, win_ref[:, off:off + width], preferred_element_type=F32)

        scale = HEAD_DIM ** -0.5
        for j in range(ATTN_Q // IN_PROJ_COLS):
            q2 = proj(OFF_AQ + j * IN_PROJ_COLS, IN_PROJ_COLS)
            for i in range(IN_PROJ_COLS // LANE):
                q = _rope_pair(q2[:, i * LANE:(i + 1) * LANE], cos, sin, first_half)
                qa_ref[:, j * IN_PROJ_COLS + i * LANE:j * IN_PROJ_COLS + (i + 1) * LANE] = (q * scale).astype(BF16)
        kv = proj(OFF_AK, 2 * ATTN_KV)
        ka_ref[...] = _rope_pair(kv[:, :ATTN_KV], cos, sin, first_half).astype(BF16)
        va_ref[...] = kv[:, ATTN_KV:].T.astype(BF16)
        gq = proj(OFF_GQ, GLA_QK)
        gq_ref[...] = gq
        pq_ref[cur_slot] = gq
        gk = proj(OFF_GK, GLA_QK)
        gk_ref[...] = gk
        pk_ref[cur_slot] = gk
        for j in range(GLA_V // IN_PROJ_COLS):
            cols = slice(j * IN_PROJ_COLS, (j + 1) * IN_PROJ_COLS)
            gv = proj(OFF_GV + j * IN_PROJ_COLS, IN_PROJ_COLS).astype(BF16)
            gv_ref[:, cols] = gv
            pv_ref[cur_slot, :, cols] = gv
            gg_ref[:, cols] = proj(OFF_GG + j * IN_PROJ_COLS, IN_PROJ_COLS)
        rr = proj(OFF_R, LANE)[:, :2 * GLA_RANK].astype(BF16)
        r_ref[...] = rr
        pr_ref[cur_slot] = rr

    def step(with_ffn, with_scan):
        side = iter(())
        if with_scan:
            side = _round_robin(*_gla_tile_streams(pq_ref.at[prev_slot], pk_ref.at[prev_slot], pv_ref.at[prev_slot],
                                                   pr_ref.at[prev_slot], wdec_ref, bdec_ref, tri_ref, s_ref, True,
                                                   emit_ob))
        if with_ffn:
            ffn1_inproj(side)
        for _ in side:
            pass

    @pl.when(g == 0)
    def _():
        pq_ref[...] = jnp.zeros_like(pq_ref)
        pk_ref[...] = jnp.zeros_like(pk_ref)
        pv_ref[...] = jnp.zeros_like(pv_ref)
        pr_ref[...] = jnp.zeros_like(pr_ref)

    pl.when(g < last)(lambda: step(True, True))
    pl.when(g == last)(lambda: step(False, True))


def _const_spec(shape):
    return pl.BlockSpec(shape, lambda *_: (0,) * len(shape), pipeline_mode=pl.Buffered(1))


def _reverse_sweep(x2d, seq_len, n1, wg, wu, wd, nm, win, cos_tab, sin_tab, wdec, bdec, tri_up):
    n_rows = x2d.shape[0]
    tm = ROW_TILE
    assert n_rows % tm == 0 and seq_len % tm == 0
    nt = seq_len // tm
    n_tiles = n_rows // tm

    def tile_of(step):
        return (step // nt) * nt + (nt - 1 - step % nt)

    cur = lambda g: tile_of(jnp.minimum(g, n_tiles - 1))
    lag = lambda g: tile_of(jnp.maximum(g - 1, 0))
    row = lambda w: pl.BlockSpec((tm, w), lambda g: (cur(g), 0))
    rope = pl.BlockSpec((tm, LANE), lambda g: (nt - 1 - jnp.minimum(g, n_tiles - 1) % nt, 0))
    out_shapes = (
        jax.ShapeDtypeStruct((n_rows, D_MODEL), F32),
        jax.ShapeDtypeStruct((n_rows, ATTN_Q), BF16),
        jax.ShapeDtypeStruct((n_rows, ATTN_KV), BF16),
        jax.ShapeDtypeStruct((ATTN_KV, n_rows), BF16),
        jax.ShapeDtypeStruct((n_rows, GLA_QK), F32),
        jax.ShapeDtypeStruct((n_rows, GLA_QK), F32),
        jax.ShapeDtypeStruct((n_rows, GLA_V), BF16),
        jax.ShapeDtypeStruct((n_rows, GLA_V), F32),
        jax.ShapeDtypeStruct((n_rows, 2 * GLA_RANK), BF16),
        jax.ShapeDtypeStruct((n_rows, GLA_V), F32),
    )
    return pl.pallas_call(
        functools.partial(_reverse_sweep_kernel, tiles_per_seq=nt),
        grid=(n_tiles + 1,),
        in_specs=[
            row(D_MODEL),
            _const_spec((1, D_MODEL)),
            _const_spec((D_MODEL, D_FF)), _const_spec((D_MODEL, D_FF)), _const_spec((D_FF, D_MODEL)),
            _const_spec((1, D_MODEL)),
            _const_spec((D_MODEL, IN_PROJ_PAD)),
            rope, rope,
            _const_spec((2 * GLA_RANK, GLA_QK)), _const_spec((1, GLA_QK)), _const_spec((GLA_BLOCK, GLA_BLOCK)),
        ],
        out_specs=[row(D_MODEL), row(ATTN_Q), row(ATTN_KV), pl.BlockSpec((ATTN_KV, tm), lambda g: (0, cur(g))),
                   row(GLA_QK), row(GLA_QK),
                   row(GLA_V), row(GLA_V), row(2 * GLA_RANK),
                   pl.BlockSpec((tm, GLA_V), lambda g: (lag(g), 0))],
        out_shape=out_shapes,
        scratch_shapes=[pltpu.VMEM((tm, D_FF), BF16),
                        pltpu.VMEM((2, tm, GLA_QK), F32), pltpu.VMEM((2, tm, GLA_QK), F32),
                        pltpu.VMEM((2, tm, GLA_V), BF16), pltpu.VMEM((2, tm, 2 * GLA_RANK), BF16),
                        pltpu.VMEM((GLA_QK, GLA_DV), F32)],
        compiler_params=pltpu.CompilerParams(dimension_semantics=("arbitrary",), vmem_limit_bytes=VMEM_LIMIT),
        name="reverse_sweep",
    )(x2d, n1, wg, wu, wd, nm, win, cos_tab, sin_tab, wdec, bdec, tri_up)


ATTN_UNITS_PER_TILE = (ROW_TILE // ATTN_BLOCK) * N_KV_HEADS
ATTN_STAGES_PER_TILE = 3 * ATTN_UNITS_PER_TILE


def _attention_tile_stages(sink_ref, qa_ref, kp_ref, kc_ref, kn_ref, vp_ref, vc_ref, vn_ref, mix_ref, tpos,
                           tiles_per_seq):
    sub = ROW_TILE // ATTN_BLOCK
    n_qblocks = tiles_per_seq * sub
    n_keys = 3 * ATTN_BLOCK
    n_cols = ATTN_GROUP * ATTN_BLOCK
    kbuf = jnp.concatenate([kp_ref[...], kc_ref[...], kn_ref[...]], axis=0)
    vbuf_t = jnp.concatenate([vp_ref[...], vc_ref[...], vn_ref[...]], axis=1)
    kj = lax.broadcasted_iota(jnp.int32, (n_keys, n_cols), 0)
    col = lax.broadcasted_iota(jnp.int32, (n_keys, n_cols), 1)
    qi = col % ATTN_BLOCK
    in_window = (kj >= qi) & (kj <= qi + 2 * ATTN_BLOCK)
    head_of_col = lax.broadcasted_iota(jnp.int32, (1, n_cols), 1) // ATTN_BLOCK
    pending = {}

    def unit(jb, kv):
        qblk = tpos * sub + jb
        qrows = slice(jb * ATTN_BLOCK, (jb + 1) * ATTN_BLOCK)
        krows = slice(jb * ATTN_BLOCK, (jb + 3) * ATTN_BLOCK)
        kvl = slice(kv * HEAD_DIM, (kv + 1) * HEAD_DIM)
        heads = range(kv * ATTN_GROUP, (kv + 1) * ATTN_GROUP)
        qs = jnp.concatenate([qa_ref[qrows, h * HEAD_DIM:(h + 1) * HEAD_DIM] for h in heads], axis=0)
        s_t = lax.dot_general(kbuf[krows, kvl], qs, (((1,), (1,)), ((), ())), preferred_element_type=F32)
        yield
        mask = in_window
        if jb == 0:
            mask = mask & ((kj >= ATTN_BLOCK) | (qblk > 0))
        if jb == sub - 1:
            mask = mask & ((kj < 2 * ATTN_BLOCK) | (qblk < n_qblocks - 1))
        s_t = jnp.where(mask, s_t, -1e30)
        sink = jnp.full((1, n_cols), sink_ref[heads[-1]], F32)
        for hl in range(ATTN_GROUP - 2, -1, -1):
            sink = jnp.where(head_of_col == hl, sink_ref[heads[hl]], sink)
        m = jnp.maximum(jnp.max(s_t, axis=0, keepdims=True), sink)
        p = jnp.exp(s_t - m)
        denom = jnp.sum(p, axis=0, keepdims=True) + jnp.exp(sink - m)
        p_t = p.astype(BF16)
        yield
        pending[kv] = jnp.dot(vbuf_t[kvl, krows], p_t, preferred_element_type=F32) / denom
        if kv == N_KV_HEADS - 1:
            o_t = jnp.concatenate([pending.pop(i) for i in range(N_KV_HEADS)], axis=0)
            for hl in range(ATTN_GROUP):
                cols = slice(hl * ATTN_BLOCK, (hl + 1) * ATTN_BLOCK)
                mix_ref[qrows, hl * ATTN_KV:(hl + 1) * ATTN_KV] = o_t[:, cols].T.astype(BF16)
        yield

    units = [unit(jb, kv) for jb in range(sub) for kv in range(N_KV_HEADS)]
    for slot in range(len(units) + 4):
        for stage in range(3):
            u = slot - 2 * stage
            if 0 <= u < len(units):
                next(units[u])
                yield


def _forward_sweep_kernel(sink_ref, qa_ref, kp_ref, kc_ref, kn_ref, vp_ref, vc_ref, vn_ref,
                          gq_ref, gk_ref, gv_ref, gg_ref, r_ref, ob_ref, wdec_ref, bdec_ref, tri_ref, gnorm_ref,
                          x1_ref, wout_ref, n2_ref, wg_ref, wu_ref, wd_ref, nf_ref,
                          y_ref, act_ref, mix_ref, s_ref, *, tiles_per_seq):
    g = pl.program_id(0)
    mix_cur = mix_ref.at[g % 2]
    mix_prev = mix_ref.at[1 - g % 2]

    @pl.when(g % tiles_per_seq == 0)
    def _():
        s_ref[...] = jnp.zeros_like(s_ref)

    tpos = jnp.minimum(g, pl.num_programs(0) - 2) % tiles_per_seq
    gain = gnorm_ref[...]

    def emit_mix(row0, h, o):
        rows = slice(row0, row0 + GLA_CHUNK)
        cols = slice(h * GLA_DV, (h + 1) * GLA_DV)
        o = _rms(o + ob_ref[rows, cols], gain) * _silu(gg_ref[rows, cols])
        mix_cur[rows, ATTN_Q + h * GLA_DV:ATTN_Q + (h + 1) * GLA_DV] = o.astype(BF16)

    def outproj_ffn2(side):
        side_per_tick = -(-(ATTN_STAGES_PER_TILE + GLA_STAGES_PER_TILE) // _ffn_ticks(FF_CHUNK_FFN2))
        x2_slabs = [x1_ref[r0:r0 + ROW_SLAB, :]
                    + jnp.dot(mix_prev[r0:r0 + ROW_SLAB, :], wout_ref[...], preferred_element_type=F32)
                    for r0 in range(0, ROW_TILE, ROW_SLAB)]
        x3 = _swiglu_residual(x2_slabs, n2_ref, wg_ref, wu_ref, wd_ref, act_ref, FF_CHUNK_FFN2, side, side_per_tick)
        for r0 in range(0, ROW_TILE, ROW_SLAB):
            y_ref[r0:r0 + ROW_SLAB, :] = _rms(x3[r0:r0 + ROW_SLAB, :], nf_ref[...])

    def step(with_mixer, with_ffn):
        side = iter(())
        if with_mixer:
            side = _round_robin(
                _attention_tile_stages(sink_ref, qa_ref, kp_ref, kc_ref, kn_ref, vp_ref, vc_ref, vn_ref, mix_cur,
                                       tpos, tiles_per_seq),
                *_gla_tile_streams(gq_ref, gk_ref, gv_ref, r_ref, wdec_ref, bdec_ref, tri_ref, s_ref, False,
                                   emit_mix))
        if with_ffn:
            outproj_ffn2(side)
        for _ in side:
            pass

    pl.when(g == 0)(lambda: step(True, False))
    pl.when(g > 0)(lambda: step(True, True))


def _forward_sweep(seq_len, sink, qa, ka, va, gq, gk, gv, gg, r, ob, wdec, bdec, tri_lo, gnorm,
                   x1, wout, n2, wg, wu, wd, nf):
    n_rows = x1.shape[0]
    tm = ROW_TILE
    nt = seq_len // tm
    n_tiles = n_rows // tm
    sub = tm // ATTN_BLOCK
    halo_per_seq = seq_len // ATTN_BLOCK
    cur = lambda g: jnp.minimum(g, n_tiles - 1)
    lag = lambda g: jnp.maximum(g - 1, 0)
    row = lambda w: pl.BlockSpec((tm, w), lambda g: (cur(g), 0))
    lag_row = lambda w: pl.BlockSpec((tm, w), lambda g: (lag(g), 0))

    def prev_idx(g):
        t = cur(g)
        return jnp.maximum(t * sub - 1, (t // nt) * halo_per_seq)

    def next_idx(g):
        t = cur(g)
        return jnp.minimum((t + 1) * sub, (t // nt + 1) * halo_per_seq - 1)

    prev = pl.BlockSpec((ATTN_BLOCK, ATTN_KV), lambda g: (prev_idx(g), 0))
    nxt = pl.BlockSpec((ATTN_BLOCK, ATTN_KV), lambda g: (next_idx(g), 0))
    prev_t = pl.BlockSpec((ATTN_KV, ATTN_BLOCK), lambda g: (0, prev_idx(g)))
    cur_t = pl.BlockSpec((ATTN_KV, tm), lambda g: (0, cur(g)))
    nxt_t = pl.BlockSpec((ATTN_KV, ATTN_BLOCK), lambda g: (0, next_idx(g)))
    return pl.pallas_call(
        functools.partial(_forward_sweep_kernel, tiles_per_seq=nt),
        grid=(n_tiles + 1,),
        in_specs=[pl.BlockSpec(memory_space=pltpu.SMEM),
                  row(ATTN_Q), prev, row(ATTN_KV), nxt, prev_t, cur_t, nxt_t,
                  row(GLA_QK), row(GLA_QK), row(GLA_V), row(GLA_V), row(2 * GLA_RANK), row(GLA_V),
                  _const_spec((2 * GLA_RANK, GLA_QK)), _const_spec((1, GLA_QK)), _const_spec((GLA_BLOCK, GLA_BLOCK)),
                  _const_spec((1, GLA_DV)),
                  lag_row(D_MODEL), _const_spec((D_MODEL, D_MODEL)), _const_spec((1, D_MODEL)),
                  _const_spec((D_MODEL, D_FF)), _const_spec((D_MODEL, D_FF)), _const_spec((D_FF, D_MODEL)),
                  _const_spec((1, D_MODEL))],
        out_specs=lag_row(D_MODEL),
        out_shape=jax.ShapeDtypeStruct((n_rows, D_MODEL), F32),
        scratch_shapes=[pltpu.VMEM((tm, D_FF), BF16), pltpu.VMEM((2, tm, D_MODEL), BF16),
                        pltpu.VMEM((GLA_QK, GLA_DV), F32)],
        compiler_params=pltpu.CompilerParams(dimension_semantics=("arbitrary",), vmem_limit_bytes=VMEM_LIMIT),
        name="forward_sweep",
    )(sink, qa, ka, ka, ka, va, va, va, gq, gk, gv, gg, r, ob, wdec, bdec, tri_lo, gnorm,
      x1, wout, n2, wg, wu, wd, nf)


def _rope_tables(seq_len):
    half = HEAD_DIM // 2
    inv_freq = ROPE_THETA ** (-jnp.arange(half, dtype=F32) / half)
    ang = jnp.arange(seq_len, dtype=F32)[:, None] * inv_freq[None, :]
    cos, sin = jnp.cos(ang), jnp.sin(ang)
    cos_tab = jnp.tile(cos, (1, LANE // half))
    sin_tab = jnp.tile(jnp.concatenate([-sin, sin], axis=1), (1, LANE // HEAD_DIM))
    return cos_tab, sin_tab


def _chunk_tri(bt, upper):
    i = np.arange(bt)[:, None]
    j = np.arange(bt)[None, :]
    same_chunk = (i // GLA_CHUNK) == (j // GLA_CHUNK)
    keep = (j >= i) if upper else (j <= i)
    return jnp.asarray(same_chunk & keep, dtype=BF16)


def _trunk(x, p):
    batch, seq_len, _ = x.shape
    cos_tab, sin_tab = _rope_tables(seq_len)
    x2d = x.reshape(batch * seq_len, D_MODEL)
    x1, qa, ka, va, gq, gk, gv, gg, r, ob = _reverse_sweep(
        x2d, seq_len, p["n1"], p["wg1"], p["wu1"], p["wd1"], p["nm"], p["win"], cos_tab, sin_tab,
        p["wdec_b"], p["bdec_b"], _chunk_tri(GLA_BLOCK, True))
    y = _forward_sweep(seq_len, p["sink"], qa, ka, va, gq, gk, gv, gg, r, ob,
                       p["wdec_f"], p["bdec_f"], _chunk_tri(GLA_BLOCK, False), p["gnorm"],
                       x1, p["wout"], p["n2"], p["wg2"], p["wu2"], p["wd2"], p["nf"])
    return y.reshape(batch, seq_len, D_MODEL)


def kernel(x_prompt, x_sample, norm_ffn1, w_ffn1_gate, w_ffn1_up, w_ffn1_down, norm_mix, w_in, attn_sink, w_gla_decay_fwd, b_gla_decay_fwd, w_gla_decay_bwd, b_gla_decay_bwd, gla_out_norm, w_out, norm_ffn2, w_ffn2_gate, w_ffn2_up, w_ffn2_down, norm_final):
    assert norm_ffn1.shape[0] == 1, "single-layer trunk"
    zeros_rank = jnp.zeros((GLA_RANK, GLA_QK), F32)
    p = dict(
        n1=norm_ffn1[0][None, :], wg1=w_ffn1_gate[0].astype(BF16), wu1=w_ffn1_up[0].astype(BF16),
        wd1=w_ffn1_down[0].astype(BF16),
        nm=norm_mix[0][None, :],
        win=jnp.pad(w_in[0], ((0, 0), (0, IN_PROJ_PAD - IN_PROJ_WIDTH))).astype(BF16),
        sink=attn_sink[0],
        wdec_f=jnp.concatenate([w_gla_decay_fwd[0], zeros_rank], axis=0).astype(BF16),
        bdec_f=b_gla_decay_fwd[0][None, :],
        wdec_b=jnp.concatenate([zeros_rank, w_gla_decay_bwd[0]], axis=0).astype(BF16),
        bdec_b=b_gla_decay_bwd[0][None, :],
        gnorm=gla_out_norm[0][None, :],
        wout=jnp.concatenate([
            w_out[0][:ATTN_Q].reshape(N_KV_HEADS, ATTN_GROUP, HEAD_DIM, D_MODEL).transpose(1, 0, 2, 3)
            .reshape(ATTN_Q, D_MODEL), w_out[0][ATTN_Q:]], axis=0).astype(BF16),
        n2=norm_ffn2[0][None, :], wg2=w_ffn2_gate[0].astype(BF16), wu2=w_ffn2_up[0].astype(BF16),
        wd2=w_ffn2_down[0].astype(BF16),
        nf=norm_final[None, :],
    )
    return _trunk(x_prompt, p), _trunk(x_sample, p)
```

```python
import functools

import jax
import jax.numpy as jnp
import numpy as np
from jax import lax
from jax.experimental import pallas as pl
from jax.experimental.pallas import tpu as pltpu

F32 = jnp.float32
BF16 = jnp.bfloat16

D_MODEL = 1024
D_FF = 2816
EPS = 1e-6
N_ATTN_HEADS = 8
N_KV_HEADS = 2
ATTN_GROUP = N_ATTN_HEADS // N_KV_HEADS
HEAD_DIM = 64
ATTN_BLOCK = 128
ROPE_THETA = 10000.0
N_GLA_HEADS = 4
GLA_DK = 64
GLA_DV = 128
GLA_RANK = 16
GLA_GATE_NORMALIZER = 16.0
GLA_CHUNK = 64
ATTN_Q = N_ATTN_HEADS * HEAD_DIM
ATTN_KV = N_KV_HEADS * HEAD_DIM
GLA_QK = N_GLA_HEADS * GLA_DK
GLA_V = N_GLA_HEADS * GLA_DV
IN_PROJ_WIDTH = ATTN_Q + 2 * ATTN_KV + 2 * GLA_QK + 2 * GLA_V + 2 * GLA_RANK
LANE = 128
MXU_COLS = 256
OFF_AQ = 0
OFF_AK = OFF_AQ + ATTN_Q
OFF_AV = OFF_AK + ATTN_KV
OFF_GQ = OFF_AV + ATTN_KV
OFF_GK = OFF_GQ + GLA_QK
OFF_GV = OFF_GK + GLA_QK
OFF_GG = OFF_GV + GLA_V
OFF_R = OFF_GG + GLA_V

ROW_TILE = 512
ROW_SLAB = 256
FF_CHUNK_FFN1 = 768
FF_CHUNK_FFN2 = 256
IN_PROJ_COLS = MXU_COLS
IN_PROJ_DOTS = ATTN_Q // IN_PROJ_COLS + 3 + 2 * (GLA_V // IN_PROJ_COLS) + 1
GLA_BLOCK = 256
VMEM_LIMIT = 56 * 1024 * 1024


def _rms(x, gain):
    return x * lax.rsqrt(jnp.mean(x * x, axis=-1, keepdims=True) + EPS) * gain


def _silu(x):
    return x * (1.0 / (1.0 + jnp.exp(-x)))


def _advance(side, n):
    for _ in range(n):
        next(side, None)


def _ffn_ticks(ff_chunk):
    return 1 + -(-D_FF // ff_chunk) + D_MODEL // MXU_COLS


def _swiglu_residual(x_slabs, gain_ref, wg_ref, wu_ref, wd_ref, act_ref, ff_chunk, side, side_per_tick):
    bounds = list(range(0, D_FF, ff_chunk)) + [D_FF]
    _advance(side, side_per_tick)
    gain = gain_ref[...]
    hs = [_rms(x, gain).astype(BF16) for x in x_slabs]

    def up_chunk(h, rows, c):
        sl = slice(bounds[c], bounds[c + 1])
        g = jnp.dot(h, wg_ref[:, sl], preferred_element_type=F32)
        u = jnp.dot(h, wu_ref[:, sl], preferred_element_type=F32)
        act_ref[rows, sl] = (_silu(g) * u).astype(BF16)

    row0 = 0
    for h in hs:
        up_chunk(h, slice(row0, row0 + h.shape[0]), 0)
        row0 += h.shape[0]
    _advance(side, side_per_tick)
    h = jnp.concatenate(hs, axis=0)
    for c in range(1, len(bounds) - 1):
        up_chunk(h, slice(0, row0), c)
        _advance(side, side_per_tick)
    x = jnp.concatenate(x_slabs, axis=0)
    out = []
    for j in range(D_MODEL // MXU_COLS):
        cols = slice(j * MXU_COLS, (j + 1) * MXU_COLS)
        y = jnp.dot(act_ref[...], wd_ref[:, cols], preferred_element_type=F32)
        out.append(x[:, cols] + 0.5 * y)
        _advance(side, side_per_tick)
    return jnp.concatenate(out, axis=1)


def _rope_pair(x, cos, sin_signed, first_half):
    swapped = jnp.where(first_half, pltpu.roll(x, LANE - HEAD_DIM // 2, 1), pltpu.roll(x, HEAD_DIM // 2, 1))
    return x * cos + swapped * sin_signed


GLA_CHUNKS_PER_BLOCK = GLA_BLOCK // GLA_CHUNK
GLA_BLOCKS_PER_TILE = ROW_TILE // GLA_BLOCK
GLA_STAGES_PER_TILE = GLA_BLOCKS_PER_TILE * (4 + 2 * GLA_CHUNKS_PER_BLOCK)


def _round_robin(*stage_generators):
    live = list(stage_generators)
    while live:
        for gen in list(live):
            try:
                next(gen)
                yield
            except StopIteration:
                live.remove(gen)


def _gla_block_stages(q_ref, k_ref, v_ref, r_ref, brows, wdec_ref, bdec_ref, tri_ref, s_ref, reverse, emit):
    bt = brows.stop - brows.start
    nch = bt // GLA_CHUNK
    z = jnp.dot(r_ref[brows, :], wdec_ref[...], preferred_element_type=F32) + bdec_ref[...]
    yield
    log_a = (jnp.minimum(z, 0.0) - jnp.log1p(jnp.exp(-jnp.abs(z)))) * (1.0 / GLA_GATE_NORMALIZER)
    hi = log_a.astype(BF16)
    lo = (log_a - hi.astype(F32)).astype(BF16)
    tri = tri_ref[...]
    cum = jnp.dot(tri, hi, preferred_element_type=F32) + jnp.dot(tri, lo, preferred_element_type=F32)
    yield
    q = q_ref[brows, :]
    k = k_ref[brows, :]
    cum3 = cum.reshape(nch, GLA_CHUNK, GLA_QK)
    edge = GLA_CHUNK - 1 if not reverse else 0
    tot3 = cum3[:, edge:edge + 1, :]
    rest = (tot3 - cum3).reshape(bt, GLA_QK)
    qe = (q * (GLA_DK ** -0.5) * jnp.exp(cum)).astype(BF16)
    ke = (k * jnp.exp(-cum)).astype(BF16)
    ks_t = (k * jnp.exp(rest)).T.astype(BF16)
    tot = tot3.reshape(nch, GLA_QK)
    tot_t = jnp.concatenate([tot, jnp.zeros((LANE - nch, GLA_QK), F32)], axis=0).T
    decay_t = jnp.exp(tot_t)

    ii = lax.broadcasted_iota(jnp.int32, (GLA_CHUNK, GLA_CHUNK), 0)
    jj = lax.broadcasted_iota(jnp.int32, (GLA_CHUNK, GLA_CHUNK), 1)
    keep = (jj > ii) if reverse else (jj <= ii)
    rows = [slice(n * GLA_CHUNK, (n + 1) * GLA_CHUNK) for n in range(nch)]
    klanes = [slice(h * GLA_DK, (h + 1) * GLA_DK) for h in range(N_GLA_HEADS)]

    def v_of(n, h):
        return v_ref[brows.start + n * GLA_CHUNK:brows.start + (n + 1) * GLA_CHUNK, h * GLA_DV:(h + 1) * GLA_DV]

    yield

    a = [[None] * N_GLA_HEADS for _ in range(nch)]
    u = [None] * nch
    for n in range(nch):
        for h in range(N_GLA_HEADS):
            s_nh = lax.dot_general(qe[rows[n], klanes[h]], ke[rows[n], klanes[h]], (((1,), (1,)), ((), ())),
                                   preferred_element_type=F32)
            a[n][h] = jnp.where(keep, s_nh, 0.0).astype(BF16)
        u[n] = jnp.concatenate([jnp.dot(ks_t[klanes[h], rows[n]], v_of(n, h), preferred_element_type=F32)
                                for h in range(N_GLA_HEADS)], axis=0)
        yield
    s = s_ref[...]
    s_in = [None] * nch
    for n in (range(nch - 1, -1, -1) if reverse else range(nch)):
        s_in[n] = s.astype(BF16)
        s = decay_t[:, n:n + 1] * s + u[n]
    s_ref[...] = s
    yield
    for n in range(nch):
        for h in range(N_GLA_HEADS):
            emit(n, h, jnp.dot(a[n][h], v_of(n, h), preferred_element_type=F32)
                 + jnp.dot(qe[rows[n], klanes[h]], s_in[n][klanes[h], :], preferred_element_type=F32))
        yield


def _gla_tile_streams(q_ref, k_ref, v_ref, r_ref, wdec_ref, bdec_ref, tri_ref, s_ref, reverse, emit):
    streams = []
    for blk in (range(GLA_BLOCKS_PER_TILE - 1, -1, -1) if reverse else range(GLA_BLOCKS_PER_TILE)):
        brows = slice(blk * GLA_BLOCK, (blk + 1) * GLA_BLOCK)
        emit_block = lambda n, h, o, base=blk * GLA_BLOCK: emit(base + n * GLA_CHUNK, h, o)
        streams.append(_gla_block_stages(q_ref, k_ref, v_ref, r_ref, brows, wdec_ref, bdec_ref, tri_ref, s_ref,
                                         reverse, emit_block))
    return streams


def _reverse_sweep_kernel(x_ref, n1_ref, wg_ref, wu_ref, wd_ref, nm_ref, win_ref, cos_ref, sin_ref,
                          wdec_ref, bdec_ref, tri_ref,
                          x1_ref, qa_ref, ka_ref, va_ref, gq_ref, gk_ref, gv_ref, gg_ref, r_ref, ob_ref,
                          act_ref, pq_ref, pk_ref, pv_ref, pr_ref, s_ref, *, tiles_per_seq):
    g = pl.program_id(0)
    last = pl.num_programs(0) - 1
    cur_slot = g % 2
    prev_slot = 1 - cur_slot

    @pl.when(jnp.maximum(g - 1, 0) % tiles_per_seq == 0)
    def _():
        s_ref[...] = jnp.zeros_like(s_ref)

    def emit_ob(row0, h, o):
        ob_ref[row0:row0 + GLA_CHUNK, h * GLA_DV:(h + 1) * GLA_DV] = o

    def ffn1_inproj(side):
        x_slabs = [x_ref[r0:r0 + ROW_SLAB, :] for r0 in range(0, ROW_TILE, ROW_SLAB)]
        side_per_tick = -(-(GLA_STAGES_PER_TILE - IN_PROJ_DOTS) // _ffn_ticks(FF_CHUNK_FFN1))
        x1 = _swiglu_residual(x_slabs, n1_ref, wg_ref, wu_ref, wd_ref, act_ref, FF_CHUNK_FFN1, side, side_per_tick)
        x1_ref[...] = x1
        h = jnp.concatenate([_rms(x1[r0:r0 + ROW_SLAB, :], nm_ref[...]).astype(BF16)
                             for r0 in range(0, ROW_TILE, ROW_SLAB)], axis=0)
        cos = cos_ref[...]
        sin = sin_ref[...]
        lane = lax.broadcasted_iota(jnp.int32, (1, LANE), 1)
        first_half = (lane % HEAD_DIM) < (HEAD_DIM // 2)

        def proj(off, width):
            _advance(side, 1)
            return jnp.dot(h, win_ref[:, off:off + width], preferred_element_type=F32)

        scale = HEAD_DIM ** -0.5
        for j in range(ATTN_Q // IN_PROJ_COLS):
            q2 = proj(OFF_AQ + j * IN_PROJ_COLS, IN_PROJ_COLS)
            for i in range(IN_PROJ_COLS // LANE):
                q = _rope_pair(q2[:, i * LANE:(i + 1) * LANE], cos, sin, first_half)
                qa_ref[:, j * IN_PROJ_COLS + i * LANE:j * IN_PROJ_COLS + (i + 1) * LANE] = (q * scale).astype(BF16)
        kv = proj(OFF_AK, 2 * ATTN_KV)
        ka_ref[...] = _rope_pair(kv[:, :ATTN_KV], cos, sin, first_half).astype(BF16)
        va_ref[...] = kv[:, ATTN_KV:].T.astype(BF16)
        gq = proj(OFF_GQ, GLA_QK)
        gq_ref[...] = gq
        pq_ref[cur_slot] = gq
        gk = proj(OFF_GK, GLA_QK)
        gk_ref[...] = gk
        pk_ref[cur_slot] = gk
        for j in range(GLA_V // IN_PROJ_COLS):
            cols = slice(j * IN_PROJ_COLS, (j + 1) * IN_PROJ_COLS)
            gv = proj(OFF_GV + j * IN_PROJ_COLS, IN_PROJ_COLS).astype(BF16)
            gv_ref[:, cols] = gv
            pv_ref[cur_slot, :, cols] = gv
            gg_ref[:, cols] = proj(OFF_GG + j * IN_PROJ_COLS, IN_PROJ_COLS)
        rr = proj(OFF_R, 2 * GLA_RANK).astype(BF16)
        r_ref[...] = rr
        pr_ref[cur_slot] = rr

    def step(with_ffn, with_scan):
        side = iter(())
        if with_scan:
            side = _round_robin(*_gla_tile_streams(pq_ref.at[prev_slot], pk_ref.at[prev_slot], pv_ref.at[prev_slot],
                                                   pr_ref.at[prev_slot], wdec_ref, bdec_ref, tri_ref, s_ref, True,
                                                   emit_ob))
        if with_ffn:
            ffn1_inproj(side)
        for _ in side:
            pass

    @pl.when(g == 0)
    def _():
        pq_ref[...] = jnp.zeros_like(pq_ref)
        pk_ref[...] = jnp.zeros_like(pk_ref)
        pv_ref[...] = jnp.zeros_like(pv_ref)
        pr_ref[...] = jnp.zeros_like(pr_ref)

    pl.when(g < last)(lambda: step(True, True))
    pl.when(g == last)(lambda: step(False, True))


def _const_spec(shape):
    return pl.BlockSpec(shape, lambda *_: (0,) * len(shape), pipeline_mode=pl.Buffered(1))


def _reverse_sweep(x2d, seq_len, n1, wg, wu, wd, nm, win, cos_tab, sin_tab, wdec, bdec, tri_up):
    n_rows = x2d.shape[0]
    tm = ROW_TILE
    assert n_rows % tm == 0 and seq_len % tm == 0
    nt = seq_len // tm
    n_tiles = n_rows // tm

    def tile_of(step):
        return (step // nt) * nt + (nt - 1 - step % nt)

    cur = lambda g: tile_of(jnp.minimum(g, n_tiles - 1))
    lag = lambda g: tile_of(jnp.maximum(g - 1, 0))
    row = lambda w: pl.BlockSpec((tm, w), lambda g: (cur(g), 0))
    rope = pl.BlockSpec((tm, LANE), lambda g: (nt - 1 - jnp.minimum(g, n_tiles - 1) % nt, 0))
    out_shapes = (
        jax.ShapeDtypeStruct((n_rows, D_MODEL), F32),
        jax.ShapeDtypeStruct((n_rows, ATTN_Q), BF16),
        jax.ShapeDtypeStruct((n_rows, ATTN_KV), BF16),
        jax.ShapeDtypeStruct((ATTN_KV, n_rows), BF16),
        jax.ShapeDtypeStruct((n_rows, GLA_QK), F32),
        jax.ShapeDtypeStruct((n_rows, GLA_QK), F32),
        jax.ShapeDtypeStruct((n_rows, GLA_V), BF16),
        jax.ShapeDtypeStruct((n_rows, GLA_V), F32),
        jax.ShapeDtypeStruct((n_rows, 2 * GLA_RANK), BF16),
        jax.ShapeDtypeStruct((n_rows, GLA_V), F32),
    )
    return pl.pallas_call(
        functools.partial(_reverse_sweep_kernel, tiles_per_seq=nt),
        grid=(n_tiles + 1,),
        in_specs=[
            row(D_MODEL),
            _const_spec((1, D_MODEL)),
            _const_spec((D_MODEL, D_FF)), _const_spec((D_MODEL, D_FF)), _const_spec((D_FF, D_MODEL)),
            _const_spec((1, D_MODEL)),
            _const_spec((D_MODEL, IN_PROJ_WIDTH)),
            rope, rope,
            _const_spec((2 * GLA_RANK, GLA_QK)), _const_spec((1, GLA_QK)), _const_spec((GLA_BLOCK, GLA_BLOCK)),
        ],
        out_specs=[row(D_MODEL), row(ATTN_Q), row(ATTN_KV), pl.BlockSpec((ATTN_KV, tm), lambda g: (0, cur(g))),
                   row(GLA_QK), row(GLA_QK),
                   row(GLA_V), row(GLA_V), row(2 * GLA_RANK),
                   pl.BlockSpec((tm, GLA_V), lambda g: (lag(g), 0))],
        out_shape=out_shapes,
        scratch_shapes=[pltpu.VMEM((tm, D_FF), BF16),
                        pltpu.VMEM((2, tm, GLA_QK), F32), pltpu.VMEM((2, tm, GLA_QK), F32),
                        pltpu.VMEM((2, tm, GLA_V), BF16), pltpu.VMEM((2, tm, 2 * GLA_RANK), BF16),
                        pltpu.VMEM((GLA_QK, GLA_DV), F32)],
        compiler_params=pltpu.CompilerParams(dimension_semantics=("arbitrary",), vmem_limit_bytes=VMEM_LIMIT),
        name="reverse_sweep",
    )(x2d, n1, wg, wu, wd, nm, win, cos_tab, sin_tab, wdec, bdec, tri_up)


ATTN_UNITS_PER_TILE = (ROW_TILE // ATTN_BLOCK) * N_KV_HEADS
ATTN_STAGES_PER_TILE = 3 * ATTN_UNITS_PER_TILE


def _attention_tile_stages(sink_ref, qa_ref, kp_ref, kc_ref, kn_ref, vp_ref, vc_ref, vn_ref, mix_ref, tpos,
                           tiles_per_seq):
    sub = ROW_TILE // ATTN_BLOCK
    n_qblocks = tiles_per_seq * sub
    n_keys = 3 * ATTN_BLOCK
    n_cols = ATTN_GROUP * ATTN_BLOCK
    kbuf = jnp.concatenate([kp_ref[...], kc_ref[...], kn_ref[...]], axis=0)
    vbuf_t = jnp.concatenate([vp_ref[...], vc_ref[...], vn_ref[...]], axis=1)
    kj = lax.broadcasted_iota(jnp.int32, (n_keys, n_cols), 0)
    col = lax.broadcasted_iota(jnp.int32, (n_keys, n_cols), 1)
    qi = col % ATTN_BLOCK
    in_window = (kj >= qi) & (kj <= qi + 2 * ATTN_BLOCK)
    head_of_col = lax.broadcasted_iota(jnp.int32, (1, n_cols), 1) // ATTN_BLOCK
    pending = {}

    def unit(jb, kv):
        qblk = tpos * sub + jb
        qrows = slice(jb * ATTN_BLOCK, (jb + 1) * ATTN_BLOCK)
        krows = slice(jb * ATTN_BLOCK, (jb + 3) * ATTN_BLOCK)
        kvl = slice(kv * HEAD_DIM, (kv + 1) * HEAD_DIM)
        heads = range(kv * ATTN_GROUP, (kv + 1) * ATTN_GROUP)
        qs = jnp.concatenate([qa_ref[qrows, h * HEAD_DIM:(h + 1) * HEAD_DIM] for h in heads], axis=0)
        s_t = lax.dot_general(kbuf[krows, kvl], qs, (((1,), (1,)), ((), ())), preferred_element_type=F32)
        yield
        mask = in_window
        if jb == 0:
            mask = mask & ((kj >= ATTN_BLOCK) | (qblk > 0))
        if jb == sub - 1:
            mask = mask & ((kj < 2 * ATTN_BLOCK) | (qblk < n_qblocks - 1))
        s_t = jnp.where(mask, s_t, -1e30)
        sink = jnp.full((1, n_cols), sink_ref[heads[-1]], F32)
        for hl in range(ATTN_GROUP - 2, -1, -1):
            sink = jnp.where(head_of_col == hl, sink_ref[heads[hl]], sink)
        m = jnp.maximum(jnp.max(s_t, axis=0, keepdims=True), sink)
        p = jnp.exp(s_t - m)
        denom = jnp.sum(p, axis=0, keepdims=True) + jnp.exp(sink - m)
        p_t = p.astype(BF16)
        yield
        pending[kv] = jnp.dot(vbuf_t[kvl, krows], p_t, preferred_element_type=F32) / denom
        if kv == N_KV_HEADS - 1:
            o_t = jnp.concatenate([pending.pop(i) for i in range(N_KV_HEADS)], axis=0)
            for hl in range(ATTN_GROUP):
                cols = slice(hl * ATTN_BLOCK, (hl + 1) * ATTN_BLOCK)
                mix_ref[qrows, hl * ATTN_KV:(hl + 1) * ATTN_KV] = o_t[:, cols].T.astype(BF16)
        yield

    units = [unit(jb, kv) for jb in range(sub) for kv in range(N_KV_HEADS)]
    for slot in range(len(units) + 4):
        for stage in range(3):
            u = slot - 2 * stage
            if 0 <= u < len(units):
                next(units[u])
                yield


def _forward_sweep_kernel(sink_ref, qa_ref, kp_ref, kc_ref, kn_ref, vp_ref, vc_ref, vn_ref,
                          gq_ref, gk_ref, gv_ref, gg_ref, r_ref, ob_ref, wdec_ref, bdec_ref, tri_ref, gnorm_ref,
                          x1_ref, wout_ref, n2_ref, wg_ref, wu_ref, wd_ref, nf_ref,
                          y_ref, act_ref, mix_ref, s_ref, *, tiles_per_seq):
    g = pl.program_id(0)
    mix_cur = mix_ref.at[g % 2]
    mix_prev = mix_ref.at[1 - g % 2]

    @pl.when(g % tiles_per_seq == 0)
    def _():
        s_ref[...] = jnp.zeros_like(s_ref)

    tpos = jnp.minimum(g, pl.num_programs(0) - 2) % tiles_per_seq
    gain = gnorm_ref[...]

    def emit_mix(row0, h, o):
        rows = slice(row0, row0 + GLA_CHUNK)
        cols = slice(h * GLA_DV, (h + 1) * GLA_DV)
        o = _rms(o + ob_ref[rows, cols], gain) * _silu(gg_ref[rows, cols])
        mix_cur[rows, ATTN_Q + h * GLA_DV:ATTN_Q + (h + 1) * GLA_DV] = o.astype(BF16)

    def outproj_ffn2(side):
        side_per_tick = -(-(ATTN_STAGES_PER_TILE + GLA_STAGES_PER_TILE) // _ffn_ticks(FF_CHUNK_FFN2))
        x2_slabs = [x1_ref[r0:r0 + ROW_SLAB, :]
                    + jnp.dot(mix_prev[r0:r0 + ROW_SLAB, :], wout_ref[...], preferred_element_type=F32)
                    for r0 in range(0, ROW_TILE, ROW_SLAB)]
        x3 = _swiglu_residual(x2_slabs, n2_ref, wg_ref, wu_ref, wd_ref, act_ref, FF_CHUNK_FFN2, side, side_per_tick)
        for r0 in range(0, ROW_TILE, ROW_SLAB):
            y_ref[r0:r0 + ROW_SLAB, :] = _rms(x3[r0:r0 + ROW_SLAB, :], nf_ref[...])

    def step(with_mixer, with_ffn):
        side = iter(())
        if with_mixer:
            side = _round_robin(
                _attention_tile_stages(sink_ref, qa_ref, kp_ref, kc_ref, kn_ref, vp_ref, vc_ref, vn_ref, mix_cur,
                                       tpos, tiles_per_seq),
                *_gla_tile_streams(gq_ref, gk_ref, gv_ref, r_ref, wdec_ref, bdec_ref, tri_ref, s_ref, False,
                                   emit_mix))
        if with_ffn:
            outproj_ffn2(side)
        for _ in side:
            pass

    pl.when(g == 0)(lambda: step(True, False))
    pl.when(g > 0)(lambda: step(True, True))


def _forward_sweep(seq_len, sink, qa, ka, va, gq, gk, gv, gg, r, ob, wdec, bdec, tri_lo, gnorm,
                   x1, wout, n2, wg, wu, wd, nf):
    n_rows = x1.shape[0]
    tm = ROW_TILE
    nt = seq_len // tm
    n_tiles = n_rows // tm
    sub = tm // ATTN_BLOCK
    halo_per_seq = seq_len // ATTN_BLOCK
    cur = lambda g: jnp.minimum(g, n_tiles - 1)
    lag = lambda g: jnp.maximum(g - 1, 0)
    row = lambda w: pl.BlockSpec((tm, w), lambda g: (cur(g), 0))
    lag_row = lambda w: pl.BlockSpec((tm, w), lambda g: (lag(g), 0))

    def prev_idx(g):
        t = cur(g)
        return jnp.maximum(t * sub - 1, (t // nt) * halo_per_seq)

    def next_idx(g):
        t = cur(g)
        return jnp.minimum((t + 1) * sub, (t // nt + 1) * halo_per_seq - 1)

    prev = pl.BlockSpec((ATTN_BLOCK, ATTN_KV), lambda g: (prev_idx(g), 0))
    nxt = pl.BlockSpec((ATTN_BLOCK, ATTN_KV), lambda g: (next_idx(g), 0))
    prev_t = pl.BlockSpec((ATTN_KV, ATTN_BLOCK), lambda g: (0, prev_idx(g)))
    cur_t = pl.BlockSpec((ATTN_KV, tm), lambda g: (0, cur(g)))
    nxt_t = pl.BlockSpec((ATTN_KV, ATTN_BLOCK), lambda g: (0, next_idx(g)))
    return pl.pallas_call(
        functools.partial(_forward_sweep_kernel, tiles_per_seq=nt),
        grid=(n_tiles + 1,),
        in_specs=[pl.BlockSpec(memory_space=pltpu.SMEM),
                  row(ATTN_Q), prev, row(ATTN_KV), nxt, prev_t, cur_t, nxt_t,
                  row(GLA_QK), row(GLA_QK), row(GLA_V), row(GLA_V), row(2 * GLA_RANK), row(GLA_V),
                  _const_spec((2 * GLA_RANK, GLA_QK)), _const_spec((1, GLA_QK)), _const_spec((GLA_BLOCK, GLA_BLOCK)),
                  _const_spec((1, GLA_DV)),
                  lag_row(D_MODEL), _const_spec((D_MODEL, D_MODEL)), _const_spec((1, D_MODEL)),
                  _const_spec((D_MODEL, D_FF)), _const_spec((D_MODEL, D_FF)), _const_spec((D_FF, D_MODEL)),
                  _const_spec((1, D_MODEL))],
        out_specs=lag_row(D_MODEL),
        out_shape=jax.ShapeDtypeStruct((n_rows, D_MODEL), F32),
        scratch_shapes=[pltpu.VMEM((tm, D_FF), BF16), pltpu.VMEM((2, tm, D_MODEL), BF16),
                        pltpu.VMEM((GLA_QK, GLA_DV), F32)],
        compiler_params=pltpu.CompilerParams(dimension_semantics=("arbitrary",), vmem_limit_bytes=VMEM_LIMIT),
        name="forward_sweep",
    )(sink, qa, ka, ka, ka, va, va, va, gq, gk, gv, gg, r, ob, wdec, bdec, tri_lo, gnorm,
      x1, wout, n2, wg, wu, wd, nf)


def _rope_tables(seq_len):
    half = HEAD_DIM // 2
    inv_freq = ROPE_THETA ** (-jnp.arange(half, dtype=F32) / half)
    inv_lane = jnp.tile(inv_freq, LANE // half)
    sign = jnp.tile(jnp.concatenate([-jnp.ones(half, F32), jnp.ones(half, F32)]), LANE // HEAD_DIM)
    ang = jnp.arange(seq_len, dtype=F32)[:, None] * inv_lane[None, :]
    return jnp.cos(ang), jnp.sin(ang) * sign[None, :]


def _chunk_tri(bt, upper):
    i = np.arange(bt)[:, None]
    j = np.arange(bt)[None, :]
    same_chunk = (i // GLA_CHUNK) == (j // GLA_CHUNK)
    keep = (j >= i) if upper else (j <= i)
    return jnp.asarray(same_chunk & keep, dtype=BF16)


def _trunk(x, p):
    batch, seq_len, _ = x.shape
    cos_tab, sin_tab = _rope_tables(seq_len)
    x2d = x.reshape(batch * seq_len, D_MODEL)
    x1, qa, ka, va, gq, gk, gv, gg, r, ob = _reverse_sweep(
        x2d, seq_len, p["n1"], p["wg1"], p["wu1"], p["wd1"], p["nm"], p["win"], cos_tab, sin_tab,
        p["wdec_b"], p["bdec_b"], _chunk_tri(GLA_BLOCK, True))
    y = _forward_sweep(seq_len, p["sink"], qa, ka, va, gq, gk, gv, gg, r, ob,
                       p["wdec_f"], p["bdec_f"], _chunk_tri(GLA_BLOCK, False), p["gnorm"],
                       x1, p["wout"], p["n2"], p["wg2"], p["wu2"], p["wd2"], p["nf"])
    return y.reshape(batch, seq_len, D_MODEL)


def kernel(x_prompt, x_sample, norm_ffn1, w_ffn1_gate, w_ffn1_up, w_ffn1_down, norm_mix, w_in, attn_sink, w_gla_decay_fwd, b_gla_decay_fwd, w_gla_decay_bwd, b_gla_decay_bwd, gla_out_norm, w_out, norm_ffn2, w_ffn2_gate, w_ffn2_up, w_ffn2_down, norm_final):
    assert norm_ffn1.shape[0] == 1, "single-layer trunk"
    zeros_rank = jnp.zeros((GLA_RANK, GLA_QK), F32)
    p = dict(
        n1=norm_ffn1[0][None, :], wg1=w_ffn1_gate[0].astype(BF16), wu1=w_ffn1_up[0].astype(BF16),
        wd1=w_ffn1_down[0].astype(BF16),
        nm=norm_mix[0][None, :],
        win=w_in[0].astype(BF16),
        sink=attn_sink[0],
        wdec_f=jnp.concatenate([w_gla_decay_fwd[0], zeros_rank], axis=0).astype(BF16),
        bdec_f=b_gla_decay_fwd[0][None, :],
        wdec_b=jnp.concatenate([zeros_rank, w_gla_decay_bwd[0]], axis=0).astype(BF16),
        bdec_b=b_gla_decay_bwd[0][None, :],
        gnorm=gla_out_norm[0][None, :],
        wout=jnp.concatenate([
            w_out[0][:ATTN_Q].reshape(N_KV_HEADS, ATTN_GROUP, HEAD_DIM, D_MODEL).transpose(1, 0, 2, 3)
            .reshape(ATTN_Q, D_MODEL), w_out[0][ATTN_Q:]], axis=0).astype(BF16),
        n2=norm_ffn2[0][None, :], wg2=w_ffn2_gate[0].astype(BF16), wu2=w_ffn2_up[0].astype(BF16),
        wd2=w_ffn2_down[0].astype(BF16),
        nf=norm_final[None, :],
    )
    return _trunk(x_prompt, p), _trunk(x_sample, p)
```

```python
import functools

import jax
import jax.numpy as jnp
import numpy as np
from jax import lax
from jax.experimental import pallas as pl
from jax.experimental.pallas import tpu as pltpu

F32 = jnp.float32
BF16 = jnp.bfloat16

D_MODEL = 1024
D_FF = 2816
EPS = 1e-6
N_ATTN_HEADS = 8
N_KV_HEADS = 2
ATTN_GROUP = N_ATTN_HEADS // N_KV_HEADS
HEAD_DIM = 64
ATTN_BLOCK = 128
ROPE_THETA = 10000.0
N_GLA_HEADS = 4
GLA_DK = 64
GLA_DV = 128
GLA_RANK = 16
GLA_GATE_NORMALIZER = 16.0
GLA_CHUNK = 64
ATTN_Q = N_ATTN_HEADS * HEAD_DIM
ATTN_KV = N_KV_HEADS * HEAD_DIM
GLA_QK = N_GLA_HEADS * GLA_DK
GLA_V = N_GLA_HEADS * GLA_DV
IN_PROJ_WIDTH = ATTN_Q + 2 * ATTN_KV + 2 * GLA_QK + 2 * GLA_V + 2 * GLA_RANK
LANE = 128
MXU_COLS = 256
OFF_AQ = 0
OFF_AK = OFF_AQ + ATTN_Q
OFF_AV = OFF_AK + ATTN_KV
OFF_GQ = OFF_AV + ATTN_KV
OFF_GK = OFF_GQ + GLA_QK
OFF_GV = OFF_GK + GLA_QK
OFF_GG = OFF_GV + GLA_V
OFF_R = OFF_GG + GLA_V

ROW_TILE = 512
ROW_SLAB = 256
FF_CHUNK_FFN1 = 768
FF_CHUNK_FFN2 = 256
IN_PROJ_COLS = MXU_COLS
IN_PROJ_DOTS = ATTN_Q // IN_PROJ_COLS + 3 + 2 * (GLA_V // IN_PROJ_COLS) + 1
GLA_BLOCK = 256
VMEM_LIMIT = 56 * 1024 * 1024


def _rms(x, gain):
    return x * lax.rsqrt(jnp.mean(x * x, axis=-1, keepdims=True) + EPS) * gain


def _silu(x):
    return x * (1.0 / (1.0 + jnp.exp(-x)))


def _advance(side, n):
    for _ in range(n):
        next(side, None)


def _ffn_ticks(ff_chunk):
    return 1 + -(-D_FF // ff_chunk) + D_MODEL // MXU_COLS


def _swiglu_residual(x_slabs, gain_ref, wg_ref, wu_ref, wd_ref, act_ref, ff_chunk, side, side_per_tick):
    bounds = list(range(0, D_FF, ff_chunk)) + [D_FF]
    _advance(side, side_per_tick)
    gain = gain_ref[...]
    hs = [_rms(x, gain).astype(BF16) for x in x_slabs]

    def up_chunk(h, rows, c):
        sl = slice(bounds[c], bounds[c + 1])
        g = jnp.dot(h, wg_ref[:, sl], preferred_element_type=F32)
        u = jnp.dot(h, wu_ref[:, sl], preferred_element_type=F32)
        act_ref[rows, sl] = (_silu(g) * u).astype(BF16)

    row0 = 0
    for h in hs:
        up_chunk(h, slice(row0, row0 + h.shape[0]), 0)
        row0 += h.shape[0]
    _advance(side, side_per_tick)
    h = jnp.concatenate(hs, axis=0)
    for c in range(1, len(bounds) - 1):
        up_chunk(h, slice(0, row0), c)
        _advance(side, side_per_tick)
    x = jnp.concatenate(x_slabs, axis=0)
    out = []
    for j in range(D_MODEL // MXU_COLS):
        cols = slice(j * MXU_COLS, (j + 1) * MXU_COLS)
        y = jnp.dot(act_ref[...], wd_ref[:, cols], preferred_element_type=F32)
        out.append(x[:, cols] + 0.5 * y)
        _advance(side, side_per_tick)
    return jnp.concatenate(out, axis=1)


def _rope_pair(x, cos, sin_signed, first_half):
    swapped = jnp.where(first_half, pltpu.roll(x, LANE - HEAD_DIM // 2, 1), pltpu.roll(x, HEAD_DIM // 2, 1))
    return x * cos + swapped * sin_signed


GLA_CHUNKS_PER_BLOCK = GLA_BLOCK // GLA_CHUNK
GLA_BLOCKS_PER_TILE = ROW_TILE // GLA_BLOCK
GLA_STAGES_PER_TILE = GLA_BLOCKS_PER_TILE * (4 + 2 * GLA_CHUNKS_PER_BLOCK)


def _round_robin(*stage_generators):
    live = list(stage_generators)
    while live:
        for gen in list(live):
            try:
                next(gen)
                yield
            except StopIteration:
                live.remove(gen)


def _gla_block_stages(q_ref, k_ref, v_ref, r_ref, brows, wdec_ref, bdec_ref, tri_ref, s_ref, reverse, emit):
    bt = brows.stop - brows.start
    nch = bt // GLA_CHUNK
    z = jnp.dot(r_ref[brows, :], wdec_ref[...], preferred_element_type=F32) + bdec_ref[...]
    yield
    log_a = (jnp.minimum(z, 0.0) - jnp.log1p(jnp.exp(-jnp.abs(z)))) * (1.0 / GLA_GATE_NORMALIZER)
    hi = log_a.astype(BF16)
    lo = (log_a - hi.astype(F32)).astype(BF16)
    tri = tri_ref[...]
    cum = jnp.dot(tri, hi, preferred_element_type=F32) + jnp.dot(tri, lo, preferred_element_type=F32)
    yield
    q = q_ref[brows, :]
    k = k_ref[brows, :]
    cum3 = cum.reshape(nch, GLA_CHUNK, GLA_QK)
    edge = GLA_CHUNK - 1 if not reverse else 0
    tot3 = cum3[:, edge:edge + 1, :]
    rest = (tot3 - cum3).reshape(bt, GLA_QK)
    qe = (q * (GLA_DK ** -0.5) * jnp.exp(cum)).astype(BF16)
    ke = (k * jnp.exp(-cum)).astype(BF16)
    ks_t = (k * jnp.exp(rest)).T.astype(BF16)
    tot = tot3.reshape(nch, GLA_QK)
    tot_t = jnp.concatenate([tot, jnp.zeros((LANE - nch, GLA_QK), F32)], axis=0).T
    decay_t = jnp.exp(tot_t)

    ii = lax.broadcasted_iota(jnp.int32, (GLA_CHUNK, GLA_CHUNK), 0)
    jj = lax.broadcasted_iota(jnp.int32, (GLA_CHUNK, GLA_CHUNK), 1)
    keep = (jj > ii) if reverse else (jj <= ii)
    rows = [slice(n * GLA_CHUNK, (n + 1) * GLA_CHUNK) for n in range(nch)]
    klanes = [slice(h * GLA_DK, (h + 1) * GLA_DK) for h in range(N_GLA_HEADS)]

    def v_of(n, h):
        return v_ref[brows.start + n * GLA_CHUNK:brows.start + (n + 1) * GLA_CHUNK, h * GLA_DV:(h + 1) * GLA_DV]

    yield

    a = [[None] * N_GLA_HEADS for _ in range(nch)]
    u = [None] * nch
    for n in range(nch):
        for h in range(N_GLA_HEADS):
            s_nh = lax.dot_general(qe[rows[n], klanes[h]], ke[rows[n], klanes[h]], (((1,), (1,)), ((), ())),
                                   preferred_element_type=F32)
            a[n][h] = jnp.where(keep, s_nh, 0.0).astype(BF16)
        u[n] = jnp.concatenate([jnp.dot(ks_t[klanes[h], rows[n]], v_of(n, h), preferred_element_type=F32)
                                for h in range(N_GLA_HEADS)], axis=0)
        yield
    s = s_ref[...]
    s_in = [None] * nch
    for n in (range(nch - 1, -1, -1) if reverse else range(nch)):
        s_in[n] = s.astype(BF16)
        s = decay_t[:, n:n + 1] * s + u[n]
    s_ref[...] = s
    yield
    for n in range(nch):
        for h in range(N_GLA_HEADS):
            emit(n, h, jnp.dot(a[n][h], v_of(n, h), preferred_element_type=F32)
                 + jnp.dot(qe[rows[n], klanes[h]], s_in[n][klanes[h], :], preferred_element_type=F32))
        yield


def _gla_tile_streams(q_ref, k_ref, v_ref, r_ref, wdec_ref, bdec_ref, tri_ref, s_ref, reverse, emit):
    streams = []
    for blk in (range(GLA_BLOCKS_PER_TILE - 1, -1, -1) if reverse else range(GLA_BLOCKS_PER_TILE)):
        brows = slice(blk * GLA_BLOCK, (blk + 1) * GLA_BLOCK)
        emit_block = lambda n, h, o, base=blk * GLA_BLOCK: emit(base + n * GLA_CHUNK, h, o)
        streams.append(_gla_block_stages(q_ref, k_ref, v_ref, r_ref, brows, wdec_ref, bdec_ref, tri_ref, s_ref,
                                         reverse, emit_block))
    return streams


def _reverse_sweep_kernel(x_ref, n1_ref, wg_ref, wu_ref, wd_ref, nm_ref, win_ref, cos_ref, sin_ref,
                          wdec_ref, bdec_ref, tri_ref,
                          x1_ref, qa_ref, ka_ref, va_ref, gq_ref, gk_ref, gv_ref, gg_ref, r_ref, ob_ref,
                          act_ref, pq_ref, pk_ref, pv_ref, pr_ref, s_ref, *, tiles_per_seq):
    g = pl.program_id(0)
    last = pl.num_programs(0) - 1
    cur_slot = g % 2
    prev_slot = 1 - cur_slot

    @pl.when(jnp.maximum(g - 1, 0) % tiles_per_seq == 0)
    def _():
        s_ref[...] = jnp.zeros_like(s_ref)

    def emit_ob(row0, h, o):
        ob_ref[row0:row0 + GLA_CHUNK, h * GLA_DV:(h + 1) * GLA_DV] = o

    def ffn1_inproj(side):
        x_slabs = [x_ref[r0:r0 + ROW_SLAB, :] for r0 in range(0, ROW_TILE, ROW_SLAB)]
        side_per_tick = -(-(GLA_STAGES_PER_TILE - IN_PROJ_DOTS) // _ffn_ticks(FF_CHUNK_FFN1))
        x1 = _swiglu_residual(x_slabs, n1_ref, wg_ref, wu_ref, wd_ref, act_ref, FF_CHUNK_FFN1, side, side_per_tick)
        x1_ref[...] = x1
        h = jnp.concatenate([_rms(x1[r0:r0 + ROW_SLAB, :], nm_ref[...]).astype(BF16)
                             for r0 in range(0, ROW_TILE, ROW_SLAB)], axis=0)
        cos = cos_ref[...]
        sin = sin_ref[...]
        lane = lax.broadcasted_iota(jnp.int32, (1, LANE), 1)
        first_half = (lane % HEAD_DIM) < (HEAD_DIM // 2)

        def proj(off, width):
            _advance(side, 1)
            return jnp.dot(h, win_ref[:, off:off + width], preferred_element_type=F32)

        scale = HEAD_DIM ** -0.5
        for j in range(ATTN_Q // IN_PROJ_COLS):
            q2 = proj(OFF_AQ + j * IN_PROJ_COLS, IN_PROJ_COLS)
            for i in range(IN_PROJ_COLS // LANE):
                q = _rope_pair(q2[:, i * LANE:(i + 1) * LANE], cos, sin, first_half)
                qa_ref[:, j * IN_PROJ_COLS + i * LANE:j * IN_PROJ_COLS + (i + 1) * LANE] = (q * scale).astype(BF16)
        kv = proj(OFF_AK, 2 * ATTN_KV)
        ka_ref[...] = _rope_pair(kv[:, :ATTN_KV], cos, sin, first_half).astype(BF16)
        va_ref[...] = kv[:, ATTN_KV:].T.astype(BF16)
        gq = proj(OFF_GQ, GLA_QK)
        gq_ref[...] = gq
        pq_ref[cur_slot] = gq
        gk = proj(OFF_GK, GLA_QK)
        gk_ref[...] = gk
        pk_ref[cur_slot] = gk
        for j in range(GLA_V // IN_PROJ_COLS):
            cols = slice(j * IN_PROJ_COLS, (j + 1) * IN_PROJ_COLS)
            gv = proj(OFF_GV + j * IN_PROJ_COLS, IN_PROJ_COLS).astype(BF16)
            gv_ref[:, cols] = gv
            pv_ref[cur_slot, :, cols] = gv
            gg_ref[:, cols] = proj(OFF_GG + j * IN_PROJ_COLS, IN_PROJ_COLS)
        rr = proj(OFF_R, 2 * GLA_RANK).astype(BF16)
        r_ref[...] = rr
        pr_ref[cur_slot] = rr

    def step(with_ffn, with_scan):
        side = iter(())
        if with_scan:
            side = _round_robin(*_gla_tile_streams(pq_ref.at[prev_slot], pk_ref.at[prev_slot], pv_ref.at[prev_slot],
                                                   pr_ref.at[prev_slot], wdec_ref, bdec_ref, tri_ref, s_ref, True,
                                                   emit_ob))
        if with_ffn:
            ffn1_inproj(side)
        for _ in side:
            pass

    @pl.when(g == 0)
    def _():
        pq_ref[...] = jnp.zeros_like(pq_ref)
        pk_ref[...] = jnp.zeros_like(pk_ref)
        pv_ref[...] = jnp.zeros_like(pv_ref)
        pr_ref[...] = jnp.zeros_like(pr_ref)

    pl.when(g < last)(lambda: step(True, True))
    pl.when(g == last)(lambda: step(False, True))


def _const_spec(shape):
    return pl.BlockSpec(shape, lambda *_: (0,) * len(shape), pipeline_mode=pl.Buffered(1))


def _reverse_sweep(x2d, seq_len, n1, wg, wu, wd, nm, win, cos_tab, sin_tab, wdec, bdec, tri_up):
    n_rows = x2d.shape[0]
    tm = ROW_TILE
    assert n_rows % tm == 0 and seq_len % tm == 0
    nt = seq_len // tm
    n_tiles = n_rows // tm

    def tile_of(step):
        return (step // nt) * nt + (nt - 1 - step % nt)

    cur = lambda g: tile_of(jnp.minimum(g, n_tiles - 1))
    lag = lambda g: tile_of(jnp.maximum(g - 1, 0))
    row = lambda w: pl.BlockSpec((tm, w), lambda g: (cur(g), 0))
    rope = pl.BlockSpec((tm, LANE), lambda g: (nt - 1 - jnp.minimum(g, n_tiles - 1) % nt, 0))
    out_shapes = (
        jax.ShapeDtypeStruct((n_rows, D_MODEL), F32),
        jax.ShapeDtypeStruct((n_rows, ATTN_Q), BF16),
        jax.ShapeDtypeStruct((n_rows, ATTN_KV), BF16),
        jax.ShapeDtypeStruct((ATTN_KV, n_rows), BF16),
        jax.ShapeDtypeStruct((n_rows, GLA_QK), F32),
        jax.ShapeDtypeStruct((n_rows, GLA_QK), F32),
        jax.ShapeDtypeStruct((n_rows, GLA_V), BF16),
        jax.ShapeDtypeStruct((n_rows, GLA_V), F32),
        jax.ShapeDtypeStruct((n_rows, 2 * GLA_RANK), BF16),
        jax.ShapeDtypeStruct((n_rows, GLA_V), F32),
    )
    return pl.pallas_call(
        functools.partial(_reverse_sweep_kernel, tiles_per_seq=nt),
        grid=(n_tiles + 1,),
        in_specs=[
            row(D_MODEL),
            _const_spec((1, D_MODEL)),
            _const_spec((D_MODEL, D_FF)), _const_spec((D_MODEL, D_FF)), _const_spec((D_FF, D_MODEL)),
            _const_spec((1, D_MODEL)),
            _const_spec((D_MODEL, IN_PROJ_WIDTH)),
            rope, rope,
            _const_spec((2 * GLA_RANK, GLA_QK)), _const_spec((1, GLA_QK)), _const_spec((GLA_BLOCK, GLA_BLOCK)),
        ],
        out_specs=[row(D_MODEL), row(ATTN_Q), row(ATTN_KV), pl.BlockSpec((ATTN_KV, tm), lambda g: (0, cur(g))),
                   row(GLA_QK), row(GLA_QK),
                   row(GLA_V), row(GLA_V), row(2 * GLA_RANK),
                   pl.BlockSpec((tm, GLA_V), lambda g: (lag(g), 0))],
        out_shape=out_shapes,
        scratch_shapes=[pltpu.VMEM((tm, D_FF), BF16),
                        pltpu.VMEM((2, tm, GLA_QK), F32), pltpu.VMEM((2, tm, GLA_QK), F32),
                        pltpu.VMEM((2, tm, GLA_V), BF16), pltpu.VMEM((2, tm, 2 * GLA_RANK), BF16),
                        pltpu.VMEM((GLA_QK, GLA_DV), F32)],
        compiler_params=pltpu.CompilerParams(dimension_semantics=("arbitrary",), vmem_limit_bytes=VMEM_LIMIT),
        name="reverse_sweep",
    )(x2d, n1, wg, wu, wd, nm, win, cos_tab, sin_tab, wdec, bdec, tri_up)


ATTN_UNITS_PER_TILE = (ROW_TILE // ATTN_BLOCK) * N_KV_HEADS
ATTN_STAGES_PER_TILE = 3 * ATTN_UNITS_PER_TILE


def _attention_tile_stages(sink_ref, qa_ref, kp_ref, kc_ref, kn_ref, vp_ref, vc_ref, vn_ref, mix_ref, tpos,
                           tiles_per_seq):
    sub = ROW_TILE // ATTN_BLOCK
    n_qblocks = tiles_per_seq * sub
    n_keys = 3 * ATTN_BLOCK
    n_cols = ATTN_GROUP * ATTN_BLOCK
    kbuf = jnp.concatenate([kp_ref[...], kc_ref[...], kn_ref[...]], axis=0)
    vbuf_t = jnp.concatenate([vp_ref[...], vc_ref[...], vn_ref[...]], axis=1)
    kj = lax.broadcasted_iota(jnp.int32, (n_keys, n_cols), 0)
    col = lax.broadcasted_iota(jnp.int32, (n_keys, n_cols), 1)
    qi = col % ATTN_BLOCK
    in_window = (kj >= qi) & (kj <= qi + 2 * ATTN_BLOCK)
    head_of_col = lax.broadcasted_iota(jnp.int32, (1, n_cols), 1) // ATTN_BLOCK
    pending = {}

    def unit(jb, kv):
        qblk = tpos * sub + jb
        qrows = slice(jb * ATTN_BLOCK, (jb + 1) * ATTN_BLOCK)
        krows = slice(jb * ATTN_BLOCK, (jb + 3) * ATTN_BLOCK)
        kvl = slice(kv * HEAD_DIM, (kv + 1) * HEAD_DIM)
        heads = range(kv * ATTN_GROUP, (kv + 1) * ATTN_GROUP)
        qs = jnp.concatenate([qa_ref[qrows, h * HEAD_DIM:(h + 1) * HEAD_DIM] for h in heads], axis=0)
        s_t = lax.dot_general(kbuf[krows, kvl], qs, (((1,), (1,)), ((), ())), preferred_element_type=F32)
        yield
        mask = in_window
        if jb == 0:
            mask = mask & ((kj >= ATTN_BLOCK) | (qblk > 0))
        if jb == sub - 1:
            mask = mask & ((kj < 2 * ATTN_BLOCK) | (qblk < n_qblocks - 1))
        s_t = jnp.where(mask, s_t, -1e30)
        sink = jnp.full((1, n_cols), sink_ref[heads[-1]], F32)
        for hl in range(ATTN_GROUP - 2, -1, -1):
            sink = jnp.where(head_of_col == hl, sink_ref[heads[hl]], sink)
        m = jnp.maximum(jnp.max(s_t, axis=0, keepdims=True), sink)
        p = jnp.exp(s_t - m)
        denom = jnp.sum(p, axis=0, keepdims=True) + jnp.exp(sink - m)
        p_t = p.astype(BF16)
        yield
        pending[kv] = jnp.dot(vbuf_t[kvl, krows], p_t, preferred_element_type=F32) / denom
        if kv == N_KV_HEADS - 1:
            o_t = jnp.concatenate([pending.pop(i) for i in range(N_KV_HEADS)], axis=0)
            for hl in range(ATTN_GROUP):
                cols = slice(hl * ATTN_BLOCK, (hl + 1) * ATTN_BLOCK)
                mix_ref[qrows, hl * ATTN_KV:(hl + 1) * ATTN_KV] = o_t[:, cols].T.astype(BF16)
        yield

    units = [unit(jb, kv) for jb in range(sub) for kv in range(N_KV_HEADS)]
    for slot in range(len(units) + 4):
        for stage in range(3):
            u = slot - 2 * stage
            if 0 <= u < len(units):
                next(units[u])
                yield


def _forward_sweep_kernel(sink_ref, qa_ref, kp_ref, kc_ref, kn_ref, vp_ref, vc_ref, vn_ref,
                          gq_ref, gk_ref, gv_ref, gg_ref, r_ref, ob_ref, wdec_ref, bdec_ref, tri_ref, gnorm_ref,
                          x1_ref, wout_ref, n2_ref, wg_ref, wu_ref, wd_ref, nf_ref,
                          y_ref, act_ref, mix_ref, s_ref, *, tiles_per_seq):
    g = pl.program_id(0)
    mix_cur = mix_ref.at[g % 2]
    mix_prev = mix_ref.at[1 - g % 2]

    @pl.when(g % tiles_per_seq == 0)
    def _():
        s_ref[...] = jnp.zeros_like(s_ref)

    tpos = jnp.minimum(g, pl.num_programs(0) - 2) % tiles_per_seq
    gain = gnorm_ref[...]

    def emit_mix(row0, h, o):
        rows = slice(row0, row0 + GLA_CHUNK)
        cols = slice(h * GLA_DV, (h + 1) * GLA_DV)
        o = _rms(o + ob_ref[rows, cols], gain) * _silu(gg_ref[rows, cols])
        mix_cur[rows, ATTN_Q + h * GLA_DV:ATTN_Q + (h + 1) * GLA_DV] = o.astype(BF16)

    def outproj_ffn2(side):
        side_per_tick = -(-(ATTN_STAGES_PER_TILE + GLA_STAGES_PER_TILE) // _ffn_ticks(FF_CHUNK_FFN2))
        x2_slabs = [x1_ref[r0:r0 + ROW_SLAB, :]
                    + jnp.dot(mix_prev[r0:r0 + ROW_SLAB, :], wout_ref[...], preferred_element_type=F32)
                    for r0 in range(0, ROW_TILE, ROW_SLAB)]
        x3 = _swiglu_residual(x2_slabs, n2_ref, wg_ref, wu_ref, wd_ref, act_ref, FF_CHUNK_FFN2, side, side_per_tick)
        for r0 in range(0, ROW_TILE, ROW_SLAB):
            y_ref[r0:r0 + ROW_SLAB, :] = _rms(x3[r0:r0 + ROW_SLAB, :], nf_ref[...])

    def step(with_mixer, with_ffn):
        side = iter(())
        if with_mixer:
            side = _round_robin(
                _attention_tile_stages(sink_ref, qa_ref, kp_ref, kc_ref, kn_ref, vp_ref, vc_ref, vn_ref, mix_cur,
                                       tpos, tiles_per_seq),
                *_gla_tile_streams(gq_ref, gk_ref, gv_ref, r_ref, wdec_ref, bdec_ref, tri_ref, s_ref, False,
                                   emit_mix))
        if with_ffn:
            outproj_ffn2(side)
        for _ in side:
            pass

    pl.when(g == 0)(lambda: step(True, False))
    pl.when(g > 0)(lambda: step(True, True))


def _forward_sweep(seq_len, sink, qa, ka, va, gq, gk, gv, gg, r, ob, wdec, bdec, tri_lo, gnorm,
                   x1, wout, n2, wg, wu, wd, nf):
    n_rows = x1.shape[0]
    tm = ROW_TILE
    nt = seq_len // tm
    n_tiles = n_rows // tm
    sub = tm // ATTN_BLOCK
    halo_per_seq = seq_len // ATTN_BLOCK
    cur = lambda g: jnp.minimum(g, n_tiles - 1)
    lag = lambda g: jnp.maximum(g - 1, 0)
    row = lambda w: pl.BlockSpec((tm, w), lambda g: (cur(g), 0))
    lag_row = lambda w: pl.BlockSpec((tm, w), lambda g: (lag(g), 0))

    def prev_idx(g):
        t = cur(g)
        return jnp.maximum(t * sub - 1, (t // nt) * halo_per_seq)

    def next_idx(g):
        t = cur(g)
        return jnp.minimum((t + 1) * sub, (t // nt + 1) * halo_per_seq - 1)

    prev = pl.BlockSpec((ATTN_BLOCK, ATTN_KV), lambda g: (prev_idx(g), 0))
    nxt = pl.BlockSpec((ATTN_BLOCK, ATTN_KV), lambda g: (next_idx(g), 0))
    prev_t = pl.BlockSpec((ATTN_KV, ATTN_BLOCK), lambda g: (0, prev_idx(g)))
    cur_t = pl.BlockSpec((ATTN_KV, tm), lambda g: (0, cur(g)))
    nxt_t = pl.BlockSpec((ATTN_KV, ATTN_BLOCK), lambda g: (0, next_idx(g)))
    return pl.pallas_call(
        functools.partial(_forward_sweep_kernel, tiles_per_seq=nt),
        grid=(n_tiles + 1,),
        in_specs=[pl.BlockSpec(memory_space=pltpu.SMEM),
                  row(ATTN_Q), prev, row(ATTN_KV), nxt, prev_t, cur_t, nxt_t,
                  row(GLA_QK), row(GLA_QK), row(GLA_V), row(GLA_V), row(2 * GLA_RANK), row(GLA_V),
                  _const_spec((2 * GLA_RANK, GLA_QK)), _const_spec((1, GLA_QK)), _const_spec((GLA_BLOCK, GLA_BLOCK)),
                  _const_spec((1, GLA_DV)),
                  lag_row(D_MODEL), _const_spec((D_MODEL, D_MODEL)), _const_spec((1, D_MODEL)),
                  _const_spec((D_MODEL, D_FF)), _const_spec((D_MODEL, D_FF)), _const_spec((D_FF, D_MODEL)),
                  _const_spec((1, D_MODEL))],
        out_specs=lag_row(D_MODEL),
        out_shape=jax.ShapeDtypeStruct((n_rows, D_MODEL), F32),
        scratch_shapes=[pltpu.VMEM((tm, D_FF), BF16), pltpu.VMEM((2, tm, D_MODEL), BF16),
                        pltpu.VMEM((GLA_QK, GLA_DV), F32)],
        compiler_params=pltpu.CompilerParams(dimension_semantics=("arbitrary",), vmem_limit_bytes=VMEM_LIMIT),
        name="forward_sweep",
    )(sink, qa, ka, ka, ka, va, va, va, gq, gk, gv, gg, r, ob, wdec, bdec, tri_lo, gnorm,
      x1, wout, n2, wg, wu, wd, nf)


def _rope_tables(seq_len):
    half = HEAD_DIM // 2
    inv_freq = (np.float32(ROPE_THETA) ** (-np.arange(half, dtype=np.float32) / np.float32(half))).astype(np.float32)
    ang = np.arange(seq_len, dtype=np.float32)[:, None] * inv_freq[None, :]
    cos, sin = np.cos(ang), np.sin(ang)
    cos_tab = np.tile(cos, (1, LANE // half))
    sin_tab = np.tile(np.concatenate([-sin, sin], axis=1), (1, LANE // HEAD_DIM))
    return jnp.asarray(cos_tab, dtype=F32), jnp.asarray(sin_tab, dtype=F32)


def _chunk_tri(bt, upper):
    i = np.arange(bt)[:, None]
    j = np.arange(bt)[None, :]
    same_chunk = (i // GLA_CHUNK) == (j // GLA_CHUNK)
    keep = (j >= i) if upper else (j <= i)
    return jnp.asarray(same_chunk & keep, dtype=BF16)


def _trunk(x, p):
    batch, seq_len, _ = x.shape
    cos_tab, sin_tab = _rope_tables(seq_len)
    x2d = x.reshape(batch * seq_len, D_MODEL)
    x1, qa, ka, va, gq, gk, gv, gg, r, ob = _reverse_sweep(
        x2d, seq_len, p["n1"], p["wg1"], p["wu1"], p["wd1"], p["nm"], p["win"], cos_tab, sin_tab,
        p["wdec_b"], p["bdec_b"], _chunk_tri(GLA_BLOCK, True))
    y = _forward_sweep(seq_len, p["sink"], qa, ka, va, gq, gk, gv, gg, r, ob,
                       p["wdec_f"], p["bdec_f"], _chunk_tri(GLA_BLOCK, False), p["gnorm"],
                       x1, p["wout"], p["n2"], p["wg2"], p["wu2"], p["wd2"], p["nf"])
    return y.reshape(batch, seq_len, D_MODEL)


def kernel(x_prompt, x_sample, norm_ffn1, w_ffn1_gate, w_ffn1_up, w_ffn1_down, norm_mix, w_in, attn_sink, w_gla_decay_fwd, b_gla_decay_fwd, w_gla_decay_bwd, b_gla_decay_bwd, gla_out_norm, w_out, norm_ffn2, w_ffn2_gate, w_ffn2_up, w_ffn2_down, norm_final):
    assert norm_ffn1.shape[0] == 1, "single-layer trunk"
    zeros_rank = jnp.zeros((GLA_RANK, GLA_QK), F32)
    p = dict(
        n1=norm_ffn1[0][None, :], wg1=w_ffn1_gate[0].astype(BF16), wu1=w_ffn1_up[0].astype(BF16),
        wd1=w_ffn1_down[0].astype(BF16),
        nm=norm_mix[0][None, :],
        win=w_in[0].astype(BF16),
        sink=attn_sink[0],
        wdec_f=jnp.concatenate([w_gla_decay_fwd[0], zeros_rank], axis=0).astype(BF16),
        bdec_f=b_gla_decay_fwd[0][None, :],
        wdec_b=jnp.concatenate([zeros_rank, w_gla_decay_bwd[0]], axis=0).astype(BF16),
        bdec_b=b_gla_decay_bwd[0][None, :],
        gnorm=gla_out_norm[0][None, :],
        wout=jnp.concatenate([
            w_out[0][:ATTN_Q].reshape(N_KV_HEADS, ATTN_GROUP, HEAD_DIM, D_MODEL).transpose(1, 0, 2, 3)
            .reshape(ATTN_Q, D_MODEL), w_out[0][ATTN_Q:]], axis=0).astype(BF16),
        n2=norm_ffn2[0][None, :], wg2=w_ffn2_gate[0].astype(BF16), wu2=w_ffn2_up[0].astype(BF16),
        wd2=w_ffn2_down[0].astype(BF16),
        nf=norm_final[None, :],
    )
    return _trunk(x_prompt, p), _trunk(x_sample, p)
```

```python
import functools

import jax
import jax.numpy as jnp
import numpy as np
from jax import lax
from jax.experimental import pallas as pl
from jax.experimental.pallas import tpu as pltpu

F32 = jnp.float32
BF16 = jnp.bfloat16

D_MODEL = 1024
D_FF = 2816
EPS = 1e-6
N_ATTN_HEADS = 8
N_KV_HEADS = 2
ATTN_GROUP = N_ATTN_HEADS // N_KV_HEADS
HEAD_DIM = 64
ATTN_BLOCK = 128
ROPE_THETA = 10000.0
N_GLA_HEADS = 4
GLA_DK = 64
GLA_DV = 128
GLA_RANK = 16
GLA_GATE_NORMALIZER = 16.0
GLA_CHUNK = 64
ATTN_Q = N_ATTN_HEADS * HEAD_DIM
ATTN_KV = N_KV_HEADS * HEAD_DIM
GLA_QK = N_GLA_HEADS * GLA_DK
GLA_V = N_GLA_HEADS * GLA_DV
IN_PROJ_WIDTH = ATTN_Q + 2 * ATTN_KV + 2 * GLA_QK + 2 * GLA_V + 2 * GLA_RANK
LANE = 128
MXU_COLS = 256
OFF_AQ = 0
OFF_AK = OFF_AQ + ATTN_Q
OFF_AV = OFF_AK + ATTN_KV
OFF_GQ = OFF_AV + ATTN_KV
OFF_GK = OFF_GQ + GLA_QK
OFF_GV = OFF_GK + GLA_QK
OFF_GG = OFF_GV + GLA_V
OFF_R = OFF_GG + GLA_V

ROW_TILE = 512
ROW_SLAB = 256
FF_CHUNK_FFN1 = 768
FF_CHUNK_FFN2 = 256
IN_PROJ_COLS = MXU_COLS
IN_PROJ_DOTS = ATTN_Q // IN_PROJ_COLS + 3 + 2 * (GLA_V // IN_PROJ_COLS) + 1
GLA_BLOCK = 256
VMEM_LIMIT = 56 * 1024 * 1024


def _rms(x, gain):
    return x * lax.rsqrt(jnp.mean(x * x, axis=-1, keepdims=True) + EPS) * gain


def _silu(x):
    return x * (1.0 / (1.0 + jnp.exp(-x)))


def _advance(side, n):
    for _ in range(n):
        next(side, None)


def _ffn_ticks(ff_chunk):
    return 1 + -(-D_FF // ff_chunk) + D_MODEL // MXU_COLS


def _swiglu_residual(x_slabs, gain_ref, wg_ref, wu_ref, wd_ref, act_ref, ff_chunk, side, side_per_tick):
    bounds = list(range(0, D_FF, ff_chunk)) + [D_FF]
    _advance(side, side_per_tick)
    gain = gain_ref[...]
    hs = [_rms(x, gain).astype(BF16) for x in x_slabs]

    def up_chunk(h, rows, c):
        sl = slice(bounds[c], bounds[c + 1])
        g = jnp.dot(h, wg_ref[:, sl], preferred_element_type=F32)
        u = jnp.dot(h, wu_ref[:, sl], preferred_element_type=F32)
        act_ref[rows, sl] = (_silu(g) * u).astype(BF16)

    row0 = 0
    for h in hs:
        up_chunk(h, slice(row0, row0 + h.shape[0]), 0)
        row0 += h.shape[0]
    _advance(side, side_per_tick)
    h = jnp.concatenate(hs, axis=0)
    for c in range(1, len(bounds) - 1):
        up_chunk(h, slice(0, row0), c)
        _advance(side, side_per_tick)
    x = jnp.concatenate(x_slabs, axis=0)
    out = []
    for j in range(D_MODEL // MXU_COLS):
        cols = slice(j * MXU_COLS, (j + 1) * MXU_COLS)
        y = jnp.dot(act_ref[...], wd_ref[:, cols], preferred_element_type=F32)
        out.append(x[:, cols] + 0.5 * y)
        _advance(side, side_per_tick)
    return jnp.concatenate(out, axis=1)


def _rope_pair(x, cos, sin_signed, first_half):
    swapped = jnp.where(first_half, pltpu.roll(x, LANE - HEAD_DIM // 2, 1), pltpu.roll(x, HEAD_DIM // 2, 1))
    return x * cos + swapped * sin_signed


GLA_CHUNKS_PER_BLOCK = GLA_BLOCK // GLA_CHUNK
GLA_BLOCKS_PER_TILE = ROW_TILE // GLA_BLOCK
GLA_STAGES_PER_TILE = GLA_BLOCKS_PER_TILE * (4 + 2 * GLA_CHUNKS_PER_BLOCK)


def _round_robin(*stage_generators):
    live = list(stage_generators)
    while live:
        for gen in list(live):
            try:
                next(gen)
                yield
            except StopIteration:
                live.remove(gen)


def _gla_block_stages(q_ref, k_ref, v_ref, r_ref, brows, wdec_ref, bdec_ref, tri_ref, s_ref, reverse, emit):
    bt = brows.stop - brows.start
    nch = bt // GLA_CHUNK
    z = jnp.dot(r_ref[brows, :], wdec_ref[...], preferred_element_type=F32) + bdec_ref[...]
    yield
    log_a = (jnp.minimum(z, 0.0) - jnp.log1p(jnp.exp(-jnp.abs(z)))) * (1.0 / GLA_GATE_NORMALIZER)
    hi = log_a.astype(BF16)
    lo = (log_a - hi.astype(F32)).astype(BF16)
    tri = tri_ref[...]
    cum = jnp.dot(tri, hi, preferred_element_type=F32) + jnp.dot(tri, lo, preferred_element_type=F32)
    yield
    q = q_ref[brows, :].astype(F32)
    k = k_ref[brows, :].astype(F32)
    cum3 = cum.reshape(nch, GLA_CHUNK, GLA_QK)
    edge = GLA_CHUNK - 1 if not reverse else 0
    tot3 = cum3[:, edge:edge + 1, :]
    rest = (tot3 - cum3).reshape(bt, GLA_QK)
    qe = (q * (GLA_DK ** -0.5) * jnp.exp(cum)).astype(BF16)
    ke = (k * jnp.exp(-cum)).astype(BF16)
    ks_t = (k * jnp.exp(rest)).T.astype(BF16)
    tot = tot3.reshape(nch, GLA_QK)
    tot_t = jnp.concatenate([tot, jnp.zeros((LANE - nch, GLA_QK), F32)], axis=0).T
    decay_t = jnp.exp(tot_t)

    ii = lax.broadcasted_iota(jnp.int32, (GLA_CHUNK, GLA_CHUNK), 0)
    jj = lax.broadcasted_iota(jnp.int32, (GLA_CHUNK, GLA_CHUNK), 1)
    keep = (jj > ii) if reverse else (jj <= ii)
    rows = [slice(n * GLA_CHUNK, (n + 1) * GLA_CHUNK) for n in range(nch)]
    klanes = [slice(h * GLA_DK, (h + 1) * GLA_DK) for h in range(N_GLA_HEADS)]

    def v_of(n, h):
        return v_ref[brows.start + n * GLA_CHUNK:brows.start + (n + 1) * GLA_CHUNK, h * GLA_DV:(h + 1) * GLA_DV]

    yield

    a = [[None] * N_GLA_HEADS for _ in range(nch)]
    u = [None] * nch
    for n in range(nch):
        for h in range(N_GLA_HEADS):
            s_nh = lax.dot_general(qe[rows[n], klanes[h]], ke[rows[n], klanes[h]], (((1,), (1,)), ((), ())),
                                   preferred_element_type=F32)
            a[n][h] = jnp.where(keep, s_nh, 0.0).astype(BF16)
        u[n] = jnp.concatenate([jnp.dot(ks_t[klanes[h], rows[n]], v_of(n, h), preferred_element_type=F32)
                                for h in range(N_GLA_HEADS)], axis=0)
        yield
    s = s_ref[...]
    s_in = [None] * nch
    for n in (range(nch - 1, -1, -1) if reverse else range(nch)):
        s_in[n] = s.astype(BF16)
        s = decay_t[:, n:n + 1] * s + u[n]
    s_ref[...] = s
    yield
    for n in range(nch):
        for h in range(N_GLA_HEADS):
            emit(n, h, jnp.dot(a[n][h], v_of(n, h), preferred_element_type=F32)
                 + jnp.dot(qe[rows[n], klanes[h]], s_in[n][klanes[h], :], preferred_element_type=F32))
        yield


def _gla_tile_streams(q_ref, k_ref, v_ref, r_ref, wdec_ref, bdec_ref, tri_ref, s_ref, reverse, emit):
    streams = []
    for blk in (range(GLA_BLOCKS_PER_TILE - 1, -1, -1) if reverse else range(GLA_BLOCKS_PER_TILE)):
        brows = slice(blk * GLA_BLOCK, (blk + 1) * GLA_BLOCK)
        emit_block = lambda n, h, o, base=blk * GLA_BLOCK: emit(base + n * GLA_CHUNK, h, o)
        streams.append(_gla_block_stages(q_ref, k_ref, v_ref, r_ref, brows, wdec_ref, bdec_ref, tri_ref, s_ref,
                                         reverse, emit_block))
    return streams


def _reverse_sweep_kernel(x_ref, n1_ref, wg_ref, wu_ref, wd_ref, nm_ref, win_ref, cos_ref, sin_ref,
                          wdec_ref, bdec_ref, tri_ref,
                          x1_ref, qa_ref, ka_ref, va_ref, gq_ref, gk_ref, gv_ref, gg_ref, r_ref, ob_ref,
                          act_ref, pq_ref, pk_ref, pv_ref, pr_ref, s_ref, *, tiles_per_seq):
    g = pl.program_id(0)
    last = pl.num_programs(0) - 1
    cur_slot = g % 2
    prev_slot = 1 - cur_slot

    @pl.when(jnp.maximum(g - 1, 0) % tiles_per_seq == 0)
    def _():
        s_ref[...] = jnp.zeros_like(s_ref)

    def emit_ob(row0, h, o):
        ob_ref[row0:row0 + GLA_CHUNK, h * GLA_DV:(h + 1) * GLA_DV] = o.astype(ob_ref.dtype)

    def ffn1_inproj(side):
        x_slabs = [x_ref[r0:r0 + ROW_SLAB, :] for r0 in range(0, ROW_TILE, ROW_SLAB)]
        side_per_tick = -(-(GLA_STAGES_PER_TILE - IN_PROJ_DOTS) // _ffn_ticks(FF_CHUNK_FFN1))
        x1 = _swiglu_residual(x_slabs, n1_ref, wg_ref, wu_ref, wd_ref, act_ref, FF_CHUNK_FFN1, side, side_per_tick)
        x1_ref[...] = x1
        h = jnp.concatenate([_rms(x1[r0:r0 + ROW_SLAB, :], nm_ref[...]).astype(BF16)
                             for r0 in range(0, ROW_TILE, ROW_SLAB)], axis=0)
        cos = cos_ref[...]
        sin = sin_ref[...]
        lane = lax.broadcasted_iota(jnp.int32, (1, LANE), 1)
        first_half = (lane % HEAD_DIM) < (HEAD_DIM // 2)

        def proj(off, width):
            _advance(side, 1)
            return jnp.dot(h, win_ref[:, off:off + width], preferred_element_type=F32)

        scale = HEAD_DIM ** -0.5
        for j in range(ATTN_Q // IN_PROJ_COLS):
            q2 = proj(OFF_AQ + j * IN_PROJ_COLS, IN_PROJ_COLS)
            for i in range(IN_PROJ_COLS // LANE):
                q = _rope_pair(q2[:, i * LANE:(i + 1) * LANE], cos, sin, first_half)
                qa_ref[:, j * IN_PROJ_COLS + i * LANE:j * IN_PROJ_COLS + (i + 1) * LANE] = (q * scale).astype(BF16)
        kv = proj(OFF_AK, 2 * ATTN_KV)
        ka_ref[...] = _rope_pair(kv[:, :ATTN_KV], cos, sin, first_half).astype(BF16)
        va_ref[...] = kv[:, ATTN_KV:].T.astype(BF16)
        gq = proj(OFF_GQ, GLA_QK).astype(BF16)
        gq_ref[...] = gq
        pq_ref[cur_slot] = gq
        gk = proj(OFF_GK, GLA_QK).astype(BF16)
        gk_ref[...] = gk
        pk_ref[cur_slot] = gk
        for j in range(GLA_V // IN_PROJ_COLS):
            cols = slice(j * IN_PROJ_COLS, (j + 1) * IN_PROJ_COLS)
            gv = proj(OFF_GV + j * IN_PROJ_COLS, IN_PROJ_COLS).astype(BF16)
            gv_ref[:, cols] = gv
            pv_ref[cur_slot, :, cols] = gv
            gg_ref[:, cols] = proj(OFF_GG + j * IN_PROJ_COLS, IN_PROJ_COLS).astype(BF16)
        rr = proj(OFF_R, 2 * GLA_RANK).astype(BF16)
        r_ref[...] = rr
        pr_ref[cur_slot] = rr

    def step(with_ffn, with_scan):
        side = iter(())
        if with_scan:
            side = _round_robin(*_gla_tile_streams(pq_ref.at[prev_slot], pk_ref.at[prev_slot], pv_ref.at[prev_slot],
                                                   pr_ref.at[prev_slot], wdec_ref, bdec_ref, tri_ref, s_ref, True,
                                                   emit_ob))
        if with_ffn:
            ffn1_inproj(side)
        for _ in side:
            pass

    @pl.when(g == 0)
    def _():
        pq_ref[...] = jnp.zeros_like(pq_ref)
        pk_ref[...] = jnp.zeros_like(pk_ref)
        pv_ref[...] = jnp.zeros_like(pv_ref)
        pr_ref[...] = jnp.zeros_like(pr_ref)

    pl.when(g < last)(lambda: step(True, True))
    pl.when(g == last)(lambda: step(False, True))


def _const_spec(shape):
    return pl.BlockSpec(shape, lambda *_: (0,) * len(shape), pipeline_mode=pl.Buffered(1))


def _reverse_sweep(x2d, seq_len, n1, wg, wu, wd, nm, win, cos_tab, sin_tab, wdec, bdec, tri_up):
    n_rows = x2d.shape[0]
    tm = ROW_TILE
    assert n_rows % tm == 0 and seq_len % tm == 0
    nt = seq_len // tm
    n_tiles = n_rows // tm

    def tile_of(step):
        return (step // nt) * nt + (nt - 1 - step % nt)

    cur = lambda g: tile_of(jnp.minimum(g, n_tiles - 1))
    lag = lambda g: tile_of(jnp.maximum(g - 1, 0))
    row = lambda w: pl.BlockSpec((tm, w), lambda g: (cur(g), 0))
    rope = pl.BlockSpec((tm, LANE), lambda g: (nt - 1 - jnp.minimum(g, n_tiles - 1) % nt, 0))
    out_shapes = (
        jax.ShapeDtypeStruct((n_rows, D_MODEL), F32),
        jax.ShapeDtypeStruct((n_rows, ATTN_Q), BF16),
        jax.ShapeDtypeStruct((n_rows, ATTN_KV), BF16),
        jax.ShapeDtypeStruct((ATTN_KV, n_rows), BF16),
        jax.ShapeDtypeStruct((n_rows, GLA_QK), BF16),
        jax.ShapeDtypeStruct((n_rows, GLA_QK), BF16),
        jax.ShapeDtypeStruct((n_rows, GLA_V), BF16),
        jax.ShapeDtypeStruct((n_rows, GLA_V), BF16),
        jax.ShapeDtypeStruct((n_rows, 2 * GLA_RANK), BF16),
        jax.ShapeDtypeStruct((n_rows, GLA_V), BF16),
    )
    return pl.pallas_call(
        functools.partial(_reverse_sweep_kernel, tiles_per_seq=nt),
        grid=(n_tiles + 1,),
        in_specs=[
            row(D_MODEL),
            _const_spec((1, D_MODEL)),
            _const_spec((D_MODEL, D_FF)), _const_spec((D_MODEL, D_FF)), _const_spec((D_FF, D_MODEL)),
            _const_spec((1, D_MODEL)),
            _const_spec((D_MODEL, IN_PROJ_WIDTH)),
            rope, rope,
            _const_spec((2 * GLA_RANK, GLA_QK)), _const_spec((1, GLA_QK)), _const_spec((GLA_BLOCK, GLA_BLOCK)),
        ],
        out_specs=[row(D_MODEL), row(ATTN_Q), row(ATTN_KV), pl.BlockSpec((ATTN_KV, tm), lambda g: (0, cur(g))),
                   row(GLA_QK), row(GLA_QK),
                   row(GLA_V), row(GLA_V), row(2 * GLA_RANK),
                   pl.BlockSpec((tm, GLA_V), lambda g: (lag(g), 0))],
        out_shape=out_shapes,
        scratch_shapes=[pltpu.VMEM((tm, D_FF), BF16),
                        pltpu.VMEM((2, tm, GLA_QK), BF16), pltpu.VMEM((2, tm, GLA_QK), BF16),
                        pltpu.VMEM((2, tm, GLA_V), BF16), pltpu.VMEM((2, tm, 2 * GLA_RANK), BF16),
                        pltpu.VMEM((GLA_QK, GLA_DV), F32)],
        compiler_params=pltpu.CompilerParams(dimension_semantics=("arbitrary",), vmem_limit_bytes=VMEM_LIMIT),
        name="reverse_sweep",
    )(x2d, n1, wg, wu, wd, nm, win, cos_tab, sin_tab, wdec, bdec, tri_up)


ATTN_UNITS_PER_TILE = (ROW_TILE // ATTN_BLOCK) * N_KV_HEADS
ATTN_STAGES_PER_TILE = 3 * ATTN_UNITS_PER_TILE


def _attention_tile_stages(sink_ref, qa_ref, kp_ref, kc_ref, kn_ref, vp_ref, vc_ref, vn_ref, mix_ref, tpos,
                           tiles_per_seq):
    sub = ROW_TILE // ATTN_BLOCK
    n_qblocks = tiles_per_seq * sub
    n_keys = 3 * ATTN_BLOCK
    n_cols = ATTN_GROUP * ATTN_BLOCK
    kbuf = jnp.concatenate([kp_ref[...], kc_ref[...], kn_ref[...]], axis=0)
    vbuf_t = jnp.concatenate([vp_ref[...], vc_ref[...], vn_ref[...]], axis=1)
    kj = lax.broadcasted_iota(jnp.int32, (n_keys, n_cols), 0)
    col = lax.broadcasted_iota(jnp.int32, (n_keys, n_cols), 1)
    qi = col % ATTN_BLOCK
    in_window = (kj >= qi) & (kj <= qi + 2 * ATTN_BLOCK)
    head_of_col = lax.broadcasted_iota(jnp.int32, (1, n_cols), 1) // ATTN_BLOCK
    pending = {}

    def unit(jb, kv):
        qblk = tpos * sub + jb
        qrows = slice(jb * ATTN_BLOCK, (jb + 1) * ATTN_BLOCK)
        krows = slice(jb * ATTN_BLOCK, (jb + 3) * ATTN_BLOCK)
        kvl = slice(kv * HEAD_DIM, (kv + 1) * HEAD_DIM)
        heads = range(kv * ATTN_GROUP, (kv + 1) * ATTN_GROUP)
        qs = jnp.concatenate([qa_ref[qrows, h * HEAD_DIM:(h + 1) * HEAD_DIM] for h in heads], axis=0)
        s_t = lax.dot_general(kbuf[krows, kvl], qs, (((1,), (1,)), ((), ())), preferred_element_type=F32)
        yield
        mask = in_window
        if jb == 0:
            mask = mask & ((kj >= ATTN_BLOCK) | (qblk > 0))
        if jb == sub - 1:
            mask = mask & ((kj < 2 * ATTN_BLOCK) | (qblk < n_qblocks - 1))
        s_t = jnp.where(mask, s_t, -1e30)
        sink = jnp.full((1, n_cols), sink_ref[heads[-1]], F32)
        for hl in range(ATTN_GROUP - 2, -1, -1):
            sink = jnp.where(head_of_col == hl, sink_ref[heads[hl]], sink)
        m = jnp.maximum(jnp.max(s_t, axis=0, keepdims=True), sink)
        p = jnp.exp(s_t - m)
        denom = jnp.sum(p, axis=0, keepdims=True) + jnp.exp(sink - m)
        p_t = p.astype(BF16)
        yield
        pending[kv] = jnp.dot(vbuf_t[kvl, krows], p_t, preferred_element_type=F32) / denom
        if kv == N_KV_HEADS - 1:
            o_t = jnp.concatenate([pending.pop(i) for i in range(N_KV_HEADS)], axis=0)
            for hl in range(ATTN_GROUP):
                cols = slice(hl * ATTN_BLOCK, (hl + 1) * ATTN_BLOCK)
                mix_ref[qrows, hl * ATTN_KV:(hl + 1) * ATTN_KV] = o_t[:, cols].T.astype(BF16)
        yield

    units = [unit(jb, kv) for jb in range(sub) for kv in range(N_KV_HEADS)]
    for slot in range(len(units) + 4):
        for stage in range(3):
            u = slot - 2 * stage
            if 0 <= u < len(units):
                next(units[u])
                yield


def _forward_sweep_kernel(sink_ref, qa_ref, kp_ref, kc_ref, kn_ref, vp_ref, vc_ref, vn_ref,
                          gq_ref, gk_ref, gv_ref, gg_ref, r_ref, ob_ref, wdec_ref, bdec_ref, tri_ref, gnorm_ref,
                          x1_ref, wout_ref, n2_ref, wg_ref, wu_ref, wd_ref, nf_ref,
                          y_ref, act_ref, mix_ref, s_ref, *, tiles_per_seq):
    g = pl.program_id(0)
    mix_cur = mix_ref.at[g % 2]
    mix_prev = mix_ref.at[1 - g % 2]

    @pl.when(g % tiles_per_seq == 0)
    def _():
        s_ref[...] = jnp.zeros_like(s_ref)

    tpos = jnp.minimum(g, pl.num_programs(0) - 2) % tiles_per_seq
    gain = gnorm_ref[...]

    def emit_mix(row0, h, o):
        rows = slice(row0, row0 + GLA_CHUNK)
        cols = slice(h * GLA_DV, (h + 1) * GLA_DV)
        o = _rms(o + ob_ref[rows, cols].astype(F32), gain) * _silu(gg_ref[rows, cols].astype(F32))
        mix_cur[rows, ATTN_Q + h * GLA_DV:ATTN_Q + (h + 1) * GLA_DV] = o.astype(BF16)

    def outproj_ffn2(side):
        side_per_tick = -(-(ATTN_STAGES_PER_TILE + GLA_STAGES_PER_TILE) // _ffn_ticks(FF_CHUNK_FFN2))
        x2_slabs = [x1_ref[r0:r0 + ROW_SLAB, :]
                    + jnp.dot(mix_prev[r0:r0 + ROW_SLAB, :], wout_ref[...], preferred_element_type=F32)
                    for r0 in range(0, ROW_TILE, ROW_SLAB)]
        x3 = _swiglu_residual(x2_slabs, n2_ref, wg_ref, wu_ref, wd_ref, act_ref, FF_CHUNK_FFN2, side, side_per_tick)
        for r0 in range(0, ROW_TILE, ROW_SLAB):
            y_ref[r0:r0 + ROW_SLAB, :] = _rms(x3[r0:r0 + ROW_SLAB, :], nf_ref[...])

    def step(with_mixer, with_ffn):
        side = iter(())
        if with_mixer:
            side = _round_robin(
                _attention_tile_stages(sink_ref, qa_ref, kp_ref, kc_ref, kn_ref, vp_ref, vc_ref, vn_ref, mix_cur,
                                       tpos, tiles_per_seq),
                *_gla_tile_streams(gq_ref, gk_ref, gv_ref, r_ref, wdec_ref, bdec_ref, tri_ref, s_ref, False,
                                   emit_mix))
        if with_ffn:
            outproj_ffn2(side)
        for _ in side:
            pass

    pl.when(g == 0)(lambda: step(True, False))
    pl.when(g > 0)(lambda: step(True, True))


def _forward_sweep(seq_len, sink, qa, ka, va, gq, gk, gv, gg, r, ob, wdec, bdec, tri_lo, gnorm,
                   x1, wout, n2, wg, wu, wd, nf):
    n_rows = x1.shape[0]
    tm = ROW_TILE
    nt = seq_len // tm
    n_tiles = n_rows // tm
    sub = tm // ATTN_BLOCK
    halo_per_seq = seq_len // ATTN_BLOCK
    cur = lambda g: jnp.minimum(g, n_tiles - 1)
    lag = lambda g: jnp.maximum(g - 1, 0)
    row = lambda w: pl.BlockSpec((tm, w), lambda g: (cur(g), 0))
    lag_row = lambda w: pl.BlockSpec((tm, w), lambda g: (lag(g), 0))

    def prev_idx(g):
        t = cur(g)
        return jnp.maximum(t * sub - 1, (t // nt) * halo_per_seq)

    def next_idx(g):
        t = cur(g)
        return jnp.minimum((t + 1) * sub, (t // nt + 1) * halo_per_seq - 1)

    prev = pl.BlockSpec((ATTN_BLOCK, ATTN_KV), lambda g: (prev_idx(g), 0))
    nxt = pl.BlockSpec((ATTN_BLOCK, ATTN_KV), lambda g: (next_idx(g), 0))
    prev_t = pl.BlockSpec((ATTN_KV, ATTN_BLOCK), lambda g: (0, prev_idx(g)))
    cur_t = pl.BlockSpec((ATTN_KV, tm), lambda g: (0, cur(g)))
    nxt_t = pl.BlockSpec((ATTN_KV, ATTN_BLOCK), lambda g: (0, next_idx(g)))
    return pl.pallas_call(
        functools.partial(_forward_sweep_kernel, tiles_per_seq=nt),
        grid=(n_tiles + 1,),
        in_specs=[pl.BlockSpec(memory_space=pltpu.SMEM),
                  row(ATTN_Q), prev, row(ATTN_KV), nxt, prev_t, cur_t, nxt_t,
                  row(GLA_QK), row(GLA_QK), row(GLA_V), row(GLA_V), row(2 * GLA_RANK), row(GLA_V),
                  _const_spec((2 * GLA_RANK, GLA_QK)), _const_spec((1, GLA_QK)), _const_spec((GLA_BLOCK, GLA_BLOCK)),
                  _const_spec((1, GLA_DV)),
                  lag_row(D_MODEL), _const_spec((D_MODEL, D_MODEL)), _const_spec((1, D_MODEL)),
                  _const_spec((D_MODEL, D_FF)), _const_spec((D_MODEL, D_FF)), _const_spec((D_FF, D_MODEL)),
                  _const_spec((1, D_MODEL))],
        out_specs=lag_row(D_MODEL),
        out_shape=jax.ShapeDtypeStruct((n_rows, D_MODEL), F32),
        scratch_shapes=[pltpu.VMEM((tm, D_FF), BF16), pltpu.VMEM((2, tm, D_MODEL), BF16),
                        pltpu.VMEM((GLA_QK, GLA_DV), F32)],
        compiler_params=pltpu.CompilerParams(dimension_semantics=("arbitrary",), vmem_limit_bytes=VMEM_LIMIT),
        name="forward_sweep",
    )(sink, qa, ka, ka, ka, va, va, va, gq, gk, gv, gg, r, ob, wdec, bdec, tri_lo, gnorm,
      x1, wout, n2, wg, wu, wd, nf)


def _rope_tables(seq_len):
    half = HEAD_DIM // 2
    inv_freq = (np.float32(ROPE_THETA) ** (-np.arange(half, dtype=np.float32) / np.float32(half))).astype(np.float32)
    ang = np.arange(seq_len, dtype=np.float32)[:, None] * inv_freq[None, :]
    cos, sin = np.cos(ang), np.sin(ang)
    cos_tab = np.tile(cos, (1, LANE // half))
    sin_tab = np.tile(np.concatenate([-sin, sin], axis=1), (1, LANE // HEAD_DIM))
    return jnp.asarray(cos_tab, dtype=F32), jnp.asarray(sin_tab, dtype=F32)


def _chunk_tri(bt, upper):
    i = np.arange(bt)[:, None]
    j = np.arange(bt)[None, :]
    same_chunk = (i // GLA_CHUNK) == (j // GLA_CHUNK)
    keep = (j >= i) if upper else (j <= i)
    return jnp.asarray(same_chunk & keep, dtype=BF16)


def _trunk(x, p):
    batch, seq_len, _ = x.shape
    cos_tab, sin_tab = _rope_tables(seq_len)
    x2d = x.reshape(batch * seq_len, D_MODEL)
    x1, qa, ka, va, gq, gk, gv, gg, r, ob = _reverse_sweep(
        x2d, seq_len, p["n1"], p["wg1"], p["wu1"], p["wd1"], p["nm"], p["win"], cos_tab, sin_tab,
        p["wdec_b"], p["bdec_b"], _chunk_tri(GLA_BLOCK, True))
    y = _forward_sweep(seq_len, p["sink"], qa, ka, va, gq, gk, gv, gg, r, ob,
                       p["wdec_f"], p["bdec_f"], _chunk_tri(GLA_BLOCK, False), p["gnorm"],
                       x1, p["wout"], p["n2"], p["wg2"], p["wu2"], p["wd2"], p["nf"])
    return y.reshape(batch, seq_len, D_MODEL)


def kernel(x_prompt, x_sample, norm_ffn1, w_ffn1_gate, w_ffn1_up, w_ffn1_down, norm_mix, w_in, attn_sink, w_gla_decay_fwd, b_gla_decay_fwd, w_gla_decay_bwd, b_gla_decay_bwd, gla_out_norm, w_out, norm_ffn2, w_ffn2_gate, w_ffn2_up, w_ffn2_down, norm_final):
    assert norm_ffn1.shape[0] == 1, "single-layer trunk"
    zeros_rank = jnp.zeros((GLA_RANK, GLA_QK), F32)
    p = dict(
        n1=norm_ffn1[0][None, :], wg1=w_ffn1_gate[0].astype(BF16), wu1=w_ffn1_up[0].astype(BF16),
        wd1=w_ffn1_down[0].astype(BF16),
        nm=norm_mix[0][None, :],
        win=w_in[0].astype(BF16),
        sink=attn_sink[0],
        wdec_f=jnp.concatenate([w_gla_decay_fwd[0], zeros_rank], axis=0).astype(BF16),
        bdec_f=b_gla_decay_fwd[0][None, :],
        wdec_b=jnp.concatenate([zeros_rank, w_gla_decay_bwd[0]], axis=0).astype(BF16),
        bdec_b=b_gla_decay_bwd[0][None, :],
        gnorm=gla_out_norm[0][None, :],
        wout=jnp.concatenate([
            w_out[0][:ATTN_Q].reshape(N_KV_HEADS, ATTN_GROUP, HEAD_DIM, D_MODEL).transpose(1, 0, 2, 3)
            .reshape(ATTN_Q, D_MODEL), w_out[0][ATTN_Q:]], axis=0).astype(BF16),
        n2=norm_ffn2[0][None, :], wg2=w_ffn2_gate[0].astype(BF16), wu2=w_ffn2_up[0].astype(BF16),
        wd2=w_ffn2_down[0].astype(BF16),
        nf=norm_final[None, :],
    )
    return _trunk(x_prompt, p), _trunk(x_sample, p)
```

```python
import functools

import jax
import jax.numpy as jnp
import numpy as np
from jax import lax
from jax.experimental import pallas as pl
from jax.experimental.pallas import tpu as pltpu

F32 = jnp.float32
BF16 = jnp.bfloat16

D_MODEL = 1024
D_FF = 2816
EPS = 1e-6
N_ATTN_HEADS = 8
N_KV_HEADS = 2
ATTN_GROUP = N_ATTN_HEADS // N_KV_HEADS
HEAD_DIM = 64
ATTN_BLOCK = 128
ROPE_THETA = 10000.0
N_GLA_HEADS = 4
GLA_DK = 64
GLA_DV = 128
GLA_RANK = 16
GLA_GATE_NORMALIZER = 16.0
GLA_CHUNK = 64
ATTN_Q = N_ATTN_HEADS * HEAD_DIM
ATTN_KV = N_KV_HEADS * HEAD_DIM
GLA_QK = N_GLA_HEADS * GLA_DK
GLA_V = N_GLA_HEADS * GLA_DV
IN_PROJ_WIDTH = ATTN_Q + 2 * ATTN_KV + 2 * GLA_QK + 2 * GLA_V + 2 * GLA_RANK
LANE = 128
MXU_COLS = 256
OFF_AQ = 0
OFF_AK = OFF_AQ + ATTN_Q
OFF_AV = OFF_AK + ATTN_KV
OFF_GQ = OFF_AV + ATTN_KV
OFF_GK = OFF_GQ + GLA_QK
OFF_GV = OFF_GK + GLA_QK
OFF_GG = OFF_GV + GLA_V
OFF_R = OFF_GG + GLA_V

ROW_TILE = 512
ROW_SLAB = 256
FF_CHUNK_FFN1 = 768
FF_CHUNK_FFN2 = 256
IN_PROJ_COLS = MXU_COLS
IN_PROJ_DOTS = ATTN_Q // IN_PROJ_COLS + 3 + 2 * (GLA_V // IN_PROJ_COLS) + 1
GLA_BLOCK = 256
VMEM_LIMIT = 56 * 1024 * 1024


def _rms(x, gain):
    return x * lax.rsqrt(jnp.mean(x * x, axis=-1, keepdims=True) + EPS) * gain


def _silu(x):
    return x * (1.0 / (1.0 + jnp.exp(-x)))


def _advance(side, n):
    for _ in range(n):
        next(side, None)


def _ffn_ticks(ff_chunk):
    return 1 + -(-D_FF // ff_chunk) + D_MODEL // MXU_COLS


def _swiglu_residual(x_slabs, gain_ref, wg_ref, wu_ref, wd_ref, act_ref, ff_chunk, side, side_per_tick):
    bounds = list(range(0, D_FF, ff_chunk)) + [D_FF]
    _advance(side, side_per_tick)
    gain = gain_ref[...]
    hs = [_rms(x, gain).astype(BF16) for x in x_slabs]

    def up_chunk(h, rows, c):
        sl = slice(bounds[c], bounds[c + 1])
        g = jnp.dot(h, wg_ref[:, sl], preferred_element_type=F32)
        u = jnp.dot(h, wu_ref[:, sl], preferred_element_type=F32)
        act_ref[rows, sl] = (_silu(g) * u).astype(BF16)

    row0 = 0
    for h in hs:
        up_chunk(h, slice(row0, row0 + h.shape[0]), 0)
        row0 += h.shape[0]
    _advance(side, side_per_tick)
    h = jnp.concatenate(hs, axis=0)
    for c in range(1, len(bounds) - 1):
        up_chunk(h, slice(0, row0), c)
        _advance(side, side_per_tick)
    x = jnp.concatenate(x_slabs, axis=0)
    out = []
    for j in range(D_MODEL // MXU_COLS):
        cols = slice(j * MXU_COLS, (j + 1) * MXU_COLS)
        y = jnp.dot(act_ref[...], wd_ref[:, cols], preferred_element_type=F32)
        out.append(x[:, cols] + 0.5 * y)
        _advance(side, side_per_tick)
    return jnp.concatenate(out, axis=1)


def _rope_pair(x, cos, sin_signed, first_half):
    swapped = jnp.where(first_half, pltpu.roll(x, LANE - HEAD_DIM // 2, 1), pltpu.roll(x, HEAD_DIM // 2, 1))
    return x * cos + swapped * sin_signed


GLA_CHUNKS_PER_BLOCK = GLA_BLOCK // GLA_CHUNK
GLA_BLOCKS_PER_TILE = ROW_TILE // GLA_BLOCK
GLA_STAGES_PER_TILE = GLA_BLOCKS_PER_TILE * (4 + 2 * GLA_CHUNKS_PER_BLOCK)


def _round_robin(*stage_generators):
    live = list(stage_generators)
    while live:
        for gen in list(live):
            try:
                next(gen)
                yield
            except StopIteration:
                live.remove(gen)


GLA_PROLOGUE_STAGES = 3


def _gla_prologue_stages(q_ref, k_ref, r_ref, brows, wdec_ref, bdec_ref, tri_ref, reverse, finish):
    bt = brows.stop - brows.start
    nch = bt // GLA_CHUNK
    z = jnp.dot(r_ref[brows, :], wdec_ref[...], preferred_element_type=F32) + bdec_ref[...]
    yield
    log_a = (jnp.minimum(z, 0.0) - jnp.log1p(jnp.exp(-jnp.abs(z)))) * (1.0 / GLA_GATE_NORMALIZER)
    hi = log_a.astype(BF16)
    lo = (log_a - hi.astype(F32)).astype(BF16)
    tri = tri_ref[...]
    cum = jnp.dot(tri, hi, preferred_element_type=F32) + jnp.dot(tri, lo, preferred_element_type=F32)
    yield
    q = q_ref[brows, :]
    k = k_ref[brows, :]
    cum3 = cum.reshape(nch, GLA_CHUNK, GLA_QK)
    edge = GLA_CHUNK - 1 if not reverse else 0
    tot3 = cum3[:, edge:edge + 1, :]
    rest = (tot3 - cum3).reshape(bt, GLA_QK)
    qe = (q * (GLA_DK ** -0.5) * jnp.exp(cum)).astype(BF16)
    ke = (k * jnp.exp(-cum)).astype(BF16)
    ks_t = (k * jnp.exp(rest)).T.astype(BF16)
    tot = tot3.reshape(nch, GLA_QK)
    tot_t = jnp.concatenate([tot, jnp.zeros((LANE - nch, GLA_QK), F32)], axis=0).T
    decay_t = jnp.exp(tot_t)
    finish(qe, ke, ks_t, decay_t)
    yield


def _gla_block_stages(q_ref, k_ref, v_ref, r_ref, brows, wdec_ref, bdec_ref, tri_ref, s_ref, reverse, emit,
                      prepared=None):
    bt = brows.stop - brows.start
    nch = bt // GLA_CHUNK
    if prepared is None:
        res = []
        yield from _gla_prologue_stages(q_ref, k_ref, r_ref, brows, wdec_ref, bdec_ref, tri_ref, reverse,
                                        lambda *vals: res.extend(vals))
        qe, ke, ks_t, decay_t = res
    else:
        qe_ref, ke_ref, kst_ref, dec_ref = prepared
        qe, ke, ks_t, decay_t = qe_ref[brows, :], ke_ref[brows, :], kst_ref[:, brows], dec_ref[brows, :]

    ii = lax.broadcasted_iota(jnp.int32, (GLA_CHUNK, GLA_CHUNK), 0)
    jj = lax.broadcasted_iota(jnp.int32, (GLA_CHUNK, GLA_CHUNK), 1)
    keep = (jj > ii) if reverse else (jj <= ii)
    rows = [slice(n * GLA_CHUNK, (n + 1) * GLA_CHUNK) for n in range(nch)]
    klanes = [slice(h * GLA_DK, (h + 1) * GLA_DK) for h in range(N_GLA_HEADS)]

    def v_of(n, h):
        return v_ref[brows.start + n * GLA_CHUNK:brows.start + (n + 1) * GLA_CHUNK, h * GLA_DV:(h + 1) * GLA_DV]

    a = [[None] * N_GLA_HEADS for _ in range(nch)]
    u = [None] * nch
    for n in range(nch):
        for h in range(N_GLA_HEADS):
            s_nh = lax.dot_general(qe[rows[n], klanes[h]], ke[rows[n], klanes[h]], (((1,), (1,)), ((), ())),
                                   preferred_element_type=F32)
            a[n][h] = jnp.where(keep, s_nh, 0.0).astype(BF16)
        u[n] = jnp.concatenate([jnp.dot(ks_t[klanes[h], rows[n]], v_of(n, h), preferred_element_type=F32)
                                for h in range(N_GLA_HEADS)], axis=0)
        yield
    s = s_ref[...]
    s_in = [None] * nch
    for n in (range(nch - 1, -1, -1) if reverse else range(nch)):
        s_in[n] = s.astype(BF16)
        s = decay_t[:, n:n + 1] * s + u[n]
    s_ref[...] = s
    yield
    for n in range(nch):
        for h in range(N_GLA_HEADS):
            emit(n, h, jnp.dot(a[n][h], v_of(n, h), preferred_element_type=F32)
                 + jnp.dot(qe[rows[n], klanes[h]], s_in[n][klanes[h], :], preferred_element_type=F32))
        yield


def _gla_tile_streams(q_ref, k_ref, v_ref, r_ref, wdec_ref, bdec_ref, tri_ref, s_ref, reverse, emit, prepared=None):
    streams = []
    for blk in (range(GLA_BLOCKS_PER_TILE - 1, -1, -1) if reverse else range(GLA_BLOCKS_PER_TILE)):
        brows = slice(blk * GLA_BLOCK, (blk + 1) * GLA_BLOCK)
        emit_block = lambda n, h, o, base=blk * GLA_BLOCK: emit(base + n * GLA_CHUNK, h, o)
        streams.append(_gla_block_stages(q_ref, k_ref, v_ref, r_ref, brows, wdec_ref, bdec_ref, tri_ref, s_ref,
                                         reverse, emit_block, prepared))
    return streams


def _gla_prepare_streams(q_ref, k_ref, r_ref, wdec_ref, bdec_ref, tri_ref, reverse, out_refs):
    qe_ref, ke_ref, kst_ref, dec_ref = out_refs

    def store(brows):
        def finish(qe, ke, ks_t, decay_t):
            qe_ref[brows, :] = qe
            ke_ref[brows, :] = ke
            kst_ref[:, brows] = ks_t
            dec_ref[brows, :] = decay_t
        return finish

    return [_gla_prologue_stages(q_ref, k_ref, r_ref, brows, wdec_ref, bdec_ref, tri_ref, reverse, store(brows))
            for brows in (slice(b * GLA_BLOCK, (b + 1) * GLA_BLOCK) for b in range(GLA_BLOCKS_PER_TILE))]


def _reverse_sweep_kernel(x_ref, n1_ref, wg_ref, wu_ref, wd_ref, nm_ref, win_ref, cos_ref, sin_ref,
                          wdec_ref, bdec_ref, tri_ref, wdecf_ref, bdecf_ref, trif_ref,
                          x1_ref, qa_ref, ka_ref, va_ref, gv_ref, gg_ref, ob_ref, qef_ref, kef_ref, kstf_ref, decf_ref,
                          act_ref, pq_ref, pk_ref, pv_ref, pr_ref, s_ref, *, tiles_per_seq):
    g = pl.program_id(0)
    last = pl.num_programs(0) - 1
    cur_slot = g % 2
    prev_slot = 1 - cur_slot

    @pl.when(jnp.maximum(g - 1, 0) % tiles_per_seq == 0)
    def _():
        s_ref[...] = jnp.zeros_like(s_ref)

    def emit_ob(row0, h, o):
        ob_ref[row0:row0 + GLA_CHUNK, h * GLA_DV:(h + 1) * GLA_DV] = o

    def ffn1_inproj(side):
        x_slabs = [x_ref[r0:r0 + ROW_SLAB, :] for r0 in range(0, ROW_TILE, ROW_SLAB)]
        n_side = GLA_STAGES_PER_TILE + GLA_BLOCKS_PER_TILE * GLA_PROLOGUE_STAGES
        side_per_tick = -(-(n_side - IN_PROJ_DOTS) // _ffn_ticks(FF_CHUNK_FFN1))
        x1 = _swiglu_residual(x_slabs, n1_ref, wg_ref, wu_ref, wd_ref, act_ref, FF_CHUNK_FFN1, side, side_per_tick)
        x1_ref[...] = x1
        h = jnp.concatenate([_rms(x1[r0:r0 + ROW_SLAB, :], nm_ref[...]).astype(BF16)
                             for r0 in range(0, ROW_TILE, ROW_SLAB)], axis=0)
        cos = cos_ref[...]
        sin = sin_ref[...]
        lane = lax.broadcasted_iota(jnp.int32, (1, LANE), 1)
        first_half = (lane % HEAD_DIM) < (HEAD_DIM // 2)

        def proj(off, width):
            _advance(side, 1)
            return jnp.dot(h, win_ref[:, off:off + width], preferred_element_type=F32)

        scale = HEAD_DIM ** -0.5
        for j in range(ATTN_Q // IN_PROJ_COLS):
            q2 = proj(OFF_AQ + j * IN_PROJ_COLS, IN_PROJ_COLS)
            for i in range(IN_PROJ_COLS // LANE):
                q = _rope_pair(q2[:, i * LANE:(i + 1) * LANE], cos, sin, first_half)
                qa_ref[:, j * IN_PROJ_COLS + i * LANE:j * IN_PROJ_COLS + (i + 1) * LANE] = (q * scale).astype(BF16)
        kv = proj(OFF_AK, 2 * ATTN_KV)
        ka_ref[...] = _rope_pair(kv[:, :ATTN_KV], cos, sin, first_half).astype(BF16)
        va_ref[...] = kv[:, ATTN_KV:].T.astype(BF16)
        gq = proj(OFF_GQ, GLA_QK)
        pq_ref[cur_slot] = gq
        gk = proj(OFF_GK, GLA_QK)
        pk_ref[cur_slot] = gk
        for j in range(GLA_V // IN_PROJ_COLS):
            cols = slice(j * IN_PROJ_COLS, (j + 1) * IN_PROJ_COLS)
            gv = proj(OFF_GV + j * IN_PROJ_COLS, IN_PROJ_COLS).astype(BF16)
            gv_ref[:, cols] = gv
            pv_ref[cur_slot, :, cols] = gv
            gg_ref[:, cols] = proj(OFF_GG + j * IN_PROJ_COLS, IN_PROJ_COLS)
        rr = proj(OFF_R, 2 * GLA_RANK).astype(BF16)
        pr_ref[cur_slot] = rr

    def step(with_ffn, with_scan):
        side = iter(())
        if with_scan:
            pq, pk, pr = pq_ref.at[prev_slot], pk_ref.at[prev_slot], pr_ref.at[prev_slot]
            side = _round_robin(
                *_gla_tile_streams(pq, pk, pv_ref.at[prev_slot], pr, wdec_ref, bdec_ref, tri_ref, s_ref, True,
                                   emit_ob),
                *_gla_prepare_streams(pq, pk, pr, wdecf_ref, bdecf_ref, trif_ref, False,
                                      (qef_ref, kef_ref, kstf_ref, decf_ref)))
        if with_ffn:
            ffn1_inproj(side)
        for _ in side:
            pass

    @pl.when(g == 0)
    def _():
        pq_ref[...] = jnp.zeros_like(pq_ref)
        pk_ref[...] = jnp.zeros_like(pk_ref)
        pv_ref[...] = jnp.zeros_like(pv_ref)
        pr_ref[...] = jnp.zeros_like(pr_ref)

    pl.when(g < last)(lambda: step(True, True))
    pl.when(g == last)(lambda: step(False, True))


def _const_spec(shape):
    return pl.BlockSpec(shape, lambda *_: (0,) * len(shape), pipeline_mode=pl.Buffered(1))


def _reverse_sweep(x2d, seq_len, n1, wg, wu, wd, nm, win, cos_tab, sin_tab, wdec, bdec, tri_up,
                   wdec_f, bdec_f, tri_lo):
    n_rows = x2d.shape[0]
    tm = ROW_TILE
    assert n_rows % tm == 0 and seq_len % tm == 0
    nt = seq_len // tm
    n_tiles = n_rows // tm

    def tile_of(step):
        return (step // nt) * nt + (nt - 1 - step % nt)

    cur = lambda g: tile_of(jnp.minimum(g, n_tiles - 1))
    lag = lambda g: tile_of(jnp.maximum(g - 1, 0))
    row = lambda w: pl.BlockSpec((tm, w), lambda g: (cur(g), 0))
    rope = pl.BlockSpec((tm, LANE), lambda g: (nt - 1 - jnp.minimum(g, n_tiles - 1) % nt, 0))
    out_shapes = (
        jax.ShapeDtypeStruct((n_rows, D_MODEL), F32),
        jax.ShapeDtypeStruct((n_rows, ATTN_Q), BF16),
        jax.ShapeDtypeStruct((n_rows, ATTN_KV), BF16),
        jax.ShapeDtypeStruct((ATTN_KV, n_rows), BF16),
        jax.ShapeDtypeStruct((n_rows, GLA_V), BF16),
        jax.ShapeDtypeStruct((n_rows, GLA_V), F32),
        jax.ShapeDtypeStruct((n_rows, GLA_V), F32),
        jax.ShapeDtypeStruct((n_rows, GLA_QK), BF16),
        jax.ShapeDtypeStruct((n_rows, GLA_QK), BF16),
        jax.ShapeDtypeStruct((GLA_QK, n_rows), BF16),
        jax.ShapeDtypeStruct((n_rows // GLA_BLOCK * GLA_QK, LANE), F32),
    )
    lag_row = lambda w: pl.BlockSpec((tm, w), lambda g: (lag(g), 0))
    return pl.pallas_call(
        functools.partial(_reverse_sweep_kernel, tiles_per_seq=nt),
        grid=(n_tiles + 1,),
        in_specs=[
            row(D_MODEL),
            _const_spec((1, D_MODEL)),
            _const_spec((D_MODEL, D_FF)), _const_spec((D_MODEL, D_FF)), _const_spec((D_FF, D_MODEL)),
            _const_spec((1, D_MODEL)),
            _const_spec((D_MODEL, IN_PROJ_WIDTH)),
            rope, rope,
            _const_spec((2 * GLA_RANK, GLA_QK)), _const_spec((1, GLA_QK)), _const_spec((GLA_BLOCK, GLA_BLOCK)),
            _const_spec((2 * GLA_RANK, GLA_QK)), _const_spec((1, GLA_QK)), _const_spec((GLA_BLOCK, GLA_BLOCK)),
        ],
        out_specs=[row(D_MODEL), row(ATTN_Q), row(ATTN_KV), pl.BlockSpec((ATTN_KV, tm), lambda g: (0, cur(g))),
                   row(GLA_V), row(GLA_V), lag_row(GLA_V),
                   lag_row(GLA_QK), lag_row(GLA_QK), pl.BlockSpec((GLA_QK, tm), lambda g: (0, lag(g))),
                   lag_row(LANE)],
        out_shape=out_shapes,
        scratch_shapes=[pltpu.VMEM((tm, D_FF), BF16),
                        pltpu.VMEM((2, tm, GLA_QK), F32), pltpu.VMEM((2, tm, GLA_QK), F32),
                        pltpu.VMEM((2, tm, GLA_V), BF16), pltpu.VMEM((2, tm, 2 * GLA_RANK), BF16),
                        pltpu.VMEM((GLA_QK, GLA_DV), F32)],
        compiler_params=pltpu.CompilerParams(dimension_semantics=("arbitrary",), vmem_limit_bytes=VMEM_LIMIT),
        name="reverse_sweep",
    )(x2d, n1, wg, wu, wd, nm, win, cos_tab, sin_tab, wdec, bdec, tri_up, wdec_f, bdec_f, tri_lo)


ATTN_UNITS_PER_TILE = (ROW_TILE // ATTN_BLOCK) * N_KV_HEADS
ATTN_STAGES_PER_TILE = 3 * ATTN_UNITS_PER_TILE


def _attention_tile_stages(sink_ref, qa_ref, kp_ref, kc_ref, kn_ref, vp_ref, vc_ref, vn_ref, mix_ref, tpos,
                           tiles_per_seq):
    sub = ROW_TILE // ATTN_BLOCK
    n_qblocks = tiles_per_seq * sub
    n_keys = 3 * ATTN_BLOCK
    n_cols = ATTN_GROUP * ATTN_BLOCK
    kbuf = jnp.concatenate([kp_ref[...], kc_ref[...], kn_ref[...]], axis=0)
    vbuf_t = jnp.concatenate([vp_ref[...], vc_ref[...], vn_ref[...]], axis=1)
    kj = lax.broadcasted_iota(jnp.int32, (n_keys, n_cols), 0)
    col = lax.broadcasted_iota(jnp.int32, (n_keys, n_cols), 1)
    qi = col % ATTN_BLOCK
    in_window = (kj >= qi) & (kj <= qi + 2 * ATTN_BLOCK)
    head_of_col = lax.broadcasted_iota(jnp.int32, (1, n_cols), 1) // ATTN_BLOCK
    pending = {}

    def unit(jb, kv):
        qblk = tpos * sub + jb
        qrows = slice(jb * ATTN_BLOCK, (jb + 1) * ATTN_BLOCK)
        krows = slice(jb * ATTN_BLOCK, (jb + 3) * ATTN_BLOCK)
        kvl = slice(kv * HEAD_DIM, (kv + 1) * HEAD_DIM)
        heads = range(kv * ATTN_GROUP, (kv + 1) * ATTN_GROUP)
        qs = jnp.concatenate([qa_ref[qrows, h * HEAD_DIM:(h + 1) * HEAD_DIM] for h in heads], axis=0)
        s_t = lax.dot_general(kbuf[krows, kvl], qs, (((1,), (1,)), ((), ())), preferred_element_type=F32)
        yield
        mask = in_window
        if jb == 0:
            mask = mask & ((kj >= ATTN_BLOCK) | (qblk > 0))
        if jb == sub - 1:
            mask = mask & ((kj < 2 * ATTN_BLOCK) | (qblk < n_qblocks - 1))
        s_t = jnp.where(mask, s_t, -1e30)
        sink = jnp.full((1, n_cols), sink_ref[heads[-1]], F32)
        for hl in range(ATTN_GROUP - 2, -1, -1):
            sink = jnp.where(head_of_col == hl, sink_ref[heads[hl]], sink)
        m = jnp.maximum(jnp.max(s_t, axis=0, keepdims=True), sink)
        p = jnp.exp(s_t - m)
        denom = jnp.sum(p, axis=0, keepdims=True) + jnp.exp(sink - m)
        p_t = p.astype(BF16)
        yield
        pending[kv] = jnp.dot(vbuf_t[kvl, krows], p_t, preferred_element_type=F32) / denom
        if kv == N_KV_HEADS - 1:
            o_t = jnp.concatenate([pending.pop(i) for i in range(N_KV_HEADS)], axis=0)
            for hl in range(ATTN_GROUP):
                cols = slice(hl * ATTN_BLOCK, (hl + 1) * ATTN_BLOCK)
                mix_ref[qrows, hl * ATTN_KV:(hl + 1) * ATTN_KV] = o_t[:, cols].T.astype(BF16)
        yield

    units = [unit(jb, kv) for jb in range(sub) for kv in range(N_KV_HEADS)]
    for slot in range(len(units) + 4):
        for stage in range(3):
            u = slot - 2 * stage
            if 0 <= u < len(units):
                next(units[u])
                yield


def _forward_sweep_kernel(sink_ref, qa_ref, kp_ref, kc_ref, kn_ref, vp_ref, vc_ref, vn_ref,
                          qe_ref, ke_ref, kst_ref, dec_ref, gv_ref, gg_ref, ob_ref, gnorm_ref,
                          x1_ref, wout_ref, n2_ref, wg_ref, wu_ref, wd_ref, nf_ref,
                          y_ref, act_ref, mix_ref, s_ref, *, tiles_per_seq):
    g = pl.program_id(0)
    mix_cur = mix_ref.at[g % 2]
    mix_prev = mix_ref.at[1 - g % 2]

    @pl.when(g % tiles_per_seq == 0)
    def _():
        s_ref[...] = jnp.zeros_like(s_ref)

    tpos = jnp.minimum(g, pl.num_programs(0) - 2) % tiles_per_seq
    gain = gnorm_ref[...]

    def emit_mix(row0, h, o):
        rows = slice(row0, row0 + GLA_CHUNK)
        cols = slice(h * GLA_DV, (h + 1) * GLA_DV)
        o = _rms(o + ob_ref[rows, cols], gain) * _silu(gg_ref[rows, cols])
        mix_cur[rows, ATTN_Q + h * GLA_DV:ATTN_Q + (h + 1) * GLA_DV] = o.astype(BF16)

    def outproj_ffn2(side):
        n_side = ATTN_STAGES_PER_TILE + GLA_STAGES_PER_TILE - GLA_BLOCKS_PER_TILE * GLA_PROLOGUE_STAGES
        side_per_tick = -(-n_side // _ffn_ticks(FF_CHUNK_FFN2))
        x2_slabs = [x1_ref[r0:r0 + ROW_SLAB, :]
                    + jnp.dot(mix_prev[r0:r0 + ROW_SLAB, :], wout_ref[...], preferred_element_type=F32)
                    for r0 in range(0, ROW_TILE, ROW_SLAB)]
        x3 = _swiglu_residual(x2_slabs, n2_ref, wg_ref, wu_ref, wd_ref, act_ref, FF_CHUNK_FFN2, side, side_per_tick)
        for r0 in range(0, ROW_TILE, ROW_SLAB):
            y_ref[r0:r0 + ROW_SLAB, :] = _rms(x3[r0:r0 + ROW_SLAB, :], nf_ref[...])

    def step(with_mixer, with_ffn):
        side = iter(())
        if with_mixer:
            side = _round_robin(
                _attention_tile_stages(sink_ref, qa_ref, kp_ref, kc_ref, kn_ref, vp_ref, vc_ref, vn_ref, mix_cur,
                                       tpos, tiles_per_seq),
                *_gla_tile_streams(None, None, gv_ref, None, None, None, None, s_ref, False, emit_mix,
                                   prepared=(qe_ref, ke_ref, kst_ref, dec_ref)))
        if with_ffn:
            outproj_ffn2(side)
        for _ in side:
            pass

    pl.when(g == 0)(lambda: step(True, False))
    pl.when(g > 0)(lambda: step(True, True))


def _forward_sweep(seq_len, sink, qa, ka, va, qe, ke, kst, dec, gv, gg, ob, gnorm,
                   x1, wout, n2, wg, wu, wd, nf):
    n_rows = x1.shape[0]
    tm = ROW_TILE
    nt = seq_len // tm
    n_tiles = n_rows // tm
    sub = tm // ATTN_BLOCK
    halo_per_seq = seq_len // ATTN_BLOCK
    cur = lambda g: jnp.minimum(g, n_tiles - 1)
    lag = lambda g: jnp.maximum(g - 1, 0)
    row = lambda w: pl.BlockSpec((tm, w), lambda g: (cur(g), 0))
    lag_row = lambda w: pl.BlockSpec((tm, w), lambda g: (lag(g), 0))

    def prev_idx(g):
        t = cur(g)
        return jnp.maximum(t * sub - 1, (t // nt) * halo_per_seq)

    def next_idx(g):
        t = cur(g)
        return jnp.minimum((t + 1) * sub, (t // nt + 1) * halo_per_seq - 1)

    prev = pl.BlockSpec((ATTN_BLOCK, ATTN_KV), lambda g: (prev_idx(g), 0))
    nxt = pl.BlockSpec((ATTN_BLOCK, ATTN_KV), lambda g: (next_idx(g), 0))
    prev_t = pl.BlockSpec((ATTN_KV, ATTN_BLOCK), lambda g: (0, prev_idx(g)))
    cur_t = pl.BlockSpec((ATTN_KV, tm), lambda g: (0, cur(g)))
    nxt_t = pl.BlockSpec((ATTN_KV, ATTN_BLOCK), lambda g: (0, next_idx(g)))
    return pl.pallas_call(
        functools.partial(_forward_sweep_kernel, tiles_per_seq=nt),
        grid=(n_tiles + 1,),
        in_specs=[pl.BlockSpec(memory_space=pltpu.SMEM),
                  row(ATTN_Q), prev, row(ATTN_KV), nxt, prev_t, cur_t, nxt_t,
                  row(GLA_QK), row(GLA_QK), pl.BlockSpec((GLA_QK, tm), lambda g: (0, cur(g))), row(LANE),
                  row(GLA_V), row(GLA_V), row(GLA_V),
                  _const_spec((1, GLA_DV)),
                  lag_row(D_MODEL), _const_spec((D_MODEL, D_MODEL)), _const_spec((1, D_MODEL)),
                  _const_spec((D_MODEL, D_FF)), _const_spec((D_MODEL, D_FF)), _const_spec((D_FF, D_MODEL)),
                  _const_spec((1, D_MODEL))],
        out_specs=lag_row(D_MODEL),
        out_shape=jax.ShapeDtypeStruct((n_rows, D_MODEL), F32),
        scratch_shapes=[pltpu.VMEM((tm, D_FF), BF16), pltpu.VMEM((2, tm, D_MODEL), BF16),
                        pltpu.VMEM((GLA_QK, GLA_DV), F32)],
        compiler_params=pltpu.CompilerParams(dimension_semantics=("arbitrary",), vmem_limit_bytes=VMEM_LIMIT),
        name="forward_sweep",
    )(sink, qa, ka, ka, ka, va, va, va, qe, ke, kst, dec, gv, gg, ob, gnorm,
      x1, wout, n2, wg, wu, wd, nf)


def _rope_tables(seq_len):
    half = HEAD_DIM // 2
    inv_freq = (np.float32(ROPE_THETA) ** (-np.arange(half, dtype=np.float32) / np.float32(half))).astype(np.float32)
    ang = np.arange(seq_len, dtype=np.float32)[:, None] * inv_freq[None, :]
    cos, sin = np.cos(ang), np.sin(ang)
    cos_tab = np.tile(cos, (1, LANE // half))
    sin_tab = np.tile(np.concatenate([-sin, sin], axis=1), (1, LANE // HEAD_DIM))
    return jnp.asarray(cos_tab, dtype=F32), jnp.asarray(sin_tab, dtype=F32)


def _chunk_tri(bt, upper):
    i = np.arange(bt)[:, None]
    j = np.arange(bt)[None, :]
    same_chunk = (i // GLA_CHUNK) == (j // GLA_CHUNK)
    keep = (j >= i) if upper else (j <= i)
    return jnp.asarray(same_chunk & keep, dtype=BF16)


def _trunk(x, p):
    batch, seq_len, _ = x.shape
    cos_tab, sin_tab = _rope_tables(seq_len)
    x2d = x.reshape(batch * seq_len, D_MODEL)
    x1, qa, ka, va, gv, gg, ob, qe, ke, kst, dec = _reverse_sweep(
        x2d, seq_len, p["n1"], p["wg1"], p["wu1"], p["wd1"], p["nm"], p["win"], cos_tab, sin_tab,
        p["wdec_b"], p["bdec_b"], _chunk_tri(GLA_BLOCK, True),
        p["wdec_f"], p["bdec_f"], _chunk_tri(GLA_BLOCK, False))
    y = _forward_sweep(seq_len, p["sink"], qa, ka, va, qe, ke, kst, dec, gv, gg, ob, p["gnorm"],
                       x1, p["wout"], p["n2"], p["wg2"], p["wu2"], p["wd2"], p["nf"])
    return y.reshape(batch, seq_len, D_MODEL)


def kernel(x_prompt, x_sample, norm_ffn1, w_ffn1_gate, w_ffn1_up, w_ffn1_down, norm_mix, w_in, attn_sink, w_gla_decay_fwd, b_gla_decay_fwd, w_gla_decay_bwd, b_gla_decay_bwd, gla_out_norm, w_out, norm_ffn2, w_ffn2_gate, w_ffn2_up, w_ffn2_down, norm_final):
    assert norm_ffn1.shape[0] == 1, "single-layer trunk"
    zeros_rank = jnp.zeros((GLA_RANK, GLA_QK), F32)
    p = dict(
        n1=norm_ffn1[0][None, :], wg1=w_ffn1_gate[0].astype(BF16), wu1=w_ffn1_up[0].astype(BF16),
        wd1=w_ffn1_down[0].astype(BF16),
        nm=norm_mix[0][None, :],
        win=w_in[0].astype(BF16),
        sink=attn_sink[0],
        wdec_f=jnp.concatenate([w_gla_decay_fwd[0], zeros_rank], axis=0).astype(BF16),
        bdec_f=b_gla_decay_fwd[0][None, :],
        wdec_b=jnp.concatenate([zeros_rank, w_gla_decay_bwd[0]], axis=0).astype(BF16),
        bdec_b=b_gla_decay_bwd[0][None, :],
        gnorm=gla_out_norm[0][None, :],
        wout=jnp.concatenate([
            w_out[0][:ATTN_Q].reshape(N_KV_HEADS, ATTN_GROUP, HEAD_DIM, D_MODEL).transpose(1, 0, 2, 3)
            .reshape(ATTN_Q, D_MODEL), w_out[0][ATTN_Q:]], axis=0).astype(BF16),
        n2=norm_ffn2[0][None, :], wg2=w_ffn2_gate[0].astype(BF16), wu2=w_ffn2_up[0].astype(BF16),
        wd2=w_ffn2_down[0].astype(BF16),
        nf=norm_final[None, :],
    )
    return _trunk(x_prompt, p), _trunk(x_sample, p)
```

```python
import functools

import jax
import jax.numpy as jnp
import numpy as np
from jax import lax
from jax.experimental import pallas as pl
from jax.experimental.pallas import tpu as pltpu

F32 = jnp.float32
BF16 = jnp.bfloat16

D_MODEL = 1024
D_FF = 2816
EPS = 1e-6
N_ATTN_HEADS = 8
N_KV_HEADS = 2
ATTN_GROUP = N_ATTN_HEADS // N_KV_HEADS
HEAD_DIM = 64
ATTN_BLOCK = 128
ROPE_THETA = 10000.0
N_GLA_HEADS = 4
GLA_DK = 64
GLA_DV = 128
GLA_RANK = 16
GLA_GATE_NORMALIZER = 16.0
GLA_CHUNK = 64
ATTN_Q = N_ATTN_HEADS * HEAD_DIM
ATTN_KV = N_KV_HEADS * HEAD_DIM
GLA_QK = N_GLA_HEADS * GLA_DK
GLA_V = N_GLA_HEADS * GLA_DV
IN_PROJ_WIDTH = ATTN_Q + 2 * ATTN_KV + 2 * GLA_QK + 2 * GLA_V + 2 * GLA_RANK
LANE = 128
MXU_COLS = 256
OFF_AQ = 0
OFF_AK = OFF_AQ + ATTN_Q
OFF_AV = OFF_AK + ATTN_KV
OFF_GQ = OFF_AV + ATTN_KV
OFF_GK = OFF_GQ + GLA_QK
OFF_GV = OFF_GK + GLA_QK
OFF_GG = OFF_GV + GLA_V
OFF_R = OFF_GG + GLA_V

ROW_TILE = 512
ROW_SLAB = 256
FF_CHUNK_FFN1 = 768
FF_CHUNK_FFN2 = 256
IN_PROJ_COLS = MXU_COLS
IN_PROJ_DOTS = ATTN_Q // IN_PROJ_COLS + 3 + 2 * (GLA_V // IN_PROJ_COLS) + 1
GLA_BLOCK = 256
VMEM_LIMIT = 56 * 1024 * 1024


def _rms(x, gain):
    return x * lax.rsqrt(jnp.mean(x * x, axis=-1, keepdims=True) + EPS) * gain


def _silu(x):
    return x * (1.0 / (1.0 + jnp.exp(-x)))


def _advance(side, n):
    for _ in range(n):
        next(side, None)


def _ffn_ticks(ff_chunk):
    return 1 + -(-D_FF // ff_chunk) + D_MODEL // MXU_COLS


def _swiglu_residual(x_slabs, gain_ref, wg_ref, wu_ref, wd_ref, act_ref, ff_chunk, side, side_per_tick):
    bounds = list(range(0, D_FF, ff_chunk)) + [D_FF]
    _advance(side, side_per_tick)
    gain = gain_ref[...]
    hs = [_rms(x, gain).astype(BF16) for x in x_slabs]

    def up_chunk(h, rows, c):
        sl = slice(bounds[c], bounds[c + 1])
        g = jnp.dot(h, wg_ref[:, sl], preferred_element_type=F32)
        u = jnp.dot(h, wu_ref[:, sl], preferred_element_type=F32)
        act_ref[rows, sl] = (_silu(g) * u).astype(BF16)

    row0 = 0
    for h in hs:
        up_chunk(h, slice(row0, row0 + h.shape[0]), 0)
        row0 += h.shape[0]
    _advance(side, side_per_tick)
    h = jnp.concatenate(hs, axis=0)
    for c in range(1, len(bounds) - 1):
        up_chunk(h, slice(0, row0), c)
        _advance(side, side_per_tick)
    x = jnp.concatenate(x_slabs, axis=0)
    out = []
    for j in range(D_MODEL // MXU_COLS):
        cols = slice(j * MXU_COLS, (j + 1) * MXU_COLS)
        y = jnp.dot(act_ref[...], wd_ref[:, cols], preferred_element_type=F32)
        out.append(x[:, cols] + 0.5 * y)
        _advance(side, side_per_tick)
    return jnp.concatenate(out, axis=1)


def _rope_pair(x, cos, sin_signed, first_half):
    swapped = jnp.where(first_half, pltpu.roll(x, LANE - HEAD_DIM // 2, 1), pltpu.roll(x, HEAD_DIM // 2, 1))
    return x * cos + swapped * sin_signed


GLA_CHUNKS_PER_BLOCK = GLA_BLOCK // GLA_CHUNK
GLA_BLOCKS_PER_TILE = ROW_TILE // GLA_BLOCK
GLA_STAGES_PER_TILE = GLA_BLOCKS_PER_TILE * (4 + 2 * GLA_CHUNKS_PER_BLOCK)


def _round_robin(*stage_generators):
    live = list(stage_generators)
    while live:
        for gen in list(live):
            try:
                next(gen)
                yield
            except StopIteration:
                live.remove(gen)


def _gla_block_stages(q_ref, k_ref, v_ref, r_ref, brows, wdec_ref, bdec_ref, tri_ref, s_ref, reverse, emit):
    bt = brows.stop - brows.start
    nch = bt // GLA_CHUNK
    z = jnp.dot(r_ref[brows, :], wdec_ref[...], preferred_element_type=F32) + bdec_ref[...]
    yield
    log_a = (jnp.minimum(z, 0.0) - jnp.log1p(jnp.exp(-jnp.abs(z)))) * (1.0 / GLA_GATE_NORMALIZER)
    hi = log_a.astype(BF16)
    lo = (log_a - hi.astype(F32)).astype(BF16)
    tri = tri_ref[...]
    cum = jnp.dot(tri, hi, preferred_element_type=F32) + jnp.dot(tri, lo, preferred_element_type=F32)
    yield
    q = q_ref[brows, :]
    k = k_ref[brows, :]
    cum3 = cum.reshape(nch, GLA_CHUNK, GLA_QK)
    edge = GLA_CHUNK - 1 if not reverse else 0
    tot3 = cum3[:, edge:edge + 1, :]
    rest = (tot3 - cum3).reshape(bt, GLA_QK)
    qe = (q * (GLA_DK ** -0.5) * jnp.exp(cum)).astype(BF16)
    ke = (k * jnp.exp(-cum)).astype(BF16)
    ks_t = (k * jnp.exp(rest)).T.astype(BF16)
    tot = tot3.reshape(nch, GLA_QK)
    tot_t = jnp.concatenate([tot, jnp.zeros((LANE - nch, GLA_QK), F32)], axis=0).T
    decay_t = jnp.exp(tot_t)

    ii = lax.broadcasted_iota(jnp.int32, (GLA_CHUNK, GLA_CHUNK), 0)
    jj = lax.broadcasted_iota(jnp.int32, (GLA_CHUNK, GLA_CHUNK), 1)
    keep = (jj > ii) if reverse else (jj <= ii)
    rows = [slice(n * GLA_CHUNK, (n + 1) * GLA_CHUNK) for n in range(nch)]
    klanes = [slice(h * GLA_DK, (h + 1) * GLA_DK) for h in range(N_GLA_HEADS)]

    def v_of(n, h):
        return v_ref[brows.start + n * GLA_CHUNK:brows.start + (n + 1) * GLA_CHUNK, h * GLA_DV:(h + 1) * GLA_DV]

    yield

    a = [[None] * N_GLA_HEADS for _ in range(nch)]
    u = [None] * nch
    for n in range(nch):
        for h in range(N_GLA_HEADS):
            s_nh = lax.dot_general(qe[rows[n], klanes[h]], ke[rows[n], klanes[h]], (((1,), (1,)), ((), ())),
                                   preferred_element_type=F32)
            a[n][h] = jnp.where(keep, s_nh, 0.0).astype(BF16)
        u[n] = jnp.concatenate([jnp.dot(ks_t[klanes[h], rows[n]], v_of(n, h), preferred_element_type=F32)
                                for h in range(N_GLA_HEADS)], axis=0)
        yield
    s = s_ref[...]
    s_in = [None] * nch
    for n in (range(nch - 1, -1, -1) if reverse else range(nch)):
        s_in[n] = s.astype(BF16)
        s = decay_t[:, n:n + 1] * s + u[n]
    s_ref[...] = s
    yield
    for n in range(nch):
        for h in range(N_GLA_HEADS):
            emit(n, h, jnp.dot(a[n][h], v_of(n, h), preferred_element_type=F32)
                 + jnp.dot(qe[rows[n], klanes[h]], s_in[n][klanes[h], :], preferred_element_type=F32))
        yield


def _gla_tile_streams(q_ref, k_ref, v_ref, r_ref, wdec_ref, bdec_ref, tri_ref, s_ref, reverse, emit):
    streams = []
    for blk in (range(GLA_BLOCKS_PER_TILE - 1, -1, -1) if reverse else range(GLA_BLOCKS_PER_TILE)):
        brows = slice(blk * GLA_BLOCK, (blk + 1) * GLA_BLOCK)
        emit_block = lambda n, h, o, base=blk * GLA_BLOCK: emit(base + n * GLA_CHUNK, h, o)
        streams.append(_gla_block_stages(q_ref, k_ref, v_ref, r_ref, brows, wdec_ref, bdec_ref, tri_ref, s_ref,
                                         reverse, emit_block))
    return streams


def _reverse_sweep_kernel(x_ref, n1_ref, wg_ref, wu_ref, wd_ref, nm_ref, win_ref, cos_ref, sin_ref,
                          wdec_ref, bdec_ref, tri_ref,
                          x1_ref, qa_ref, ka_ref, va_ref, gq_ref, gk_ref, gv_ref, gg_ref, r_ref, ob_ref,
                          act_ref, pq_ref, pk_ref, pv_ref, pr_ref, s_ref, *, tiles_per_seq):
    g = pl.program_id(0)
    last = pl.num_programs(0) - 1
    cur_slot = g % 2
    prev_slot = 1 - cur_slot

    @pl.when(jnp.maximum(g - 1, 0) % tiles_per_seq == 0)
    def _():
        s_ref[...] = jnp.zeros_like(s_ref)

    def emit_ob(row0, h, o):
        ob_ref[row0:row0 + GLA_CHUNK, h * GLA_DV:(h + 1) * GLA_DV] = o

    def ffn1_inproj(side):
        x_slabs = [x_ref[r0:r0 + ROW_SLAB, :] for r0 in range(0, ROW_TILE, ROW_SLAB)]
        side_per_tick = -(-(GLA_STAGES_PER_TILE - IN_PROJ_DOTS) // _ffn_ticks(FF_CHUNK_FFN1))
        x1 = _swiglu_residual(x_slabs, n1_ref, wg_ref, wu_ref, wd_ref, act_ref, FF_CHUNK_FFN1, side, side_per_tick)
        x1_ref[...] = x1
        h = jnp.concatenate([_rms(x1[r0:r0 + ROW_SLAB, :], nm_ref[...]).astype(BF16)
                             for r0 in range(0, ROW_TILE, ROW_SLAB)], axis=0)
        cos = cos_ref[...]
        sin = sin_ref[...]
        lane = lax.broadcasted_iota(jnp.int32, (1, LANE), 1)
        first_half = (lane % HEAD_DIM) < (HEAD_DIM // 2)

        def proj(off, width):
            _advance(side, 1)
            return jnp.dot(h, win_ref[:, off:off + width], preferred_element_type=F32)

        scale = HEAD_DIM ** -0.5
        for j in range(ATTN_Q // IN_PROJ_COLS):
            q2 = proj(OFF_AQ + j * IN_PROJ_COLS, IN_PROJ_COLS)
            for i in range(IN_PROJ_COLS // LANE):
                q = _rope_pair(q2[:, i * LANE:(i + 1) * LANE], cos, sin, first_half)
                qa_ref[:, j * IN_PROJ_COLS + i * LANE:j * IN_PROJ_COLS + (i + 1) * LANE] = (q * scale).astype(BF16)
        kv = proj(OFF_AK, 2 * ATTN_KV)
        ka_ref[...] = _rope_pair(kv[:, :ATTN_KV], cos, sin, first_half).astype(BF16)
        va_ref[...] = kv[:, ATTN_KV:].T.astype(BF16)
        gq = proj(OFF_GQ, GLA_QK)
        gq_ref[...] = gq
        pq_ref[cur_slot] = gq
        gk = proj(OFF_GK, GLA_QK)
        gk_ref[...] = gk
        pk_ref[cur_slot] = gk
        for j in range(GLA_V // IN_PROJ_COLS):
            cols = slice(j * IN_PROJ_COLS, (j + 1) * IN_PROJ_COLS)
            gv = proj(OFF_GV + j * IN_PROJ_COLS, IN_PROJ_COLS).astype(BF16)
            gv_ref[:, cols] = gv
            pv_ref[cur_slot, :, cols] = gv
            gg_ref[:, cols] = proj(OFF_GG + j * IN_PROJ_COLS, IN_PROJ_COLS)
        rr = proj(OFF_R, 2 * GLA_RANK).astype(BF16)
        r_ref[...] = rr
        pr_ref[cur_slot] = rr

    def step(with_ffn, with_scan):
        side = iter(())
        if with_scan:
            side = _round_robin(*_gla_tile_streams(pq_ref.at[prev_slot], pk_ref.at[prev_slot], pv_ref.at[prev_slot],
                                                   pr_ref.at[prev_slot], wdec_ref, bdec_ref, tri_ref, s_ref, True,
                                                   emit_ob))
        if with_ffn:
            ffn1_inproj(side)
        for _ in side:
            pass

    @pl.when(g == 0)
    def _():
        pq_ref[...] = jnp.zeros_like(pq_ref)
        pk_ref[...] = jnp.zeros_like(pk_ref)
        pv_ref[...] = jnp.zeros_like(pv_ref)
        pr_ref[...] = jnp.zeros_like(pr_ref)

    pl.when(g < last)(lambda: step(True, True))
    pl.when(g == last)(lambda: step(False, True))


def _const_spec(shape):
    return pl.BlockSpec(shape, lambda *_: (0,) * len(shape), pipeline_mode=pl.Buffered(1))


def _reverse_sweep(x2d, seq_len, n1, wg, wu, wd, nm, win, cos_tab, sin_tab, wdec, bdec, tri_up):
    n_rows = x2d.shape[0]
    tm = ROW_TILE
    assert n_rows % tm == 0 and seq_len % tm == 0
    nt = seq_len // tm
    n_tiles = n_rows // tm

    def tile_of(step):
        return (step // nt) * nt + (nt - 1 - step % nt)

    cur = lambda g: tile_of(jnp.minimum(g, n_tiles - 1))
    lag = lambda g: tile_of(jnp.maximum(g - 1, 0))
    row = lambda w: pl.BlockSpec((tm, w), lambda g: (cur(g), 0))
    rope = pl.BlockSpec((tm, LANE), lambda g: (nt - 1 - jnp.minimum(g, n_tiles - 1) % nt, 0))
    out_shapes = (
        jax.ShapeDtypeStruct((n_rows, D_MODEL), F32),
        jax.ShapeDtypeStruct((n_rows, ATTN_Q), BF16),
        jax.ShapeDtypeStruct((n_rows, ATTN_KV), BF16),
        jax.ShapeDtypeStruct((ATTN_KV, n_rows), BF16),
        jax.ShapeDtypeStruct((n_rows, GLA_QK), F32),
        jax.ShapeDtypeStruct((n_rows, GLA_QK), F32),
        jax.ShapeDtypeStruct((n_rows, GLA_V), BF16),
        jax.ShapeDtypeStruct((n_rows, GLA_V), F32),
        jax.ShapeDtypeStruct((n_rows, 2 * GLA_RANK), BF16),
        jax.ShapeDtypeStruct((n_rows, GLA_V), F32),
    )
    return pl.pallas_call(
        functools.partial(_reverse_sweep_kernel, tiles_per_seq=nt),
        grid=(n_tiles + 1,),
        in_specs=[
            row(D_MODEL),
            _const_spec((1, D_MODEL)),
            _const_spec((D_MODEL, D_FF)), _const_spec((D_MODEL, D_FF)), _const_spec((D_FF, D_MODEL)),
            _const_spec((1, D_MODEL)),
            _const_spec((D_MODEL, IN_PROJ_WIDTH)),
            rope, rope,
            _const_spec((2 * GLA_RANK, GLA_QK)), _const_spec((1, GLA_QK)), _const_spec((GLA_BLOCK, GLA_BLOCK)),
        ],
        out_specs=[row(D_MODEL), row(ATTN_Q), row(ATTN_KV), pl.BlockSpec((ATTN_KV, tm), lambda g: (0, cur(g))),
                   row(GLA_QK), row(GLA_QK),
                   row(GLA_V), row(GLA_V), row(2 * GLA_RANK),
                   pl.BlockSpec((tm, GLA_V), lambda g: (lag(g), 0))],
        out_shape=out_shapes,
        scratch_shapes=[pltpu.VMEM((tm, D_FF), BF16),
                        pltpu.VMEM((2, tm, GLA_QK), F32), pltpu.VMEM((2, tm, GLA_QK), F32),
                        pltpu.VMEM((2, tm, GLA_V), BF16), pltpu.VMEM((2, tm, 2 * GLA_RANK), BF16),
                        pltpu.VMEM((GLA_QK, GLA_DV), F32)],
        compiler_params=pltpu.CompilerParams(dimension_semantics=("arbitrary",), vmem_limit_bytes=VMEM_LIMIT),
        name="reverse_sweep",
    )(x2d, n1, wg, wu, wd, nm, win, cos_tab, sin_tab, wdec, bdec, tri_up)


ATTN_UNITS_PER_TILE = (ROW_TILE // ATTN_BLOCK) * N_KV_HEADS
ATTN_STAGES_PER_TILE = 3 * ATTN_UNITS_PER_TILE


def _attention_tile_stages(sink_ref, qa_ref, kp_ref, kc_ref, kn_ref, vp_ref, vc_ref, vn_ref, mix_ref, tpos,
                           tiles_per_seq):
    sub = ROW_TILE // ATTN_BLOCK
    n_qblocks = tiles_per_seq * sub
    n_keys = 3 * ATTN_BLOCK
    n_cols = ATTN_GROUP * ATTN_BLOCK
    kbuf = jnp.concatenate([kp_ref[...], kc_ref[...], kn_ref[...]], axis=0)
    vbuf_t = jnp.concatenate([vp_ref[...], vc_ref[...], vn_ref[...]], axis=1)
    kj = lax.broadcasted_iota(jnp.int32, (n_keys, n_cols), 0)
    col = lax.broadcasted_iota(jnp.int32, (n_keys, n_cols), 1)
    qi = col % ATTN_BLOCK
    in_window = (kj >= qi) & (kj <= qi + 2 * ATTN_BLOCK)
    head_of_col = lax.broadcasted_iota(jnp.int32, (1, n_cols), 1) // ATTN_BLOCK
    pending = {}

    def unit(jb, kv):
        qblk = tpos * sub + jb
        qrows = slice(jb * ATTN_BLOCK, (jb + 1) * ATTN_BLOCK)
        krows = slice(jb * ATTN_BLOCK, (jb + 3) * ATTN_BLOCK)
        kvl = slice(kv * HEAD_DIM, (kv + 1) * HEAD_DIM)
        heads = range(kv * ATTN_GROUP, (kv + 1) * ATTN_GROUP)
        qs = jnp.concatenate([qa_ref[qrows, h * HEAD_DIM:(h + 1) * HEAD_DIM] for h in heads], axis=0)
        s_t = lax.dot_general(kbuf[krows, kvl], qs, (((1,), (1,)), ((), ())), preferred_element_type=F32)
        yield
        mask = in_window
        if jb == 0:
            mask = mask & ((kj >= ATTN_BLOCK) | (qblk > 0))
        if jb == sub - 1:
            mask = mask & ((kj < 2 * ATTN_BLOCK) | (qblk < n_qblocks - 1))
        s_t = jnp.where(mask, s_t, -1e30)
        sink = jnp.full((1, n_cols), sink_ref[heads[-1]], F32)
        for hl in range(ATTN_GROUP - 2, -1, -1):
            sink = jnp.where(head_of_col == hl, sink_ref[heads[hl]], sink)
        m = jnp.maximum(jnp.max(s_t, axis=0, keepdims=True), sink)
        p = jnp.exp(s_t - m)
        inv_denom = 1.0 / (jnp.sum(p, axis=0, keepdims=True) + jnp.exp(sink - m))
        p_t = p.astype(BF16)
        yield
        pending[kv] = jnp.dot(vbuf_t[kvl, krows], p_t, preferred_element_type=F32) * inv_denom
        if kv == N_KV_HEADS - 1:
            o_t = jnp.concatenate([pending.pop(i) for i in range(N_KV_HEADS)], axis=0)
            for hl in range(ATTN_GROUP):
                cols = slice(hl * ATTN_BLOCK, (hl + 1) * ATTN_BLOCK)
                mix_ref[qrows, hl * ATTN_KV:(hl + 1) * ATTN_KV] = o_t[:, cols].T.astype(BF16)
        yield

    units = [unit(jb, kv) for jb in range(sub) for kv in range(N_KV_HEADS)]
    for slot in range(len(units) + 4):
        for stage in range(3):
            u = slot - 2 * stage
            if 0 <= u < len(units):
                next(units[u])
                yield


def _forward_sweep_kernel(sink_ref, qa_ref, kp_ref, kc_ref, kn_ref, vp_ref, vc_ref, vn_ref,
                          gq_ref, gk_ref, gv_ref, gg_ref, r_ref, ob_ref, wdec_ref, bdec_ref, tri_ref, gnorm_ref,
                          x1_ref, wout_ref, n2_ref, wg_ref, wu_ref, wd_ref, nf_ref,
                          y_ref, act_ref, mix_ref, s_ref, *, tiles_per_seq):
    g = pl.program_id(0)
    mix_cur = mix_ref.at[g % 2]
    mix_prev = mix_ref.at[1 - g % 2]

    @pl.when(g % tiles_per_seq == 0)
    def _():
        s_ref[...] = jnp.zeros_like(s_ref)

    tpos = jnp.minimum(g, pl.num_programs(0) - 2) % tiles_per_seq
    gain = gnorm_ref[...]

    def emit_mix(row0, h, o):
        rows = slice(row0, row0 + GLA_CHUNK)
        cols = slice(h * GLA_DV, (h + 1) * GLA_DV)
        o = _rms(o + ob_ref[rows, cols], gain) * _silu(gg_ref[rows, cols])
        mix_cur[rows, ATTN_Q + h * GLA_DV:ATTN_Q + (h + 1) * GLA_DV] = o.astype(BF16)

    def outproj_ffn2(side):
        side_per_tick = -(-(ATTN_STAGES_PER_TILE + GLA_STAGES_PER_TILE) // _ffn_ticks(FF_CHUNK_FFN2))
        x2_slabs = [x1_ref[r0:r0 + ROW_SLAB, :]
                    + jnp.dot(mix_prev[r0:r0 + ROW_SLAB, :], wout_ref[...], preferred_element_type=F32)
                    for r0 in range(0, ROW_TILE, ROW_SLAB)]
        x3 = _swiglu_residual(x2_slabs, n2_ref, wg_ref, wu_ref, wd_ref, act_ref, FF_CHUNK_FFN2, side, side_per_tick)
        for r0 in range(0, ROW_TILE, ROW_SLAB):
            y_ref[r0:r0 + ROW_SLAB, :] = _rms(x3[r0:r0 + ROW_SLAB, :], nf_ref[...])

    def step(with_mixer, with_ffn):
        side = iter(())
        if with_mixer:
            side = _round_robin(
                _attention_tile_stages(sink_ref, qa_ref, kp_ref, kc_ref, kn_ref, vp_ref, vc_ref, vn_ref, mix_cur,
                                       tpos, tiles_per_seq),
                *_gla_tile_streams(gq_ref, gk_ref, gv_ref, r_ref, wdec_ref, bdec_ref, tri_ref, s_ref, False,
                                   emit_mix))
        if with_ffn:
            outproj_ffn2(side)
        for _ in side:
            pass

    pl.when(g == 0)(lambda: step(True, False))
    pl.when(g > 0)(lambda: step(True, True))


def _forward_sweep(seq_len, sink, qa, ka, va, gq, gk, gv, gg, r, ob, wdec, bdec, tri_lo, gnorm,
                   x1, wout, n2, wg, wu, wd, nf):
    n_rows = x1.shape[0]
    tm = ROW_TILE
    nt = seq_len // tm
    n_tiles = n_rows // tm
    sub = tm // ATTN_BLOCK
    halo_per_seq = seq_len // ATTN_BLOCK
    cur = lambda g: jnp.minimum(g, n_tiles - 1)
    lag = lambda g: jnp.maximum(g - 1, 0)
    row = lambda w: pl.BlockSpec((tm, w), lambda g: (cur(g), 0))
    lag_row = lambda w: pl.BlockSpec((tm, w), lambda g: (lag(g), 0))

    def prev_idx(g):
        t = cur(g)
        return jnp.maximum(t * sub - 1, (t // nt) * halo_per_seq)

    def next_idx(g):
        t = cur(g)
        return jnp.minimum((t + 1) * sub, (t // nt + 1) * halo_per_seq - 1)

    prev = pl.BlockSpec((ATTN_BLOCK, ATTN_KV), lambda g: (prev_idx(g), 0))
    nxt = pl.BlockSpec((ATTN_BLOCK, ATTN_KV), lambda g: (next_idx(g), 0))
    prev_t = pl.BlockSpec((ATTN_KV, ATTN_BLOCK), lambda g: (0, prev_idx(g)))
    cur_t = pl.BlockSpec((ATTN_KV, tm), lambda g: (0, cur(g)))
    nxt_t = pl.BlockSpec((ATTN_KV, ATTN_BLOCK), lambda g: (0, next_idx(g)))
    return pl.pallas_call(
        functools.partial(_forward_sweep_kernel, tiles_per_seq=nt),
        grid=(n_tiles + 1,),
        in_specs=[pl.BlockSpec(memory_space=pltpu.SMEM),
                  row(ATTN_Q), prev, row(ATTN_KV), nxt, prev_t, cur_t, nxt_t,
                  row(GLA_QK), row(GLA_QK), row(GLA_V), row(GLA_V), row(2 * GLA_RANK), row(GLA_V),
                  _const_spec((2 * GLA_RANK, GLA_QK)), _const_spec((1, GLA_QK)), _const_spec((GLA_BLOCK, GLA_BLOCK)),
                  _const_spec((1, GLA_DV)),
                  lag_row(D_MODEL), _const_spec((D_MODEL, D_MODEL)), _const_spec((1, D_MODEL)),
                  _const_spec((D_MODEL, D_FF)), _const_spec((D_MODEL, D_FF)), _const_spec((D_FF, D_MODEL)),
                  _const_spec((1, D_MODEL))],
        out_specs=lag_row(D_MODEL),
        out_shape=jax.ShapeDtypeStruct((n_rows, D_MODEL), F32),
        scratch_shapes=[pltpu.VMEM((tm, D_FF), BF16), pltpu.VMEM((2, tm, D_MODEL), BF16),
                        pltpu.VMEM((GLA_QK, GLA_DV), F32)],
        compiler_params=pltpu.CompilerParams(dimension_semantics=("arbitrary",), vmem_limit_bytes=VMEM_LIMIT),
        name="forward_sweep",
    )(sink, qa, ka, ka, ka, va, va, va, gq, gk, gv, gg, r, ob, wdec, bdec, tri_lo, gnorm,
      x1, wout, n2, wg, wu, wd, nf)


def _rope_tables(seq_len):
    half = HEAD_DIM // 2
    inv_freq = (np.float32(ROPE_THETA) ** (-np.arange(half, dtype=np.float32) / np.float32(half))).astype(np.float32)
    ang = np.arange(seq_len, dtype=np.float32)[:, None] * inv_freq[None, :]
    cos, sin = np.cos(ang), np.sin(ang)
    cos_tab = np.tile(cos, (1, LANE // half))
    sin_tab = np.tile(np.concatenate([-sin, sin], axis=1), (1, LANE // HEAD_DIM))
    return jnp.asarray(cos_tab, dtype=F32), jnp.asarray(sin_tab, dtype=F32)


def _chunk_tri(bt, upper):
    i = np.arange(bt)[:, None]
    j = np.arange(bt)[None, :]
    same_chunk = (i // GLA_CHUNK) == (j // GLA_CHUNK)
    keep = (j >= i) if upper else (j <= i)
    return jnp.asarray(same_chunk & keep, dtype=BF16)


def _trunk(x, p):
    batch, seq_len, _ = x.shape
    cos_tab, sin_tab = _rope_tables(seq_len)
    x2d = x.reshape(batch * seq_len, D_MODEL)
    x1, qa, ka, va, gq, gk, gv, gg, r, ob = _reverse_sweep(
        x2d, seq_len, p["n1"], p["wg1"], p["wu1"], p["wd1"], p["nm"], p["win"], cos_tab, sin_tab,
        p["wdec_b"], p["bdec_b"], _chunk_tri(GLA_BLOCK, True))
    y = _forward_sweep(seq_len, p["sink"], qa, ka, va, gq, gk, gv, gg, r, ob,
                       p["wdec_f"], p["bdec_f"], _chunk_tri(GLA_BLOCK, False), p["gnorm"],
                       x1, p["wout"], p["n2"], p["wg2"], p["wu2"], p["wd2"], p["nf"])
    return y.reshape(batch, seq_len, D_MODEL)


def kernel(x_prompt, x_sample, norm_ffn1, w_ffn1_gate, w_ffn1_up, w_ffn1_down, norm_mix, w_in, attn_sink, w_gla_decay_fwd, b_gla_decay_fwd, w_gla_decay_bwd, b_gla_decay_bwd, gla_out_norm, w_out, norm_ffn2, w_ffn2_gate, w_ffn2_up, w_ffn2_down, norm_final):
    assert norm_ffn1.shape[0] == 1, "single-layer trunk"
    zeros_rank = jnp.zeros((GLA_RANK, GLA_QK), F32)
    p = dict(
        n1=norm_ffn1[0][None, :], wg1=w_ffn1_gate[0].astype(BF16), wu1=w_ffn1_up[0].astype(BF16),
        wd1=w_ffn1_down[0].astype(BF16),
        nm=norm_mix[0][None, :],
        win=w_in[0].astype(BF16),
        sink=attn_sink[0],
        wdec_f=jnp.concatenate([w_gla_decay_fwd[0], zeros_rank], axis=0).astype(BF16),
        bdec_f=b_gla_decay_fwd[0][None, :],
        wdec_b=jnp.concatenate([zeros_rank, w_gla_decay_bwd[0]], axis=0).astype(BF16),
        bdec_b=b_gla_decay_bwd[0][None, :],
        gnorm=gla_out_norm[0][None, :],
        wout=jnp.concatenate([
            w_out[0][:ATTN_Q].reshape(N_KV_HEADS, ATTN_GROUP, HEAD_DIM, D_MODEL).transpose(1, 0, 2, 3)
            .reshape(ATTN_Q, D_MODEL), w_out[0][ATTN_Q:]], axis=0).astype(BF16),
        n2=norm_ffn2[0][None, :], wg2=w_ffn2_gate[0].astype(BF16), wu2=w_ffn2_up[0].astype(BF16),
        wd2=w_ffn2_down[0].astype(BF16),
        nf=norm_final[None, :],
    )
    return _trunk(x_prompt, p), _trunk(x_sample, p)
```

```python
import functools

import jax
import jax.numpy as jnp
import numpy as np
from jax import lax
from jax.experimental import pallas as pl
from jax.experimental.pallas import tpu as pltpu

F32 = jnp.float32
BF16 = jnp.bfloat16

D_MODEL = 1024
D_FF = 2816
EPS = 1e-6
N_ATTN_HEADS = 8
N_KV_HEADS = 2
ATTN_GROUP = N_ATTN_HEADS // N_KV_HEADS
HEAD_DIM = 64
ATTN_BLOCK = 128
ROPE_THETA = 10000.0
N_GLA_HEADS = 4
GLA_DK = 64
GLA_DV = 128
GLA_RANK = 16
GLA_GATE_NORMALIZER = 16.0
GLA_CHUNK = 64
ATTN_Q = N_ATTN_HEADS * HEAD_DIM
ATTN_KV = N_KV_HEADS * HEAD_DIM
GLA_QK = N_GLA_HEADS * GLA_DK
GLA_V = N_GLA_HEADS * GLA_DV
IN_PROJ_WIDTH = ATTN_Q + 2 * ATTN_KV + 2 * GLA_QK + 2 * GLA_V + 2 * GLA_RANK
LANE = 128
MXU_COLS = 256
OFF_AQ = 0
OFF_AK = OFF_AQ + ATTN_Q
OFF_AV = OFF_AK + ATTN_KV
OFF_GQ = OFF_AV + ATTN_KV
OFF_GK = OFF_GQ + GLA_QK
OFF_GV = OFF_GK + GLA_QK
OFF_GG = OFF_GV + GLA_V
OFF_R = OFF_GG + GLA_V

ROW_TILE = 512
ROW_SLAB = 256
FF_CHUNK_FFN1 = 768
FF_CHUNK_FFN2 = 256
IN_PROJ_COLS = MXU_COLS
IN_PROJ_DOTS = ATTN_Q // IN_PROJ_COLS + 3 + 2 * (GLA_V // IN_PROJ_COLS) + 1
GLA_BLOCK = 256
VMEM_LIMIT = 56 * 1024 * 1024


def _rms(x, gain):
    return x * lax.rsqrt(jnp.mean(x * x, axis=-1, keepdims=True) + EPS) * gain


def _silu(x):
    half = 0.5 * x
    return half + half * jnp.tanh(half)


def _advance(side, n):
    for _ in range(n):
        next(side, None)


def _ffn_ticks(ff_chunk):
    return 1 + -(-D_FF // ff_chunk) + D_MODEL // MXU_COLS


def _swiglu_residual(x_slabs, gain_ref, wg_ref, wu_ref, wd_ref, act_ref, ff_chunk, side, side_per_tick):
    bounds = list(range(0, D_FF, ff_chunk)) + [D_FF]
    _advance(side, side_per_tick)
    gain = gain_ref[...]
    hs = [_rms(x, gain).astype(BF16) for x in x_slabs]

    def up_chunk(h, rows, c):
        sl = slice(bounds[c], bounds[c + 1])
        g = jnp.dot(h, wg_ref[:, sl], preferred_element_type=F32)
        u = jnp.dot(h, wu_ref[:, sl], preferred_element_type=F32)
        act_ref[rows, sl] = (_silu(g) * u).astype(BF16)

    row0 = 0
    for h in hs:
        up_chunk(h, slice(row0, row0 + h.shape[0]), 0)
        row0 += h.shape[0]
    _advance(side, side_per_tick)
    h = jnp.concatenate(hs, axis=0)
    for c in range(1, len(bounds) - 1):
        up_chunk(h, slice(0, row0), c)
        _advance(side, side_per_tick)
    x = jnp.concatenate(x_slabs, axis=0)
    out = []
    for j in range(D_MODEL // MXU_COLS):
        cols = slice(j * MXU_COLS, (j + 1) * MXU_COLS)
        y = jnp.dot(act_ref[...], wd_ref[:, cols], preferred_element_type=F32)
        out.append(x[:, cols] + 0.5 * y)
        _advance(side, side_per_tick)
    return jnp.concatenate(out, axis=1)


def _rope_pair(x, cos, sin_signed, first_half):
    swapped = jnp.where(first_half, pltpu.roll(x, LANE - HEAD_DIM // 2, 1), pltpu.roll(x, HEAD_DIM // 2, 1))
    return x * cos + swapped * sin_signed


GLA_CHUNKS_PER_BLOCK = GLA_BLOCK // GLA_CHUNK
GLA_BLOCKS_PER_TILE = ROW_TILE // GLA_BLOCK
GLA_STAGES_PER_TILE = GLA_BLOCKS_PER_TILE * (4 + 2 * GLA_CHUNKS_PER_BLOCK)


def _round_robin(*stage_generators):
    live = list(stage_generators)
    while live:
        for gen in list(live):
            try:
                next(gen)
                yield
            except StopIteration:
                live.remove(gen)


def _gla_block_stages(q_ref, k_ref, v_ref, r_ref, brows, wdec_ref, bdec_ref, tri_ref, s_ref, reverse, emit):
    bt = brows.stop - brows.start
    nch = bt // GLA_CHUNK
    z = jnp.dot(r_ref[brows, :], wdec_ref[...], preferred_element_type=F32) + bdec_ref[...]
    yield
    log_a = (jnp.minimum(z, 0.0) - jnp.log1p(jnp.exp(-jnp.abs(z)))) * (1.0 / GLA_GATE_NORMALIZER)
    hi = log_a.astype(BF16)
    lo = (log_a - hi.astype(F32)).astype(BF16)
    tri = tri_ref[...]
    cum = jnp.dot(tri, hi, preferred_element_type=F32) + jnp.dot(tri, lo, preferred_element_type=F32)
    yield
    q = q_ref[brows, :]
    k = k_ref[brows, :]
    cum3 = cum.reshape(nch, GLA_CHUNK, GLA_QK)
    edge = GLA_CHUNK - 1 if not reverse else 0
    tot3 = cum3[:, edge:edge + 1, :]
    rest = (tot3 - cum3).reshape(bt, GLA_QK)
    qe = (q * (GLA_DK ** -0.5) * jnp.exp(cum)).astype(BF16)
    ke = (k * jnp.exp(-cum)).astype(BF16)
    ks_t = (k * jnp.exp(rest)).T.astype(BF16)
    tot = tot3.reshape(nch, GLA_QK)
    tot_t = jnp.concatenate([tot, jnp.zeros((LANE - nch, GLA_QK), F32)], axis=0).T
    decay_t = jnp.exp(tot_t)

    ii = lax.broadcasted_iota(jnp.int32, (GLA_CHUNK, GLA_CHUNK), 0)
    jj = lax.broadcasted_iota(jnp.int32, (GLA_CHUNK, GLA_CHUNK), 1)
    keep = (jj > ii) if reverse else (jj <= ii)
    rows = [slice(n * GLA_CHUNK, (n + 1) * GLA_CHUNK) for n in range(nch)]
    klanes = [slice(h * GLA_DK, (h + 1) * GLA_DK) for h in range(N_GLA_HEADS)]

    def v_of(n, h):
        return v_ref[brows.start + n * GLA_CHUNK:brows.start + (n + 1) * GLA_CHUNK, h * GLA_DV:(h + 1) * GLA_DV]

    yield

    a = [[None] * N_GLA_HEADS for _ in range(nch)]
    u = [None] * nch
    for n in range(nch):
        for h in range(N_GLA_HEADS):
            s_nh = lax.dot_general(qe[rows[n], klanes[h]], ke[rows[n], klanes[h]], (((1,), (1,)), ((), ())),
                                   preferred_element_type=F32)
            a[n][h] = jnp.where(keep, s_nh, 0.0).astype(BF16)
        u[n] = jnp.concatenate([jnp.dot(ks_t[klanes[h], rows[n]], v_of(n, h), preferred_element_type=F32)
                                for h in range(N_GLA_HEADS)], axis=0)
        yield
    s = s_ref[...]
    s_in = [None] * nch
    for n in (range(nch - 1, -1, -1) if reverse else range(nch)):
        s_in[n] = s.astype(BF16)
        s = decay_t[:, n:n + 1] * s + u[n]
    s_ref[...] = s
    yield
    for n in range(nch):
        for h in range(N_GLA_HEADS):
            emit(n, h, jnp.dot(a[n][h], v_of(n, h), preferred_element_type=F32)
                 + jnp.dot(qe[rows[n], klanes[h]], s_in[n][klanes[h], :], preferred_element_type=F32))
        yield


def _gla_tile_streams(q_ref, k_ref, v_ref, r_ref, wdec_ref, bdec_ref, tri_ref, s_ref, reverse, emit):
    streams = []
    for blk in (range(GLA_BLOCKS_PER_TILE - 1, -1, -1) if reverse else range(GLA_BLOCKS_PER_TILE)):
        brows = slice(blk * GLA_BLOCK, (blk + 1) * GLA_BLOCK)
        emit_block = lambda n, h, o, base=blk * GLA_BLOCK: emit(base + n * GLA_CHUNK, h, o)
        streams.append(_gla_block_stages(q_ref, k_ref, v_ref, r_ref, brows, wdec_ref, bdec_ref, tri_ref, s_ref,
                                         reverse, emit_block))
    return streams


def _reverse_sweep_kernel(x_ref, n1_ref, wg_ref, wu_ref, wd_ref, nm_ref, win_ref, cos_ref, sin_ref,
                          wdec_ref, bdec_ref, tri_ref,
                          x1_ref, qa_ref, ka_ref, va_ref, gq_ref, gk_ref, gv_ref, gg_ref, r_ref, ob_ref,
                          act_ref, pq_ref, pk_ref, pv_ref, pr_ref, s_ref, *, tiles_per_seq):
    g = pl.program_id(0)
    last = pl.num_programs(0) - 1
    cur_slot = g % 2
    prev_slot = 1 - cur_slot

    @pl.when(jnp.maximum(g - 1, 0) % tiles_per_seq == 0)
    def _():
        s_ref[...] = jnp.zeros_like(s_ref)

    def emit_ob(row0, h, o):
        ob_ref[row0:row0 + GLA_CHUNK, h * GLA_DV:(h + 1) * GLA_DV] = o

    def ffn1_inproj(side):
        x_slabs = [x_ref[r0:r0 + ROW_SLAB, :] for r0 in range(0, ROW_TILE, ROW_SLAB)]
        side_per_tick = -(-(GLA_STAGES_PER_TILE - IN_PROJ_DOTS) // _ffn_ticks(FF_CHUNK_FFN1))
        x1 = _swiglu_residual(x_slabs, n1_ref, wg_ref, wu_ref, wd_ref, act_ref, FF_CHUNK_FFN1, side, side_per_tick)
        x1_ref[...] = x1
        h = jnp.concatenate([_rms(x1[r0:r0 + ROW_SLAB, :], nm_ref[...]).astype(BF16)
                             for r0 in range(0, ROW_TILE, ROW_SLAB)], axis=0)
        cos = cos_ref[...]
        sin = sin_ref[...]
        lane = lax.broadcasted_iota(jnp.int32, (1, LANE), 1)
        first_half = (lane % HEAD_DIM) < (HEAD_DIM // 2)

        def proj(off, width):
            _advance(side, 1)
            return jnp.dot(h, win_ref[:, off:off + width], preferred_element_type=F32)

        scale = HEAD_DIM ** -0.5
        for j in range(ATTN_Q // IN_PROJ_COLS):
            q2 = proj(OFF_AQ + j * IN_PROJ_COLS, IN_PROJ_COLS)
            for i in range(IN_PROJ_COLS // LANE):
                q = _rope_pair(q2[:, i * LANE:(i + 1) * LANE], cos, sin, first_half)
                qa_ref[:, j * IN_PROJ_COLS + i * LANE:j * IN_PROJ_COLS + (i + 1) * LANE] = (q * scale).astype(BF16)
        kv = proj(OFF_AK, 2 * ATTN_KV)
        ka_ref[...] = _rope_pair(kv[:, :ATTN_KV], cos, sin, first_half).astype(BF16)
        va_ref[...] = kv[:, ATTN_KV:].T.astype(BF16)
        gq = proj(OFF_GQ, GLA_QK)
        gq_ref[...] = gq
        pq_ref[cur_slot] = gq
        gk = proj(OFF_GK, GLA_QK)
        gk_ref[...] = gk
        pk_ref[cur_slot] = gk
        for j in range(GLA_V // IN_PROJ_COLS):
            cols = slice(j * IN_PROJ_COLS, (j + 1) * IN_PROJ_COLS)
            gv = proj(OFF_GV + j * IN_PROJ_COLS, IN_PROJ_COLS).astype(BF16)
            gv_ref[:, cols] = gv
            pv_ref[cur_slot, :, cols] = gv
            gg_ref[:, cols] = proj(OFF_GG + j * IN_PROJ_COLS, IN_PROJ_COLS)
        rr = proj(OFF_R, 2 * GLA_RANK).astype(BF16)
        r_ref[...] = rr
        pr_ref[cur_slot] = rr

    def step(with_ffn, with_scan):
        side = iter(())
        if with_scan:
            side = _round_robin(*_gla_tile_streams(pq_ref.at[prev_slot], pk_ref.at[prev_slot], pv_ref.at[prev_slot],
                                                   pr_ref.at[prev_slot], wdec_ref, bdec_ref, tri_ref, s_ref, True,
                                                   emit_ob))
        if with_ffn:
            ffn1_inproj(side)
        for _ in side:
            pass

    @pl.when(g == 0)
    def _():
        pq_ref[...] = jnp.zeros_like(pq_ref)
        pk_ref[...] = jnp.zeros_like(pk_ref)
        pv_ref[...] = jnp.zeros_like(pv_ref)
        pr_ref[...] = jnp.zeros_like(pr_ref)

    pl.when(g < last)(lambda: step(True, True))
    pl.when(g == last)(lambda: step(False, True))


def _const_spec(shape):
    return pl.BlockSpec(shape, lambda *_: (0,) * len(shape), pipeline_mode=pl.Buffered(1))


def _reverse_sweep(x2d, seq_len, n1, wg, wu, wd, nm, win, cos_tab, sin_tab, wdec, bdec, tri_up):
    n_rows = x2d.shape[0]
    tm = ROW_TILE
    assert n_rows % tm == 0 and seq_len % tm == 0
    nt = seq_len // tm
    n_tiles = n_rows // tm

    def tile_of(step):
        return (step // nt) * nt + (nt - 1 - step % nt)

    cur = lambda g: tile_of(jnp.minimum(g, n_tiles - 1))
    lag = lambda g: tile_of(jnp.maximum(g - 1, 0))
    row = lambda w: pl.BlockSpec((tm, w), lambda g: (cur(g), 0))
    rope = pl.BlockSpec((tm, LANE), lambda g: (nt - 1 - jnp.minimum(g, n_tiles - 1) % nt, 0))
    out_shapes = (
        jax.ShapeDtypeStruct((n_rows, D_MODEL), F32),
        jax.ShapeDtypeStruct((n_rows, ATTN_Q), BF16),
        jax.ShapeDtypeStruct((n_rows, ATTN_KV), BF16),
        jax.ShapeDtypeStruct((ATTN_KV, n_rows), BF16),
        jax.ShapeDtypeStruct((n_rows, GLA_QK), F32),
        jax.ShapeDtypeStruct((n_rows, GLA_QK), F32),
        jax.ShapeDtypeStruct((n_rows, GLA_V), BF16),
        jax.ShapeDtypeStruct((n_rows, GLA_V), F32),
        jax.ShapeDtypeStruct((n_rows, 2 * GLA_RANK), BF16),
        jax.ShapeDtypeStruct((n_rows, GLA_V), F32),
    )
    return pl.pallas_call(
        functools.partial(_reverse_sweep_kernel, tiles_per_seq=nt),
        grid=(n_tiles + 1,),
        in_specs=[
            row(D_MODEL),
            _const_spec((1, D_MODEL)),
            _const_spec((D_MODEL, D_FF)), _const_spec((D_MODEL, D_FF)), _const_spec((D_FF, D_MODEL)),
            _const_spec((1, D_MODEL)),
            _const_spec((D_MODEL, IN_PROJ_WIDTH)),
            rope, rope,
            _const_spec((2 * GLA_RANK, GLA_QK)), _const_spec((1, GLA_QK)), _const_spec((GLA_BLOCK, GLA_BLOCK)),
        ],
        out_specs=[row(D_MODEL), row(ATTN_Q), row(ATTN_KV), pl.BlockSpec((ATTN_KV, tm), lambda g: (0, cur(g))),
                   row(GLA_QK), row(GLA_QK),
                   row(GLA_V), row(GLA_V), row(2 * GLA_RANK),
                   pl.BlockSpec((tm, GLA_V), lambda g: (lag(g), 0))],
        out_shape=out_shapes,
        scratch_shapes=[pltpu.VMEM((tm, D_FF), BF16),
                        pltpu.VMEM((2, tm, GLA_QK), F32), pltpu.VMEM((2, tm, GLA_QK), F32),
                        pltpu.VMEM((2, tm, GLA_V), BF16), pltpu.VMEM((2, tm, 2 * GLA_RANK), BF16),
                        pltpu.VMEM((GLA_QK, GLA_DV), F32)],
        compiler_params=pltpu.CompilerParams(dimension_semantics=("arbitrary",), vmem_limit_bytes=VMEM_LIMIT),
        name="reverse_sweep",
    )(x2d, n1, wg, wu, wd, nm, win, cos_tab, sin_tab, wdec, bdec, tri_up)


ATTN_UNITS_PER_TILE = (ROW_TILE // ATTN_BLOCK) * N_KV_HEADS
ATTN_STAGES_PER_TILE = 3 * ATTN_UNITS_PER_TILE


def _attention_tile_stages(sink_ref, qa_ref, kp_ref, kc_ref, kn_ref, vp_ref, vc_ref, vn_ref, mix_ref, tpos,
                           tiles_per_seq):
    sub = ROW_TILE // ATTN_BLOCK
    n_qblocks = tiles_per_seq * sub
    n_keys = 3 * ATTN_BLOCK
    n_cols = ATTN_GROUP * ATTN_BLOCK
    kbuf = jnp.concatenate([kp_ref[...], kc_ref[...], kn_ref[...]], axis=0)
    vbuf_t = jnp.concatenate([vp_ref[...], vc_ref[...], vn_ref[...]], axis=1)
    kj = lax.broadcasted_iota(jnp.int32, (n_keys, n_cols), 0)
    col = lax.broadcasted_iota(jnp.int32, (n_keys, n_cols), 1)
    qi = col % ATTN_BLOCK
    in_window = (kj >= qi) & (kj <= qi + 2 * ATTN_BLOCK)
    head_of_col = lax.broadcasted_iota(jnp.int32, (1, n_cols), 1) // ATTN_BLOCK
    pending = {}

    def unit(jb, kv):
        qblk = tpos * sub + jb
        qrows = slice(jb * ATTN_BLOCK, (jb + 1) * ATTN_BLOCK)
        krows = slice(jb * ATTN_BLOCK, (jb + 3) * ATTN_BLOCK)
        kvl = slice(kv * HEAD_DIM, (kv + 1) * HEAD_DIM)
        heads = range(kv * ATTN_GROUP, (kv + 1) * ATTN_GROUP)
        qs = jnp.concatenate([qa_ref[qrows, h * HEAD_DIM:(h + 1) * HEAD_DIM] for h in heads], axis=0)
        s_t = lax.dot_general(kbuf[krows, kvl], qs, (((1,), (1,)), ((), ())), preferred_element_type=F32)
        yield
        mask = in_window
        if jb == 0:
            mask = mask & ((kj >= ATTN_BLOCK) | (qblk > 0))
        if jb == sub - 1:
            mask = mask & ((kj < 2 * ATTN_BLOCK) | (qblk < n_qblocks - 1))
        s_t = jnp.where(mask, s_t, -1e30)
        sink = jnp.full((1, n_cols), sink_ref[heads[-1]], F32)
        for hl in range(ATTN_GROUP - 2, -1, -1):
            sink = jnp.where(head_of_col == hl, sink_ref[heads[hl]], sink)
        m = jnp.maximum(jnp.max(s_t, axis=0, keepdims=True), sink)
        p = jnp.exp(s_t - m)
        denom = jnp.sum(p, axis=0, keepdims=True) + jnp.exp(sink - m)
        p_t = p.astype(BF16)
        yield
        pending[kv] = jnp.dot(vbuf_t[kvl, krows], p_t, preferred_element_type=F32) / denom
        if kv == N_KV_HEADS - 1:
            o_t = jnp.concatenate([pending.pop(i) for i in range(N_KV_HEADS)], axis=0)
            for hl in range(ATTN_GROUP):
                cols = slice(hl * ATTN_BLOCK, (hl + 1) * ATTN_BLOCK)
                mix_ref[qrows, hl * ATTN_KV:(hl + 1) * ATTN_KV] = o_t[:, cols].T.astype(BF16)
        yield

    units = [unit(jb, kv) for jb in range(sub) for kv in range(N_KV_HEADS)]
    for slot in range(len(units) + 4):
        for stage in range(3):
            u = slot - 2 * stage
            if 0 <= u < len(units):
                next(units[u])
                yield


def _forward_sweep_kernel(sink_ref, qa_ref, kp_ref, kc_ref, kn_ref, vp_ref, vc_ref, vn_ref,
                          gq_ref, gk_ref, gv_ref, gg_ref, r_ref, ob_ref, wdec_ref, bdec_ref, tri_ref, gnorm_ref,
                          x1_ref, wout_ref, n2_ref, wg_ref, wu_ref, wd_ref, nf_ref,
                          y_ref, act_ref, mix_ref, s_ref, *, tiles_per_seq):
    g = pl.program_id(0)
    mix_cur = mix_ref.at[g % 2]
    mix_prev = mix_ref.at[1 - g % 2]

    @pl.when(g % tiles_per_seq == 0)
    def _():
        s_ref[...] = jnp.zeros_like(s_ref)

    tpos = jnp.minimum(g, pl.num_programs(0) - 2) % tiles_per_seq
    gain = gnorm_ref[...]

    def emit_mix(row0, h, o):
        rows = slice(row0, row0 + GLA_CHUNK)
        cols = slice(h * GLA_DV, (h + 1) * GLA_DV)
        o = _rms(o + ob_ref[rows, cols], gain) * _silu(gg_ref[rows, cols])
        mix_cur[rows, ATTN_Q + h * GLA_DV:ATTN_Q + (h + 1) * GLA_DV] = o.astype(BF16)

    def outproj_ffn2(side):
        side_per_tick = -(-(ATTN_STAGES_PER_TILE + GLA_STAGES_PER_TILE) // _ffn_ticks(FF_CHUNK_FFN2))
        x2_slabs = [x1_ref[r0:r0 + ROW_SLAB, :]
                    + jnp.dot(mix_prev[r0:r0 + ROW_SLAB, :], wout_ref[...], preferred_element_type=F32)
                    for r0 in range(0, ROW_TILE, ROW_SLAB)]
        x3 = _swiglu_residual(x2_slabs, n2_ref, wg_ref, wu_ref, wd_ref, act_ref, FF_CHUNK_FFN2, side, side_per_tick)
        for r0 in range(0, ROW_TILE, ROW_SLAB):
            y_ref[r0:r0 + ROW_SLAB, :] = _rms(x3[r0:r0 + ROW_SLAB, :], nf_ref[...])

    def step(with_mixer, with_ffn):
        side = iter(())
        if with_mixer:
            side = _round_robin(
                _attention_tile_stages(sink_ref, qa_ref, kp_ref, kc_ref, kn_ref, vp_ref, vc_ref, vn_ref, mix_cur,
                                       tpos, tiles_per_seq),
                *_gla_tile_streams(gq_ref, gk_ref, gv_ref, r_ref, wdec_ref, bdec_ref, tri_ref, s_ref, False,
                                   emit_mix))
        if with_ffn:
            outproj_ffn2(side)
        for _ in side:
            pass

    pl.when(g == 0)(lambda: step(True, False))
    pl.when(g > 0)(lambda: step(True, True))


def _forward_sweep(seq_len, sink, qa, ka, va, gq, gk, gv, gg, r, ob, wdec, bdec, tri_lo, gnorm,
                   x1, wout, n2, wg, wu, wd, nf):
    n_rows = x1.shape[0]
    tm = ROW_TILE
    nt = seq_len // tm
    n_tiles = n_rows // tm
    sub = tm // ATTN_BLOCK
    halo_per_seq = seq_len // ATTN_BLOCK
    cur = lambda g: jnp.minimum(g, n_tiles - 1)
    lag = lambda g: jnp.maximum(g - 1, 0)
    row = lambda w: pl.BlockSpec((tm, w), lambda g: (cur(g), 0))
    lag_row = lambda w: pl.BlockSpec((tm, w), lambda g: (lag(g), 0))

    def prev_idx(g):
        t = cur(g)
        return jnp.maximum(t * sub - 1, (t // nt) * halo_per_seq)

    def next_idx(g):
        t = cur(g)
        return jnp.minimum((t + 1) * sub, (t // nt + 1) * halo_per_seq - 1)

    prev = pl.BlockSpec((ATTN_BLOCK, ATTN_KV), lambda g: (prev_idx(g), 0))
    nxt = pl.BlockSpec((ATTN_BLOCK, ATTN_KV), lambda g: (next_idx(g), 0))
    prev_t = pl.BlockSpec((ATTN_KV, ATTN_BLOCK), lambda g: (0, prev_idx(g)))
    cur_t = pl.BlockSpec((ATTN_KV, tm), lambda g: (0, cur(g)))
    nxt_t = pl.BlockSpec((ATTN_KV, ATTN_BLOCK), lambda g: (0, next_idx(g)))
    return pl.pallas_call(
        functools.partial(_forward_sweep_kernel, tiles_per_seq=nt),
        grid=(n_tiles + 1,),
        in_specs=[pl.BlockSpec(memory_space=pltpu.SMEM),
                  row(ATTN_Q), prev, row(ATTN_KV), nxt, prev_t, cur_t, nxt_t,
                  row(GLA_QK), row(GLA_QK), row(GLA_V), row(GLA_V), row(2 * GLA_RANK), row(GLA_V),
                  _const_spec((2 * GLA_RANK, GLA_QK)), _const_spec((1, GLA_QK)), _const_spec((GLA_BLOCK, GLA_BLOCK)),
                  _const_spec((1, GLA_DV)),
                  lag_row(D_MODEL), _const_spec((D_MODEL, D_MODEL)), _const_spec((1, D_MODEL)),
                  _const_spec((D_MODEL, D_FF)), _const_spec((D_MODEL, D_FF)), _const_spec((D_FF, D_MODEL)),
                  _const_spec((1, D_MODEL))],
        out_specs=lag_row(D_MODEL),
        out_shape=jax.ShapeDtypeStruct((n_rows, D_MODEL), F32),
        scratch_shapes=[pltpu.VMEM((tm, D_FF), BF16), pltpu.VMEM((2, tm, D_MODEL), BF16),
                        pltpu.VMEM((GLA_QK, GLA_DV), F32)],
        compiler_params=pltpu.CompilerParams(dimension_semantics=("arbitrary",), vmem_limit_bytes=VMEM_LIMIT),
        name="forward_sweep",
    )(sink, qa, ka, ka, ka, va, va, va, gq, gk, gv, gg, r, ob, wdec, bdec, tri_lo, gnorm,
      x1, wout, n2, wg, wu, wd, nf)


def _rope_tables(seq_len):
    half = HEAD_DIM // 2
    inv_freq = (np.float32(ROPE_THETA) ** (-np.arange(half, dtype=np.float32) / np.float32(half))).astype(np.float32)
    ang = np.arange(seq_len, dtype=np.float32)[:, None] * inv_freq[None, :]
    cos, sin = np.cos(ang), np.sin(ang)
    cos_tab = np.tile(cos, (1, LANE // half))
    sin_tab = np.tile(np.concatenate([-sin, sin], axis=1), (1, LANE // HEAD_DIM))
    return jnp.asarray(cos_tab, dtype=F32), jnp.asarray(sin_tab, dtype=F32)


def _chunk_tri(bt, upper):
    i = np.arange(bt)[:, None]
    j = np.arange(bt)[None, :]
    same_chunk = (i // GLA_CHUNK) == (j // GLA_CHUNK)
    keep = (j >= i) if upper else (j <= i)
    return jnp.asarray(same_chunk & keep, dtype=BF16)


def _trunk(x, p):
    batch, seq_len, _ = x.shape
    cos_tab, sin_tab = _rope_tables(seq_len)
    x2d = x.reshape(batch * seq_len, D_MODEL)
    x1, qa, ka, va, gq, gk, gv, gg, r, ob = _reverse_sweep(
        x2d, seq_len, p["n1"], p["wg1"], p["wu1"], p["wd1"], p["nm"], p["win"], cos_tab, sin_tab,
        p["wdec_b"], p["bdec_b"], _chunk_tri(GLA_BLOCK, True))
    y = _forward_sweep(seq_len, p["sink"], qa, ka, va, gq, gk, gv, gg, r, ob,
                       p["wdec_f"], p["bdec_f"], _chunk_tri(GLA_BLOCK, False), p["gnorm"],
                       x1, p["wout"], p["n2"], p["wg2"], p["wu2"], p["wd2"], p["nf"])
    return y.reshape(batch, seq_len, D_MODEL)


def kernel(x_prompt, x_sample, norm_ffn1, w_ffn1_gate, w_ffn1_up, w_ffn1_down, norm_mix, w_in, attn_sink, w_gla_decay_fwd, b_gla_decay_fwd, w_gla_decay_bwd, b_gla_decay_bwd, gla_out_norm, w_out, norm_ffn2, w_ffn2_gate, w_ffn2_up, w_ffn2_down, norm_final):
    assert norm_ffn1.shape[0] == 1, "single-layer trunk"
    zeros_rank = jnp.zeros((GLA_RANK, GLA_QK), F32)
    p = dict(
        n1=norm_ffn1[0][None, :], wg1=w_ffn1_gate[0].astype(BF16), wu1=w_ffn1_up[0].astype(BF16),
        wd1=w_ffn1_down[0].astype(BF16),
        nm=norm_mix[0][None, :],
        win=w_in[0].astype(BF16),
        sink=attn_sink[0],
        wdec_f=jnp.concatenate([w_gla_decay_fwd[0], zeros_rank], axis=0).astype(BF16),
        bdec_f=b_gla_decay_fwd[0][None, :],
        wdec_b=jnp.concatenate([zeros_rank, w_gla_decay_bwd[0]], axis=0).astype(BF16),
        bdec_b=b_gla_decay_bwd[0][None, :],
        gnorm=gla_out_norm[0][None, :],
        wout=jnp.concatenate([
            w_out[0][:ATTN_Q].reshape(N_KV_HEADS, ATTN_GROUP, HEAD_DIM, D_MODEL).transpose(1, 0, 2, 3)
            .reshape(ATTN_Q, D_MODEL), w_out[0][ATTN_Q:]], axis=0).astype(BF16),
        n2=norm_ffn2[0][None, :], wg2=w_ffn2_gate[0].astype(BF16), wu2=w_ffn2_up[0].astype(BF16),
        wd2=w_ffn2_down[0].astype(BF16),
        nf=norm_final[None, :],
    )
    return _trunk(x_prompt, p), _trunk(x_sample, p)
```
